```python
import jax, jax.numpy as jnp
from jax import lax
import numpy as np

D_MODEL = 2048
BATCH = 1
SEQ = 8192
DEPTH = 4

GRID_W = 64
HEAD_DIM = 128
QBLK = 128
ROPE_THETA = 10000.0
EPS = 1e-6
MLA_HEADS = 6
MLA_Q_RANK = 384
MLA_KV_RANK = 256
MLA_NOPE = 128
MLA_ROPE = 64
MLA_V = 128
GQA_HEADS = 4
GQA_KV_HEADS = 2
NA_HEADS = 6
NA_KH = 8
NA_KW = 16
IN_SIZES = (MLA_Q_RANK, MLA_KV_RANK, MLA_ROPE,
            GQA_HEADS * HEAD_DIM, GQA_KV_HEADS * HEAD_DIM, GQA_KV_HEADS * HEAD_DIM,
            NA_HEADS * HEAD_DIM, NA_HEADS * HEAD_DIM, NA_HEADS * HEAD_DIM)
D_IN = MLA_Q_RANK + MLA_KV_RANK + MLA_ROPE + (GQA_HEADS + 2 * GQA_KV_HEADS) * HEAD_DIM + 3 * NA_HEADS * HEAD_DIM
GROUP_SIZES = (MLA_HEADS * MLA_V, GQA_HEADS * HEAD_DIM, NA_HEADS * HEAD_DIM)
MIX_WIDTH = MLA_HEADS * MLA_V + GQA_HEADS * HEAD_DIM + NA_HEADS * HEAD_DIM
N_GROUPS = 4
EXPERTS_PER_GROUP = 8
N_EXPERTS = N_GROUPS * EXPERTS_PER_GROUP
TOP_K = 2
D_EXPERT = 512
MOE_BLK = 128

kernel_name = 'hybrid_mla_gqa_natten_hmoe_encoder'


def _split_cols(a, sizes):
    out, start = [], 0
    for s in sizes:
        out.append(a[..., start:start + s])
        start += s
    return out


def rms_norm(x, g):
    xf = x.astype(jnp.float32)
    y = xf * lax.rsqrt(jnp.mean(xf * xf, axis=-1, keepdims=True) + EPS)
    return (y * g.astype(jnp.float32)).astype(x.dtype)


def rope(x, pos):
    d = x.shape[-1]
    inv = ROPE_THETA ** (-jnp.arange(0, d, 2, dtype=jnp.float32) / d)
    ang = pos.astype(jnp.float32)[:, None] * inv[None, :]
    cos = jnp.cos(ang)[:, None, :]
    sin = jnp.sin(ang)[:, None, :]
    xf = x.astype(jnp.float32)
    x1, x2 = xf[..., : d // 2], xf[..., d // 2:]
    return jnp.concatenate([x1 * cos - x2 * sin, x2 * cos + x1 * sin], axis=-1).astype(x.dtype)


def axial_rope(x, row, col):
    half = x.shape[-1] // 2
    return jnp.concatenate([rope(x[..., :half], row), rope(x[..., half:], col)], axis=-1)


def sweep_attention(q, k, v):
    B, S, Hk, G, Dq = q.shape
    nb = S // QBLK
    scale = Dq ** -0.5
    qb = jnp.moveaxis(q.reshape(B, nb, QBLK, Hk, G, Dq), 1, 0)

    def block(qblk):
        s = jnp.einsum('bqhgd,bkhd->bhgqk', qblk, k).astype(jnp.float32) * scale
        p = jax.nn.softmax(s, axis=-1).astype(v.dtype)
        return jnp.einsum('bhgqk,bkhd->bqhgd', p, v)

    o = lax.map(block, qb)
    return jnp.moveaxis(o, 0, 1).reshape(B, S, Hk * G * v.shape[-1])


def mla_mixer(a_q, a_kv, a_pe, q_norm_g, w_uq, kv_norm_g, w_ukv, qk_q_g, qk_k_g):
    B, S, _ = a_q.shape
    q = (rms_norm(a_q, q_norm_g) @ w_uq).reshape(B, S, MLA_HEADS, MLA_NOPE + MLA_ROPE)
    kv = (rms_norm(a_kv, kv_norm_g) @ w_ukv).reshape(B, S, MLA_HEADS, MLA_NOPE + MLA_V)
    k_pe = jnp.broadcast_to(a_pe[:, :, None, :], (B, S, MLA_HEADS, MLA_ROPE))
    k = jnp.concatenate([kv[..., :MLA_NOPE], k_pe], axis=-1)
    v = kv[..., MLA_NOPE:]
    q = rms_norm(q, qk_q_g)
    k = rms_norm(k, qk_k_g)
    pos = jnp.arange(S)
    q = jnp.concatenate([q[..., :MLA_NOPE], rope(q[..., MLA_NOPE:], pos)], axis=-1)
    k = jnp.concatenate([k[..., :MLA_NOPE], rope(k[..., MLA_NOPE:], pos)], axis=-1)
    return sweep_attention(q[:, :, :, None, :], k, v)


def gqa_mixer(q, k, v, q_g, k_g):
    B, S, _ = q.shape
    q = rms_norm(q.reshape(B, S, GQA_HEADS, HEAD_DIM), q_g)
    k = rms_norm(k.reshape(B, S, GQA_KV_HEADS, HEAD_DIM), k_g)
    v = v.reshape(B, S, GQA_KV_HEADS, HEAD_DIM)
    t = jnp.arange(S)
    row, col = t // GRID_W, t % GRID_W
    q = axial_rope(q, row, col)
    k = axial_rope(k, row, col)
    q = q.reshape(B, S, GQA_KV_HEADS, GQA_HEADS // GQA_KV_HEADS, HEAD_DIM)
    return sweep_attention(q, k, v)


def neighbourhood_mixer(q, k, v, q_g, k_g, rpb):
    B, S, _ = q.shape
    rows = S // GRID_W
    kh = min(NA_KH, rows)
    q = rms_norm(q.reshape(B, S, NA_HEADS, HEAD_DIM), q_g)
    k = rms_norm(k.reshape(B, S, NA_HEADS, HEAD_DIM), k_g)
    v = v.reshape(B, S, NA_HEADS, HEAD_DIM)
    r = jnp.arange(rows)
    c = jnp.arange(GRID_W)
    rs = jnp.clip(r - kh // 2, 0, rows - kh)
    cs = jnp.clip(c - NA_KW // 2, 0, GRID_W - NA_KW)
    row_idx = rs[:, None] + jnp.arange(kh)[None, :]
    K = kh * GRID_W
    kg = k.reshape(B, rows, GRID_W, NA_HEADS, HEAD_DIM)[:, row_idx].reshape(B, rows, K, NA_HEADS, HEAD_DIM)
    vg = v.reshape(B, rows, GRID_W, NA_HEADS, HEAD_DIM)[:, row_idx].reshape(B, rows, K, NA_HEADS, HEAD_DIM)
    qg = q.reshape(B, rows, GRID_W, NA_HEADS, HEAD_DIM)
    s = jnp.einsum('brqhd,brkhd->bhrqk', qg, kg).astype(jnp.float32) * (HEAD_DIM ** -0.5)
    kr = jnp.repeat(row_idx, GRID_W, axis=1)
    kc = jnp.tile(jnp.arange(GRID_W), kh)
    in_win = (kc[None, :] >= cs[:, None]) & (kc[None, :] < cs[:, None] + NA_KW)
    dr_i = (kr - r[:, None]) + NA_KH - 1
    dc_i = jnp.clip(kc[None, :] - c[:, None] + NA_KW - 1, 0, 2 * NA_KW - 2)
    bias = rpb[:, dr_i[:, None, :], dc_i[None, :, :]].astype(jnp.float32)
    s = jnp.where(in_win, s + bias[None], jnp.finfo(jnp.float32).min)
    p = jax.nn.softmax(s, axis=-1).astype(v.dtype)
    o = jnp.einsum('bhrqk,brkhd->brqhd', p, vg)
    return o.reshape(B, S, NA_HEADS * HEAD_DIM)


def group_rms(o, g):
    parts = _split_cols(o, GROUP_SIZES)
    gparts = _split_cols(g, GROUP_SIZES)
    return jnp.concatenate([rms_norm(p, gp) for p, gp in zip(parts, gparts)], axis=-1)


def hier_moe(h, w_rg, b_rg, w_re, b_re, w_gate, w_up, w_down):
    B, S, D = h.shape
    N = B * S
    xt = h.reshape(N, D)
    g_prob = jax.nn.softmax((xt @ w_rg).astype(jnp.float32) + b_rg.astype(jnp.float32), axis=-1)
    g_top = jnp.argmax(g_prob, axis=-1)
    p_g = jnp.take_along_axis(g_prob, g_top[:, None], axis=-1)
    e_logits = ((xt @ w_re).astype(jnp.float32) + b_re.astype(jnp.float32)).reshape(N, N_GROUPS, EXPERTS_PER_GROUP)
    e_in = jnp.take_along_axis(e_logits, g_top[:, None, None], axis=1)[:, 0]
    e_prob = jax.nn.softmax(e_in, axis=-1)
    w_top, i_top = lax.top_k(e_prob, TOP_K)
    w_top = w_top / jnp.sum(w_top, axis=-1, keepdims=True) * p_g
    eid = (g_top[:, None] * EXPERTS_PER_GROUP + i_top).reshape(-1)
    wts = w_top.reshape(-1)
    A = N * TOP_K
    tok = jnp.repeat(jnp.arange(N), TOP_K)
    order = jnp.argsort(eid)
    se, stok, sw = eid[order], tok[order], wts[order]
    counts = jnp.bincount(eid, length=N_EXPERTS)
    padded = ((counts + MOE_BLK - 1) // MOE_BLK) * MOE_BLK
    pad_end = jnp.cumsum(padded)
    pad_start = pad_end - padded
    start = jnp.cumsum(counts) - counts
    dest = pad_start[se] + jnp.arange(A) - start[se]
    nb = -(-(A + N_EXPERTS * (MOE_BLK - 1)) // MOE_BLK)
    P = nb * MOE_BLK
    xbuf = jnp.zeros((P, D), h.dtype).at[dest].set(xt[stok])
    blk_e = jnp.minimum(jnp.searchsorted(pad_end, jnp.arange(nb) * MOE_BLK, side='right'), N_EXPERTS - 1)

    def expert_block(args):
        xb, e = args
        return (jax.nn.silu(xb @ w_gate[e]) * (xb @ w_up[e])) @ w_down[e]

    ybuf = lax.map(expert_block, (xbuf.reshape(nb, MOE_BLK, D), blk_e)).reshape(P, D)
    y = jnp.zeros((N, D), h.dtype).at[stok].add(ybuf[dest] * sw[:, None].astype(h.dtype))
    return y.reshape(B, S, D)


def setup_inputs(seed: int = 0) -> dict:
    key = jax.random.key(seed)
    ks = iter(jax.random.split(key, 32))

    def nrm(shape, scale):
        return jax.random.normal(next(ks), shape, jnp.float32) * scale

    def gain(shape):
        return 1.0 + nrm(shape, 0.02)

    L, D = DEPTH, D_MODEL
    return {
        'x': nrm((BATCH, SEQ, D), 1.0),
        'c': nrm((BATCH, D), 1.0),
        'ada_w': nrm((L, D, 6 * D), 0.5 * D ** -0.5),
        'ada_b': nrm((L, 6 * D), 0.02),
        'norm1_g': gain((L, D)),
        'norm2_g': gain((L, D)),
        'w_in': nrm((L, D, D_IN), D ** -0.5),
        'mla_q_norm_g': gain((L, MLA_Q_RANK)),
        'mla_w_uq': nrm((L, MLA_Q_RANK, MLA_HEADS * (MLA_NOPE + MLA_ROPE)), MLA_Q_RANK ** -0.5),
        'mla_kv_norm_g': gain((L, MLA_KV_RANK)),
        'mla_w_ukv': nrm((L, MLA_KV_RANK, MLA_HEADS * (MLA_NOPE + MLA_V)), MLA_KV_RANK ** -0.5),
        'mla_qk_q_g': gain((L, MLA_NOPE + MLA_ROPE)),
        'mla_qk_k_g': gain((L, MLA_NOPE + MLA_ROPE)),
        'gqa_q_g': gain((L, HEAD_DIM)),
        'gqa_k_g': gain((L, HEAD_DIM)),
        'na_q_g': gain((L, HEAD_DIM)),
        'na_k_g': gain((L, HEAD_DIM)),
        'na_rpb': nrm((L, NA_HEADS, 2 * NA_KH - 1, 2 * NA_KW - 1), 0.1),
        'mix_out_norm_g': gain((L, MIX_WIDTH)),
        'w_out': nrm((L, MIX_WIDTH, D), MIX_WIDTH ** -0.5),
        'router_group_w': nrm((L, D, N_GROUPS), D ** -0.5),
        'router_group_b': nrm((L, N_GROUPS), 0.01),
        'router_expert_w': nrm((L, D, N_EXPERTS), D ** -0.5),
        'router_expert_b': nrm((L, N_EXPERTS), 0.01),
        'expert_w_gate': nrm((L, N_EXPERTS, D, D_EXPERT), D ** -0.5),
        'expert_w_up': nrm((L, N_EXPERTS, D, D_EXPERT), D ** -0.5),
        'expert_w_down': nrm((L, N_EXPERTS, D_EXPERT, D), D_EXPERT ** -0.5),
    }


def reference(x, c, ada_w, ada_b, norm1_g, norm2_g, w_in, mla_q_norm_g, mla_w_uq, mla_kv_norm_g,
              mla_w_ukv, mla_qk_q_g, mla_qk_k_g, gqa_q_g, gqa_k_g, na_q_g, na_k_g, na_rpb,
              mix_out_norm_g, w_out, router_group_w, router_group_b, router_expert_w,
              router_expert_b, expert_w_gate, expert_w_up, expert_w_down):
    cond = jax.nn.silu(c)
    for l in range(DEPTH):
        mod = cond @ ada_w[l] + ada_b[l]
        sh1, sc1, g1, sh2, sc2, g2 = [m[:, None, :] for m in jnp.split(mod, 6, axis=-1)]
        h = rms_norm(x, norm1_g[l]) * (1 + sc1) + sh1
        a_q, a_kv, a_pe, gq, gk, gv, nq, nk, nv = _split_cols(h @ w_in[l], IN_SIZES)
        o_a = mla_mixer(a_q, a_kv, a_pe, mla_q_norm_g[l], mla_w_uq[l], mla_kv_norm_g[l],
                        mla_w_ukv[l], mla_qk_q_g[l], mla_qk_k_g[l])
        o_b = gqa_mixer(gq, gk, gv, gqa_q_g[l], gqa_k_g[l])
        o_c = neighbourhood_mixer(nq, nk, nv, na_q_g[l], na_k_g[l], na_rpb[l])
        o = group_rms(jnp.concatenate([o_a, o_b, o_c], axis=-1), mix_out_norm_g[l])
        x = x + g1 * (o @ w_out[l])
        h = rms_norm(x, norm2_g[l]) * (1 + sc2) + sh2
        x = x + g2 * hier_moe(h, router_group_w[l], router_group_b[l], router_expert_w[l],
                              router_expert_b[l], expert_w_gate[l], expert_w_up[l], expert_w_down[l])
    return x
```

```python
import functools

import jax
import jax.numpy as jnp
import numpy as np
from jax import lax
from jax.experimental import pallas as pl
from jax.experimental.pallas import tpu as pltpu

F32 = jnp.float32
BF16 = jnp.bfloat16
I32 = jnp.int32

D_MODEL = 2048
GRID_W = 64
HEAD_DIM = 128
ROPE_THETA = 10000.0
EPS = 1e-6
MLA_HEADS = 6
MLA_Q_RANK = 384
MLA_KV_RANK = 256
MLA_NOPE = 128
MLA_ROPE = 64
MLA_V = 128
MLA_QK = MLA_NOPE + MLA_ROPE
GQA_HEADS = 4
GQA_KV_HEADS = 2
NA_HEADS = 6
NA_KH = 8
NA_KW = 16
N_GROUPS = 4
EXPERTS_PER_GROUP = 8
N_EXPERTS = N_GROUPS * EXPERTS_PER_GROUP
D_EXPERT = 512

LANES = 128
SUBLANES = 8
VMEM_LIMIT_BYTES = 56 * 1024 * 1024

PE_PAD = LANES
C_AQ = 0
C_AKV = C_AQ + MLA_Q_RANK
C_APE = C_AKV + MLA_KV_RANK
C_GQ = C_APE + PE_PAD
C_GK = C_GQ + GQA_HEADS * HEAD_DIM
C_GV = C_GK + GQA_KV_HEADS * HEAD_DIM
C_NQ = C_GV + GQA_KV_HEADS * HEAD_DIM
C_NK = C_NQ + NA_HEADS * HEAD_DIM
C_NV = C_NK + NA_HEADS * HEAD_DIM
D_IN_PAD = C_NV + NA_HEADS * HEAD_DIM
MLA_QK_PAD = 2 * LANES

NA_QROWS = 4
NA_KROWS = 12
NA_TQ = NA_QROWS * GRID_W
NA_TK = NA_KROWS * GRID_W

MOE_ROWS = 256
ROUTER_ROWS = 40
ROUTER_E0 = 8


def _cparams(n_axes):
    return pltpu.CompilerParams(
        dimension_semantics=("arbitrary",) * n_axes,
        vmem_limit_bytes=VMEM_LIMIT_BYTES,
    )


ADA_TN = 1024
ADA_RC = 256


def _adaln_kernel(c_ref, w_ref, b_ref, o_ref):
    def body(i, acc):
        r0 = pl.multiple_of(i * ADA_RC, ADA_RC)
        c = c_ref[pl.ds(r0, ADA_RC), :]
        cond = c * (1.0 / (1.0 + jnp.exp(-c)))
        w = w_ref[0, pl.ds(r0, ADA_RC), :]
        return acc + jnp.sum(w * cond, axis=0, keepdims=True)

    acc = lax.fori_loop(0, D_MODEL // ADA_RC, body, jnp.zeros((1, ADA_TN), F32))
    o_ref[0] = acc + b_ref[0]


def _adaln(c, ada_w, ada_b):
    L = ada_w.shape[0]
    n_out = ada_w.shape[2]
    return pl.pallas_call(
        _adaln_kernel,
        grid=(L, n_out // ADA_TN),
        in_specs=[
            pl.BlockSpec((D_MODEL, 1), lambda l, j: (0, 0)),
            pl.BlockSpec((1, D_MODEL, ADA_TN), lambda l, j: (l, 0, j)),
            pl.BlockSpec((1, 1, ADA_TN), lambda l, j: (l, 0, j)),
        ],
        out_specs=pl.BlockSpec((1, 1, ADA_TN), lambda l, j: (l, 0, j)),
        out_shape=jax.ShapeDtypeStruct((L, 1, n_out), F32),
        compiler_params=_cparams(2),
        name="adaln",
    )(c.reshape(D_MODEL, 1), ada_w, ada_b.reshape(L, 1, n_out))


def _modulated_norm(x, g, sc, sh):
    ms = jnp.mean(x * x, axis=-1, keepdims=True)
    y = x * lax.rsqrt(ms + EPS) * g
    return y * (1.0 + sc) + sh


def _inproj_kernel(x_ref, g_ref, sh_ref, sc_ref, w_ref, o_ref, h_ref):
    @pl.when(pl.program_id(1) == 0)
    def _():
        h_ref[...] = _modulated_norm(x_ref[...], g_ref[...], sc_ref[...], sh_ref[...]).astype(BF16)

    o_ref[...] = jnp.dot(h_ref[...], w_ref[...], preferred_element_type=F32)


def _inproj(x, norm_g, mod, w_in_p):
    n = x.shape[0]
    tm = min(512, n)
    tn = 1024
    return pl.pallas_call(
        _inproj_kernel,
        grid=(n // tm, D_IN_PAD // tn),
        in_specs=[
            pl.BlockSpec((tm, D_MODEL), lambda i, j: (i, 0)),
            pl.BlockSpec((1, D_MODEL), lambda i, j: (0, 0)),
            pl.BlockSpec((1, D_MODEL), lambda i, j: (0, 0)),
            pl.BlockSpec((1, D_MODEL), lambda i, j: (0, 1)),
            pl.BlockSpec((D_MODEL, tn), lambda i, j: (0, j)),
        ],
        out_specs=pl.BlockSpec((tm, tn), lambda i, j: (i, j)),
        out_shape=jax.ShapeDtypeStruct((n, D_IN_PAD), F32),
        scratch_shapes=[pltpu.VMEM((tm, D_MODEL), BF16)],
        compiler_params=_cparams(2),
        name="inproj",
    )(x, norm_g, mod, mod, w_in_p)


PREP_TM = 512


def _rot_swap(x):
    lane = lax.broadcasted_iota(I32, x.shape, 1)
    fwd = pltpu.roll(x, LANES - 32, 1)
    bwd = pltpu.roll(x, 32, 1)
    return jnp.where((lane & 63) < 32, fwd, bwd)


def _rope_tile(x, cos, sin_signed):
    return x * cos + _rot_swap(x) * sin_signed


def _sumsq(x):
    return jnp.sum(x * x, axis=-1, keepdims=True)


def _prep_kernel(p_ref, wq_ref, wkv_ref, gql_ref, gkvl_ref, gmq_ref, gmk_ref, ggq_ref, ggk_ref,
                 gnq_ref, gnk_ref, cm_ref, sm_ref, ca_ref, sa_ref,
                 qm_ref, km_ref, vmt_ref, qg_ref, kg_ref, vgt_ref, qn_ref, kn_ref, vn_ref):
    mla_scale = MLA_QK ** -0.5
    hd_scale = HEAD_DIM ** -0.5
    cm, sm = cm_ref[...], sm_ref[...]
    ca, sa = ca_ref[...], sa_ref[...]

    a_q = p_ref[:, C_AQ:C_AQ + MLA_Q_RANK]
    a_q = a_q * lax.rsqrt(_sumsq(a_q) / MLA_Q_RANK + EPS) * gql_ref[...]
    q = jnp.dot(a_q.astype(BF16), wq_ref[...], preferred_element_type=F32)
    gq_n, gq_r = gmq_ref[:, :LANES], gmq_ref[:, LANES:]
    for h in range(MLA_HEADS):
        nope = q[:, h * LANES:(h + 1) * LANES]
        ropp = q[:, (MLA_HEADS + h) * LANES:(MLA_HEADS + h + 1) * LANES]
        rinv = lax.rsqrt((_sumsq(nope) + _sumsq(ropp)) / MLA_QK + EPS)
        qm_ref[h, :, :LANES] = (nope * rinv * gq_n * mla_scale).astype(BF16)
        qm_ref[h, :, LANES:] = (_rope_tile(ropp * rinv * gq_r, cm, sm) * mla_scale).astype(BF16)

    a_kv = p_ref[:, C_AKV:C_AKV + MLA_KV_RANK]
    a_kv = a_kv * lax.rsqrt(_sumsq(a_kv) / MLA_KV_RANK + EPS) * gkvl_ref[...]
    kv = jnp.dot(a_kv.astype(BF16), wkv_ref[...], preferred_element_type=F32)
    pe = p_ref[:, C_APE:C_APE + PE_PAD]
    ss_pe = _sumsq(pe)
    gk_n, gk_r = gmk_ref[:, :LANES], gmk_ref[:, LANES:]
    pe_rot = _rope_tile(pe * gk_r, cm, sm)
    for h in range(MLA_HEADS):
        nope = kv[:, h * LANES:(h + 1) * LANES]
        v = kv[:, (MLA_HEADS + h) * LANES:(MLA_HEADS + h + 1) * LANES]
        rinv = lax.rsqrt((_sumsq(nope) + ss_pe) / MLA_QK + EPS)
        km_ref[h, :, :LANES] = (nope * rinv * gk_n).astype(BF16)
        km_ref[h, :, LANES:] = (pe_rot * rinv).astype(BF16)
        vmt_ref[h, 0] = v.T.astype(BF16)

    def head_norm(col, g):
        t = p_ref[:, col:col + HEAD_DIM]
        return t * lax.rsqrt(_sumsq(t) / HEAD_DIM + EPS) * g

    for h in range(GQA_HEADS):
        t = head_norm(C_GQ + h * HEAD_DIM, ggq_ref[...])
        qg_ref[h] = (_rope_tile(t, ca, sa) * hd_scale).astype(BF16)
    for h in range(GQA_KV_HEADS):
        t = head_norm(C_GK + h * HEAD_DIM, ggk_ref[...])
        kg_ref[h] = _rope_tile(t, ca, sa).astype(BF16)
        vgt_ref[h, 0] = p_ref[:, C_GV + h * HEAD_DIM:C_GV + (h + 1) * HEAD_DIM].T.astype(BF16)

    for h in range(NA_HEADS):
        qn_ref[h] = (head_norm(C_NQ + h * HEAD_DIM, gnq_ref[...]) * hd_scale).astype(BF16)
        kn_ref[h] = head_norm(C_NK + h * HEAD_DIM, gnk_ref[...]).astype(BF16)
        vn_ref[h] = p_ref[:, C_NV + h * HEAD_DIM:C_NV + (h + 1) * HEAD_DIM].astype(BF16)


def _prep(proj, wq, wkv, gql, gkvl, gmq, gmk, ggq, ggk, gnq, gnk, cm, sm, ca, sa):
    n = proj.shape[0]
    tm = min(PREP_TM, n)
    nb = n // tm

    def full(a):
        return pl.BlockSpec(a.shape, lambda i: (0,) * a.ndim)

    def rows(width):
        return pl.BlockSpec((tm, width), lambda i: (i, 0))

    def heads(h, width):
        return pl.BlockSpec((h, tm, width), lambda i: (0, i, 0))

    def heads_t(h):
        return pl.BlockSpec((h, 1, HEAD_DIM, tm), lambda i: (0, i, 0, 0))

    out_shape = (
        jax.ShapeDtypeStruct((MLA_HEADS, n, MLA_QK_PAD), BF16),
        jax.ShapeDtypeStruct((MLA_HEADS, n, MLA_QK_PAD), BF16),
        jax.ShapeDtypeStruct((MLA_HEADS, nb, MLA_V, tm), BF16),
        jax.ShapeDtypeStruct((GQA_HEADS, n, HEAD_DIM), BF16),
        jax.ShapeDtypeStruct((GQA_KV_HEADS, n, HEAD_DIM), BF16),
        jax.ShapeDtypeStruct((GQA_KV_HEADS, nb, HEAD_DIM, tm), BF16),
        jax.ShapeDtypeStruct((NA_HEADS, n, HEAD_DIM), BF16),
        jax.ShapeDtypeStruct((NA_HEADS, n, HEAD_DIM), BF16),
        jax.ShapeDtypeStruct((NA_HEADS, n, HEAD_DIM), BF16),
    )
    out_specs = (
        heads(MLA_HEADS, MLA_QK_PAD), heads(MLA_HEADS, MLA_QK_PAD), heads_t(MLA_HEADS),
        heads(GQA_HEADS, HEAD_DIM), heads(GQA_KV_HEADS, HEAD_DIM), heads_t(GQA_KV_HEADS),
        heads(NA_HEADS, HEAD_DIM), heads(NA_HEADS, HEAD_DIM), heads(NA_HEADS, HEAD_DIM),
    )
    small = (wq, wkv, gql, gkvl, gmq, gmk, ggq, ggk, gnq, gnk)
    return pl.pallas_call(
        _prep_kernel,
        grid=(nb,),
        in_specs=[rows(D_IN_PAD)] + [full(a) for a in small] + [rows(LANES)] * 4,
        out_specs=out_specs,
        out_shape=out_shape,
        compiler_params=_cparams(1),
        name="prep",
    )(proj, *small, cm, sm, ca, sa)


FLASH_TQ = 512


def _flash_kernel(q_ref, k_ref, vt_ref, o_ref, m_ref, l_ref, acc_ref, *, n_chunks, tk):
    q = q_ref[0]
    m_ref[...] = jnp.full(m_ref.shape, -jnp.inf, F32)
    l_ref[...] = jnp.zeros(l_ref.shape, F32)
    acc_ref[...] = jnp.zeros(acc_ref.shape, F32)

    def body(c, carry):
        kc = k_ref[0, pl.ds(pl.multiple_of(c * tk, tk), tk), :]
        s = lax.dot_general(kc, q, (((1,), (1,)), ((), ())), preferred_element_type=F32)
        m_prev = m_ref[...]
        m_new = jnp.maximum(m_prev, jnp.max(s, axis=0, keepdims=True))
        alpha = jnp.exp(m_prev - m_new)
        p = jnp.exp(s - m_new)
        l_ref[...] = alpha * l_ref[...] + jnp.sum(p, axis=0, keepdims=True)
        pv = jnp.dot(vt_ref[0, c], p.astype(BF16), preferred_element_type=F32)
        acc_ref[...] = alpha * acc_ref[...] + pv
        m_ref[...] = m_new
        return carry

    lax.fori_loop(0, n_chunks, body, 0)
    o_ref[...] = (acc_ref[...] / l_ref[...]).T


def _flash(q, k, vt):
    H, S, dq = q.shape
    hk, n_chunks, dv, tk = vt.shape
    group = H // hk
    tq = min(FLASH_TQ, S)
    return pl.pallas_call(
        functools.partial(_flash_kernel, n_chunks=n_chunks, tk=tk),
        grid=(H, S // tq),
        in_specs=[
            pl.BlockSpec((1, tq, dq), lambda h, i: (h, i, 0)),
            pl.BlockSpec((1, S, dq), lambda h, i: (h // group, 0, 0)),
            pl.BlockSpec((1, n_chunks, dv, tk), lambda h, i: (h // group, 0, 0, 0)),
        ],
        out_specs=pl.BlockSpec((tq, dv), lambda h, i: (i, h)),
        out_shape=jax.ShapeDtypeStruct((S, H * dv), F32),
        scratch_shapes=[pltpu.VMEM((1, tq), F32), pltpu.VMEM((1, tq), F32), pltpu.VMEM((dv, tq), F32)],
        compiler_params=_cparams(2),
        name="flash",
    )(q, k, vt)


NEG_MIN = float(np.finfo(np.float32).min)
RPB_ROWS = 2 * NA_KH - 1
RPB_COLS = 2 * NA_KW - 1


def _na_rel(kind, i, j):
    if kind == 0:
        return j - i + NA_KH - 1, j < NA_KH
    if kind == 1:
        return j - i + NA_KH // 2 - 1, i <= j <= i + NA_KH - 1
    return j - i - 1, j >= NA_KROWS - NA_KH


def _na_bias_kernel(rpb_ref, o_ref):
    h = pl.program_id(0)
    shape = (GRID_W, LANES)
    c = lax.broadcasted_iota(I32, shape, 0)
    kc = lax.broadcasted_iota(I32, shape, 1) & (GRID_W - 1)
    cs = jnp.clip(c - NA_KW // 2, 0, GRID_W - NA_KW)
    in_win = (kc >= cs) & (kc < cs + NA_KW)
    dc = kc - c + NA_KW - 1
    lane_lo = lax.broadcasted_iota(I32, shape, 1) < GRID_W
    masked = jnp.full(shape, NEG_MIN, F32)
    cache = {}

    def toeplitz(dr):
        if dr not in cache:
            t = jnp.zeros(shape, F32)
            base = (h * RPB_ROWS + dr) * RPB_COLS
            for d in range(RPB_COLS):
                t = jnp.where(dc == d, rpb_ref[base + d], t)
            cache[dr] = jnp.where(in_win, t, NEG_MIN)
        return cache[dr]

    def half(kind, i, j):
        dr, valid = _na_rel(kind, i, j)
        return toeplitz(dr) if valid else masked

    for kind in range(3):
        for i in range(NA_QROWS):
            for jp in range(NA_KROWS // 2):
                piece = jnp.where(lane_lo, half(kind, i, 2 * jp), half(kind, i, 2 * jp + 1))
                o_ref[kind, 0, i * GRID_W:(i + 1) * GRID_W, jp * LANES:(jp + 1) * LANES] = piece


def _na_bias(rpb):
    return pl.pallas_call(
        _na_bias_kernel,
        grid=(NA_HEADS,),
        in_specs=[pl.BlockSpec(memory_space=pltpu.SMEM)],
        out_specs=pl.BlockSpec((3, 1, NA_TQ, NA_TK), lambda h: (0, h, 0, 0)),
        out_shape=jax.ShapeDtypeStruct((3, NA_HEADS, NA_TQ, NA_TK), F32),
        compiler_params=_cparams(1),
        name="na_bias",
    )(rpb.reshape(-1))


def _na_kernel(q_ref, k_ref, v_ref, b_ref, o_ref, *, grid_rows):
    i = pl.program_id(1)
    kstart = jnp.clip(i * NA_QROWS - NA_KH // 2, 0, grid_rows - NA_KROWS)
    off = pl.multiple_of(kstart * GRID_W, GRID_W)
    kw = k_ref[0, pl.ds(off, NA_TK), :]
    vw = v_ref[0, pl.ds(off, NA_TK), :]
    s = lax.dot_general(q_ref[0], kw, (((1,), (1,)), ((), ())), preferred_element_type=F32)
    s = s + b_ref[0, 0]
    p = jnp.exp(s - jnp.max(s, axis=-1, keepdims=True))
    l = jnp.sum(p, axis=-1, keepdims=True)
    o_ref[...] = jnp.dot(p.astype(BF16), vw, preferred_element_type=F32) / l


def _na(q, k, v, bias):
    H, S, d = q.shape
    grid_rows = S // GRID_W
    nblk = grid_rows // NA_QROWS

    def bias_idx(h, i):
        return (jnp.where(i == 0, 0, jnp.where(i == nblk - 1, 2, 1)), h, 0, 0)

    return pl.pallas_call(
        functools.partial(_na_kernel, grid_rows=grid_rows),
        grid=(H, nblk),
        in_specs=[
            pl.BlockSpec((1, NA_TQ, d), lambda h, i: (h, i, 0)),
            pl.BlockSpec((1, S, d), lambda h, i: (h, 0, 0)),
            pl.BlockSpec((1, S, d), lambda h, i: (h, 0, 0)),
            pl.BlockSpec((1, 1, NA_TQ, NA_TK), bias_idx),
        ],
        out_specs=pl.BlockSpec((NA_TQ, d), lambda h, i: (i, h)),
        out_shape=jax.ShapeDtypeStruct((S, H * d), F32),
        compiler_params=_cparams(2),
        name="na",
    )(q, k, v, bias)


W_MLA = MLA_HEADS * MLA_V
W_GQA = GQA_HEADS * HEAD_DIM
W_NA = NA_HEADS * HEAD_DIM


def _outproj_kernel(oa_ref, ob_ref, oc_ref, g_ref, w_ref, x_ref, g1_ref, o_ref):
    def gnorm(o, g):
        return (o * lax.rsqrt(jnp.mean(o * o, axis=-1, keepdims=True) + EPS) * g).astype(BF16)

    a = gnorm(oa_ref[...], g_ref[:, :W_MLA])
    b = gnorm(ob_ref[...], g_ref[:, W_MLA:W_MLA + W_GQA])
    c = gnorm(oc_ref[...], g_ref[:, W_MLA + W_GQA:])
    y = jnp.dot(a, w_ref[:W_MLA, :], preferred_element_type=F32)
    y = y + jnp.dot(b, w_ref[W_MLA:W_MLA + W_GQA, :], preferred_element_type=F32)
    y = y + jnp.dot(c, w_ref[W_MLA + W_GQA:, :], preferred_element_type=F32)
    o_ref[...] = x_ref[...] + g1_ref[...] * y


def _outproj(oa, ob, oc, gains, w_out, x, mod):
    n = x.shape[0]
    tm = min(256, n)
    return pl.pallas_call(
        _outproj_kernel,
        grid=(n // tm,),
        in_specs=[
            pl.BlockSpec((tm, W_MLA), lambda i: (i, 0)),
            pl.BlockSpec((tm, W_GQA), lambda i: (i, 0)),
            pl.BlockSpec((tm, W_NA), lambda i: (i, 0)),
            pl.BlockSpec((1, D_MODEL), lambda i: (0, 0)),
            pl.BlockSpec((D_MODEL, D_MODEL), lambda i: (0, 0)),
            pl.BlockSpec((tm, D_MODEL), lambda i: (i, 0)),
            pl.BlockSpec((1, D_MODEL), lambda i: (0, 2)),
        ],
        out_specs=pl.BlockSpec((tm, D_MODEL), lambda i: (i, 0)),
        out_shape=jax.ShapeDtypeStruct((n, D_MODEL), F32),
        compiler_params=_cparams(1),
        name="outproj",
    )(oa, ob, oc, gains, w_out, x, mod)


ROUTER_TM = 512


def _first_argmax(v, vmax, n):
    idx = lax.broadcasted_iota(I32, v.shape, 0)
    return jnp.min(jnp.where(v == vmax, idx, n), axis=0, keepdims=True)


def _router_kernel(x_ref, g_ref, sh_ref, sc_ref, w_ref, b_ref, u_ref,
                   h_ref, ri_ref, rw_ref, cnt_ref, carry_ref):
    @pl.when(pl.program_id(0) == 0)
    def _():
        carry_ref[...] = jnp.zeros(carry_ref.shape, F32)

    h = _modulated_norm(x_ref[...], g_ref[...], sc_ref[...], sh_ref[...])
    h_ref[...] = h
    logits = lax.dot_general(w_ref[...], h, (((1,), (1,)), ((), ())),
                             precision=lax.Precision.HIGHEST,
                             preferred_element_type=F32) + b_ref[...]
    tm = logits.shape[1]

    gl = logits[0:N_GROUPS]
    gmax = jnp.max(gl, axis=0, keepdims=True)
    gexp = jnp.exp(gl - gmax)
    gprob = gexp / jnp.sum(gexp, axis=0, keepdims=True)
    p_g = jnp.max(gprob, axis=0, keepdims=True)
    g_top = _first_argmax(gprob, p_g, N_GROUPS)

    e_in = jnp.zeros((EXPERTS_PER_GROUP, tm), F32)
    for g in range(N_GROUPS):
        lo = ROUTER_E0 + g * EXPERTS_PER_GROUP
        e_in = jnp.where(g_top == g, logits[lo:lo + EXPERTS_PER_GROUP], e_in)
    eexp = jnp.exp(e_in - jnp.max(e_in, axis=0, keepdims=True))
    eprob = eexp / jnp.sum(eexp, axis=0, keepdims=True)
    v1 = jnp.max(eprob, axis=0, keepdims=True)
    i1 = _first_argmax(eprob, v1, EXPERTS_PER_GROUP)
    row = lax.broadcasted_iota(I32, eprob.shape, 0)
    rest = jnp.where(row == i1, -1.0, eprob)
    v2 = jnp.max(rest, axis=0, keepdims=True)
    i2 = _first_argmax(rest, v2, EXPERTS_PER_GROUP)
    wsum = v1 + v2
    w1 = v1 / wsum * p_g
    w2 = v2 / wsum * p_g
    e1 = g_top * EXPERTS_PER_GROUP + i1
    e2 = g_top * EXPERTS_PER_GROUP + i2

    erow = lax.broadcasted_iota(I32, (N_EXPERTS, tm), 0)
    oh1 = (erow == e1).astype(F32)
    oh2 = (erow == e2).astype(F32)
    oh = oh1 + oh2
    before = jnp.dot(oh.astype(BF16), u_ref[...], preferred_element_type=F32) + carry_ref[...]
    r1 = jnp.sum(oh1 * before, axis=0, keepdims=True)
    r2 = jnp.sum(oh2 * before, axis=0, keepdims=True)
    carry_ref[...] = carry_ref[...] + jnp.sum(oh, axis=1, keepdims=True)
    cnt_ref[...] = jnp.broadcast_to(carry_ref[...], cnt_ref.shape)

    ri_ref[...] = jnp.zeros(ri_ref.shape, I32)
    ri_ref[0:1, :] = e1
    ri_ref[1:2, :] = e2
    ri_ref[2:3, :] = r1.astype(I32)
    ri_ref[3:4, :] = r2.astype(I32)
    rw_ref[...] = jnp.zeros(rw_ref.shape, F32)
    rw_ref[0:1, :] = w1
    rw_ref[1:2, :] = w2


def _router(x, norm_g, mod, w_r, b_r, upper):
    n = x.shape[0]
    tm = upper.shape[0]
    return pl.pallas_call(
        _router_kernel,
        grid=(n // tm,),
        in_specs=[
            pl.BlockSpec((tm, D_MODEL), lambda i: (i, 0)),
            pl.BlockSpec((1, D_MODEL), lambda i: (0, 0)),
            pl.BlockSpec((1, D_MODEL), lambda i: (0, 3)),
            pl.BlockSpec((1, D_MODEL), lambda i: (0, 4)),
            pl.BlockSpec((ROUTER_ROWS, D_MODEL), lambda i: (0, 0)),
            pl.BlockSpec((ROUTER_ROWS, 1), lambda i: (0, 0)),
            pl.BlockSpec((tm, tm), lambda i: (0, 0)),
        ],
        out_specs=(
            pl.BlockSpec((tm, D_MODEL), lambda i: (i, 0)),
            pl.BlockSpec((SUBLANES, tm), lambda i: (0, i)),
            pl.BlockSpec((SUBLANES, tm), lambda i: (0, i)),
            pl.BlockSpec((N_EXPERTS, LANES), lambda i: (0, 0)),
        ),
        out_shape=(
            jax.ShapeDtypeStruct((n, D_MODEL), F32),
            jax.ShapeDtypeStruct((SUBLANES, n), I32),
            jax.ShapeDtypeStruct((SUBLANES, n), F32),
            jax.ShapeDtypeStruct((N_EXPERTS, LANES), F32),
        ),
        scratch_shapes=[pltpu.VMEM((N_EXPERTS, 1), F32)],
        compiler_params=_cparams(1),
        name="router",
    )(x, norm_g, mod, mod, w_r, b_r, upper)


def _row_gather_start(src_hbm, idx_ref, base, dst, sem, n_rows):
    def body(r, carry):
        tok = idx_ref[base + r]
        pltpu.make_async_copy(src_hbm.at[pl.ds(tok, 1)], dst.at[pl.ds(r, 1)], sem).start()
        return carry

    lax.fori_loop(0, n_rows, body, 0)


def _row_gather_wait(src_hbm, dst, sem, n_rows):
    pltpu.make_async_copy(src_hbm.at[pl.ds(0, n_rows)], dst, sem).wait()


def _expert_kernel(src_ref, blk_ref, nused_ref, h_hbm, wg_ref, wu_ref, wd_ref, y_ref, xg_ref, sem):
    b = pl.program_id(0)
    nused = nused_ref[0]
    slot = b % 2

    @pl.when((b == 0) & (nused > 0))
    def _():
        _row_gather_start(h_hbm, src_ref, 0, xg_ref.at[0], sem.at[0], MOE_ROWS)

    @pl.when(b + 1 < nused)
    def _():
        _row_gather_start(h_hbm, src_ref, (b + 1) * MOE_ROWS, xg_ref.at[1 - slot], sem.at[1 - slot],
                          MOE_ROWS)

    @pl.when(b < nused)
    def _():
        _row_gather_wait(h_hbm, xg_ref.at[slot], sem.at[slot], MOE_ROWS)
        x = xg_ref[slot].astype(BF16)
        g = jnp.dot(x, wg_ref[0].astype(BF16), preferred_element_type=F32)
        u = jnp.dot(x, wu_ref[0].astype(BF16), preferred_element_type=F32)
        a = (g * (1.0 / (1.0 + jnp.exp(-g))) * u).astype(BF16)
        y_ref[...] = jnp.dot(a, wd_ref[0].astype(BF16), preferred_element_type=F32)

    @pl.when(b >= nused)
    def _():
        y_ref[...] = jnp.zeros(y_ref.shape, F32)


def _experts(src_tok, blk_e, nused, h, w_gate, w_up, w_down):
    nb = blk_e.shape[0]
    grid_spec = pltpu.PrefetchScalarGridSpec(
        num_scalar_prefetch=3,
        grid=(nb,),
        in_specs=[
            pl.BlockSpec(memory_space=pl.ANY),
            pl.BlockSpec((1, D_MODEL, D_EXPERT), lambda b, src, blk, nu: (blk[b], 0, 0)),
            pl.BlockSpec((1, D_MODEL, D_EXPERT), lambda b, src, blk, nu: (blk[b], 0, 0)),
            pl.BlockSpec((1, D_EXPERT, D_MODEL), lambda b, src, blk, nu: (blk[b], 0, 0)),
        ],
        out_specs=pl.BlockSpec((MOE_ROWS, D_MODEL), lambda b, src, blk, nu: (b, 0)),
        scratch_shapes=[pltpu.VMEM((2, MOE_ROWS, D_MODEL), F32), pltpu.SemaphoreType.DMA((2,))],
    )
    return pl.pallas_call(
        _expert_kernel,
        grid_spec=grid_spec,
        out_shape=jax.ShapeDtypeStruct((nb * MOE_ROWS, D_MODEL), F32),
        compiler_params=_cparams(1),
        name="experts",
    )(src_tok, blk_e, nused, h, w_gate, w_up, w_down)


COMB_TM = 256


def _combine_kernel(d1_ref, d2_ref, y_hbm, x_ref, w1_ref, w2_ref, g2_ref, o_ref, ya_ref, yb_ref, sem):
    i = pl.program_id(0)
    n = pl.num_programs(0)
    slot = i % 2

    def start(blk, s):
        _row_gather_start(y_hbm, d1_ref, blk * COMB_TM, ya_ref.at[s], sem.at[0, s], COMB_TM)
        _row_gather_start(y_hbm, d2_ref, blk * COMB_TM, yb_ref.at[s], sem.at[1, s], COMB_TM)

    @pl.when(i == 0)
    def _():
        start(0, 0)

    @pl.when(i + 1 < n)
    def _():
        start(i + 1, 1 - slot)

    _row_gather_wait(y_hbm, ya_ref.at[slot], sem.at[0, slot], COMB_TM)
    _row_gather_wait(y_hbm, yb_ref.at[slot], sem.at[1, slot], COMB_TM)
    y = w1_ref[...] * ya_ref[slot] + w2_ref[...] * yb_ref[slot]
    o_ref[...] = x_ref[...] + g2_ref[...] * y


def _combine(dest1, dest2, ybuf, x, w1, w2, mod):
    n = x.shape[0]
    grid_spec = pltpu.PrefetchScalarGridSpec(
        num_scalar_prefetch=2,
        grid=(n // COMB_TM,),
        in_specs=[
            pl.BlockSpec(memory_space=pl.ANY),
            pl.BlockSpec((COMB_TM, D_MODEL), lambda i, d1, d2: (i, 0)),
            pl.BlockSpec((COMB_TM, 1), lambda i, d1, d2: (i, 0)),
            pl.BlockSpec((COMB_TM, 1), lambda i, d1, d2: (i, 0)),
            pl.BlockSpec((1, D_MODEL), lambda i, d1, d2: (0, 5)),
        ],
        out_specs=pl.BlockSpec((COMB_TM, D_MODEL), lambda i, d1, d2: (i, 0)),
        scratch_shapes=[
            pltpu.VMEM((2, COMB_TM, D_MODEL), F32),
            pltpu.VMEM((2, COMB_TM, D_MODEL), F32),
            pltpu.SemaphoreType.DMA((2, 2)),
        ],
    )
    return pl.pallas_call(
        _combine_kernel,
        grid_spec=grid_spec,
        out_shape=jax.ShapeDtypeStruct((n, D_MODEL), F32),
        compiler_params=_cparams(1),
        name="combine",
    )(dest1, dest2, ybuf, x, w1, w2, mod)


def _rope_angles(pos, d):
    inv = ROPE_THETA ** (-jnp.arange(0, d, 2, dtype=F32) / d)
    return pos.astype(F32)[:, None] * inv[None, :]


def _rope_tables(S):
    t = jnp.arange(S)
    z = jnp.zeros((S, MLA_ROPE), F32)
    a = _rope_angles(t, MLA_ROPE)
    cm = jnp.concatenate([jnp.cos(a), jnp.cos(a), z], axis=1)
    sm = jnp.concatenate([-jnp.sin(a), jnp.sin(a), z], axis=1)
    ar = _rope_angles(t // GRID_W, HEAD_DIM // 2)
    ac = _rope_angles(t % GRID_W, HEAD_DIM // 2)
    ca = jnp.concatenate([jnp.cos(ar), jnp.cos(ar), jnp.cos(ac), jnp.cos(ac)], axis=1)
    sa = jnp.concatenate([-jnp.sin(ar), jnp.sin(ar), -jnp.sin(ac), jnp.sin(ac)], axis=1)
    return cm, sm, ca, sa


def _pad_lanes(a, width):
    return jnp.concatenate([a, jnp.zeros(a.shape[:-1] + (width - a.shape[-1],), a.dtype)], axis=-1)


def _layer_weights(w_in, w_uq, w_ukv, qk_q_g, qk_k_g):
    split = C_APE + MLA_ROPE
    w_in_p = jnp.concatenate(
        [w_in[:, :split], jnp.zeros((D_MODEL, PE_PAD - MLA_ROPE), F32), w_in[:, split:]], axis=1
    ).astype(BF16)
    wq = w_uq.reshape(MLA_Q_RANK, MLA_HEADS, MLA_QK)
    wq_n = wq[:, :, :MLA_NOPE].reshape(MLA_Q_RANK, MLA_HEADS * LANES)
    wq_r = _pad_lanes(wq[:, :, MLA_NOPE:], LANES).reshape(MLA_Q_RANK, MLA_HEADS * LANES)
    wq_p = jnp.concatenate([wq_n, wq_r], axis=1).astype(BF16)
    wkv = w_ukv.reshape(MLA_KV_RANK, MLA_HEADS, MLA_NOPE + MLA_V)
    wkv_p = jnp.concatenate(
        [wkv[:, :, :MLA_NOPE].reshape(MLA_KV_RANK, -1), wkv[:, :, MLA_NOPE:].reshape(MLA_KV_RANK, -1)],
        axis=1).astype(BF16)
    gmq = _pad_lanes(qk_q_g[None, :], MLA_QK_PAD)
    gmk = _pad_lanes(qk_k_g[None, :], MLA_QK_PAD)
    return w_in_p, wq_p, wkv_p, gmq, gmk


def _routing_plan(ri, cnt, n_blocks):
    n = ri.shape[1]
    e1, e2, r1, r2 = ri[0], ri[1], ri[2], ri[3]
    counts = cnt[:, 0].astype(I32)
    padded = ((counts + MOE_ROWS - 1) // MOE_ROWS) * MOE_ROWS
    pad_end = jnp.cumsum(padded)
    pad_start = pad_end - padded
    dest1 = pad_start[e1] + r1
    dest2 = pad_start[e2] + r2
    tok = jnp.arange(n, dtype=I32)
    src_tok = jnp.zeros((n_blocks * MOE_ROWS,), I32).at[dest1].set(tok).at[dest2].set(tok)
    blk_e = jnp.minimum(
        jnp.searchsorted(pad_end, jnp.arange(n_blocks, dtype=I32) * MOE_ROWS, side="right"),
        N_EXPERTS - 1).astype(I32)
    nused = (pad_end[-1:] // MOE_ROWS).astype(I32)
    return dest1, dest2, src_tok, blk_e, nused


def kernel(x, c, ada_w, ada_b, norm1_g, norm2_g, w_in, mla_q_norm_g, mla_w_uq, mla_kv_norm_g, mla_w_ukv, mla_qk_q_g, mla_qk_k_g, gqa_q_g, gqa_k_g, na_q_g, na_k_g, na_rpb, mix_out_norm_g, w_out, router_group_w, router_group_b, router_expert_w, router_expert_b, expert_w_gate, expert_w_up, expert_w_down):
    B, S, D = x.shape
    assert B == 1 and D == D_MODEL and S % (GRID_W * NA_QROWS) == 0 and S // GRID_W >= NA_KROWS
    depth = ada_w.shape[0]
    n = B * S
    xt = x.reshape(n, D)

    mod_all = _adaln(c, ada_w, ada_b)
    cm, sm, ca, sa = _rope_tables(S)
    tm_r = min(ROUTER_TM, n)
    upper = (jnp.arange(tm_r)[:, None] < jnp.arange(tm_r)[None, :]).astype(BF16)
    n_blocks = -(-(n * 2 + N_EXPERTS * (MOE_ROWS - 1)) // MOE_ROWS)

    def row(v):
        return v[None, :]

    for l in range(depth):
        mod = mod_all[l]
        w_in_p, wq_p, wkv_p, gmq, gmk = _layer_weights(
            w_in[l], mla_w_uq[l], mla_w_ukv[l], mla_qk_q_g[l], mla_qk_k_g[l])

        proj = _inproj(xt, row(norm1_g[l]), mod, w_in_p)
        qm, km, vmt, qg, kg, vgt, qn, kn, vn = _prep(
            proj, wq_p, wkv_p, row(mla_q_norm_g[l]), row(mla_kv_norm_g[l]), gmq, gmk,
            row(gqa_q_g[l]), row(gqa_k_g[l]), row(na_q_g[l]), row(na_k_g[l]), cm, sm, ca, sa)
        o_a = _flash(qm, km, vmt)
        o_b = _flash(qg, kg, vgt)
        o_c = _na(qn, kn, vn, _na_bias(na_rpb[l]))
        xt = _outproj(o_a, o_b, o_c, row(mix_out_norm_g[l]), w_out[l].astype(BF16), xt, mod)

        w_r = jnp.concatenate(
            [router_group_w[l].T, jnp.zeros((ROUTER_E0 - N_GROUPS, D), F32), router_expert_w[l].T], axis=0)
        b_r = jnp.concatenate(
            [router_group_b[l], jnp.zeros((ROUTER_E0 - N_GROUPS,), F32), router_expert_b[l]])[:, None]
        h2, ri, rw, cnt = _router(xt, row(norm2_g[l]), mod, w_r, b_r, upper)
        dest1, dest2, src_tok, blk_e, nused = _routing_plan(ri, cnt, n_blocks)
        ybuf = _experts(src_tok, blk_e, nused, h2, expert_w_gate[l], expert_w_up[l], expert_w_down[l])
        xt = _combine(dest1, dest2, ybuf, xt, rw[0][:, None], rw[1][:, None], mod)

    return xt.reshape(B, S, D)
```

```python
import functools

import jax
import jax.numpy as jnp
import numpy as np
from jax import lax
from jax.experimental import pallas as pl
from jax.experimental.pallas import tpu as pltpu

F32 = jnp.float32
BF16 = jnp.bfloat16
I32 = jnp.int32

D_MODEL = 2048
GRID_W = 64
HEAD_DIM = 128
ROPE_THETA = 10000.0
EPS = 1e-6
MLA_HEADS = 6
MLA_Q_RANK = 384
MLA_KV_RANK = 256
MLA_NOPE = 128
MLA_ROPE = 64
MLA_V = 128
MLA_QK = MLA_NOPE + MLA_ROPE
GQA_HEADS = 4
GQA_KV_HEADS = 2
NA_HEADS = 6
NA_KH = 8
NA_KW = 16
N_GROUPS = 4
EXPERTS_PER_GROUP = 8
N_EXPERTS = N_GROUPS * EXPERTS_PER_GROUP
D_EXPERT = 512

LANES = 128
SUBLANES = 8
VMEM_LIMIT_BYTES = 56 * 1024 * 1024

PE_PAD = LANES
C_AQ = 0
C_AKV = C_AQ + MLA_Q_RANK
C_APE = C_AKV + MLA_KV_RANK
C_GQ = C_APE + PE_PAD
C_GK = C_GQ + GQA_HEADS * HEAD_DIM
C_GV = C_GK + GQA_KV_HEADS * HEAD_DIM
C_NQ = C_GV + GQA_KV_HEADS * HEAD_DIM
C_NK = C_NQ + NA_HEADS * HEAD_DIM
C_NV = C_NK + NA_HEADS * HEAD_DIM
D_IN_PAD = C_NV + NA_HEADS * HEAD_DIM
MLA_QK_PAD = 2 * LANES

NA_QROWS = 4
NA_KROWS = 12
NA_TQ = NA_QROWS * GRID_W
NA_TK = NA_KROWS * GRID_W

LOG2E = float(np.log2(np.e))
DV_EXT = HEAD_DIM + 16

MOE_ROWS = 256
ROW_SLABS = D_MODEL // LANES
ROW_PITCH = 20
ROUTER_ROWS = 40
ROUTER_E0 = 8


def _cparams(n_axes):
    return pltpu.CompilerParams(
        dimension_semantics=("arbitrary",) * n_axes,
        vmem_limit_bytes=VMEM_LIMIT_BYTES,
    )


ADA_TN = 1024
ADA_RC = 256


def _adaln_kernel(c_ref, w_ref, b_ref, o_ref):
    def body(i, acc):
        r0 = pl.multiple_of(i * ADA_RC, ADA_RC)
        c = c_ref[pl.ds(r0, ADA_RC), :]
        cond = c * (1.0 / (1.0 + jnp.exp(-c)))
        w = w_ref[0, pl.ds(r0, ADA_RC), :]
        return acc + jnp.sum(w * cond, axis=0, keepdims=True)

    acc = lax.fori_loop(0, D_MODEL // ADA_RC, body, jnp.zeros((1, ADA_TN), F32))
    o_ref[0] = acc + b_ref[0]


def _adaln(c, ada_w, ada_b):
    L = ada_w.shape[0]
    n_out = ada_w.shape[2]
    return pl.pallas_call(
        _adaln_kernel,
        grid=(L, n_out // ADA_TN),
        in_specs=[
            pl.BlockSpec((D_MODEL, 1), lambda l, j: (0, 0)),
            pl.BlockSpec((1, D_MODEL, ADA_TN), lambda l, j: (l, 0, j)),
            pl.BlockSpec((1, 1, ADA_TN), lambda l, j: (l, 0, j)),
        ],
        out_specs=pl.BlockSpec((1, 1, ADA_TN), lambda l, j: (l, 0, j)),
        out_shape=jax.ShapeDtypeStruct((L, 1, n_out), F32),
        compiler_params=_cparams(2),
        name="adaln",
    )(c.reshape(D_MODEL, 1), ada_w, ada_b.reshape(L, 1, n_out))


def _modulated_norm(x, g, sc, sh):
    ms = jnp.mean(x * x, axis=-1, keepdims=True)
    y = x * lax.rsqrt(ms + EPS) * g
    return y * (1.0 + sc) + sh


def _inproj_kernel(x_ref, g_ref, sh_ref, sc_ref, w_ref, o_ref, h_ref):
    @pl.when(pl.program_id(1) == 0)
    def _():
        h_ref[...] = _modulated_norm(x_ref[...], g_ref[...], sc_ref[...], sh_ref[...]).astype(BF16)

    o_ref[...] = jnp.dot(h_ref[...], w_ref[...], preferred_element_type=F32)


def _inproj(x, norm_g, mod, w_in_p):
    n = x.shape[0]
    tm = min(512, n)
    tn = 1024
    return pl.pallas_call(
        _inproj_kernel,
        grid=(n // tm, D_IN_PAD // tn),
        in_specs=[
            pl.BlockSpec((tm, D_MODEL), lambda i, j: (i, 0)),
            pl.BlockSpec((1, D_MODEL), lambda i, j: (0, 0)),
            pl.BlockSpec((1, D_MODEL), lambda i, j: (0, 0)),
            pl.BlockSpec((1, D_MODEL), lambda i, j: (0, 1)),
            pl.BlockSpec((D_MODEL, tn), lambda i, j: (0, j)),
        ],
        out_specs=pl.BlockSpec((tm, tn), lambda i, j: (i, j)),
        out_shape=jax.ShapeDtypeStruct((n, D_IN_PAD), F32),
        scratch_shapes=[pltpu.VMEM((tm, D_MODEL), BF16)],
        compiler_params=_cparams(2),
        name="inproj",
    )(x, norm_g, mod, mod, w_in_p)


PREP_TM = 512


def _rot_swap(x):
    lane = lax.broadcasted_iota(I32, x.shape, 1)
    fwd = pltpu.roll(x, LANES - 32, 1)
    bwd = pltpu.roll(x, 32, 1)
    return jnp.where((lane & 63) < 32, fwd, bwd)


def _rope_tile(x, cos, sin_signed):
    return x * cos + _rot_swap(x) * sin_signed


def _sumsq(x):
    return jnp.sum(x * x, axis=-1, keepdims=True)


def _prep_kernel(p_ref, wq_ref, wkv_ref, gql_ref, gkvl_ref, gmq_ref, gmk_ref, ggq_ref, ggk_ref,
                 gnq_ref, gnk_ref, cm_ref, sm_ref, ca_ref, sa_ref,
                 qm_ref, km_ref, vmt_ref, qg_ref, kg_ref, vgt_ref, qn_ref, kn_ref, vn_ref):
    mla_scale = MLA_QK ** -0.5 * LOG2E
    gqa_scale = HEAD_DIM ** -0.5 * LOG2E
    hd_scale = HEAD_DIM ** -0.5
    tm = p_ref.shape[0]
    ones_rows = (lax.broadcasted_iota(I32, (DV_EXT - HEAD_DIM, tm), 0) == 0).astype(BF16)
    cm, sm = cm_ref[...], sm_ref[...]
    ca, sa = ca_ref[...], sa_ref[...]

    a_q = p_ref[:, C_AQ:C_AQ + MLA_Q_RANK]
    a_q = a_q * lax.rsqrt(_sumsq(a_q) / MLA_Q_RANK + EPS) * gql_ref[...]
    q = jnp.dot(a_q.astype(BF16), wq_ref[...], preferred_element_type=F32)
    gq_n, gq_r = gmq_ref[:, :LANES], gmq_ref[:, LANES:]
    for h in range(MLA_HEADS):
        nope = q[:, h * LANES:(h + 1) * LANES]
        ropp = q[:, (MLA_HEADS + h) * LANES:(MLA_HEADS + h + 1) * LANES]
        rinv = lax.rsqrt((_sumsq(nope) + _sumsq(ropp)) / MLA_QK + EPS)
        qm_ref[h, :, :LANES] = (nope * rinv * gq_n * mla_scale).astype(BF16)
        qm_ref[h, :, LANES:] = (_rope_tile(ropp * rinv * gq_r, cm, sm) * mla_scale).astype(BF16)

    a_kv = p_ref[:, C_AKV:C_AKV + MLA_KV_RANK]
    a_kv = a_kv * lax.rsqrt(_sumsq(a_kv) / MLA_KV_RANK + EPS) * gkvl_ref[...]
    kv = jnp.dot(a_kv.astype(BF16), wkv_ref[...], preferred_element_type=F32)
    pe = p_ref[:, C_APE:C_APE + PE_PAD]
    ss_pe = _sumsq(pe)
    gk_n, gk_r = gmk_ref[:, :LANES], gmk_ref[:, LANES:]
    pe_rot = _rope_tile(pe * gk_r, cm, sm)
    for h in range(MLA_HEADS):
        nope = kv[:, h * LANES:(h + 1) * LANES]
        v = kv[:, (MLA_HEADS + h) * LANES:(MLA_HEADS + h + 1) * LANES]
        rinv = lax.rsqrt((_sumsq(nope) + ss_pe) / MLA_QK + EPS)
        km_ref[h, :, :LANES] = (nope * rinv * gk_n).astype(BF16)
        km_ref[h, :, LANES:] = (pe_rot * rinv).astype(BF16)
        vmt_ref[h, 0, :HEAD_DIM] = v.T.astype(BF16)
        vmt_ref[h, 0, HEAD_DIM:] = ones_rows

    def head_norm(col, g):
        t = p_ref[:, col:col + HEAD_DIM]
        return t * lax.rsqrt(_sumsq(t) / HEAD_DIM + EPS) * g

    for h in range(GQA_HEADS):
        t = head_norm(C_GQ + h * HEAD_DIM, ggq_ref[...])
        qg_ref[h] = (_rope_tile(t, ca, sa) * gqa_scale).astype(BF16)
    for h in range(GQA_KV_HEADS):
        t = head_norm(C_GK + h * HEAD_DIM, ggk_ref[...])
        kg_ref[h] = _rope_tile(t, ca, sa).astype(BF16)
        vgt_ref[h, 0, :HEAD_DIM] = p_ref[:, C_GV + h * HEAD_DIM:C_GV + (h + 1) * HEAD_DIM].T.astype(BF16)
        vgt_ref[h, 0, HEAD_DIM:] = ones_rows

    for h in range(NA_HEADS):
        qn_ref[h] = (head_norm(C_NQ + h * HEAD_DIM, gnq_ref[...]) * hd_scale).astype(BF16)
        kn_ref[h] = head_norm(C_NK + h * HEAD_DIM, gnk_ref[...]).astype(BF16)
        vn_ref[h] = p_ref[:, C_NV + h * HEAD_DIM:C_NV + (h + 1) * HEAD_DIM].astype(BF16)


def _prep(proj, wq, wkv, gql, gkvl, gmq, gmk, ggq, ggk, gnq, gnk, cm, sm, ca, sa):
    n = proj.shape[0]
    tm = min(PREP_TM, n)
    nb = n // tm

    def full(a):
        return pl.BlockSpec(a.shape, lambda i: (0,) * a.ndim)

    def rows(width):
        return pl.BlockSpec((tm, width), lambda i: (i, 0))

    def heads(h, width):
        return pl.BlockSpec((h, tm, width), lambda i: (0, i, 0))

    def heads_t(h):
        return pl.BlockSpec((h, 1, DV_EXT, tm), lambda i: (0, i, 0, 0))

    out_shape = (
        jax.ShapeDtypeStruct((MLA_HEADS, n, MLA_QK_PAD), BF16),
        jax.ShapeDtypeStruct((MLA_HEADS, n, MLA_QK_PAD), BF16),
        jax.ShapeDtypeStruct((MLA_HEADS, nb, DV_EXT, tm), BF16),
        jax.ShapeDtypeStruct((GQA_HEADS, n, HEAD_DIM), BF16),
        jax.ShapeDtypeStruct((GQA_KV_HEADS, n, HEAD_DIM), BF16),
        jax.ShapeDtypeStruct((GQA_KV_HEADS, nb, DV_EXT, tm), BF16),
        jax.ShapeDtypeStruct((NA_HEADS, n, HEAD_DIM), BF16),
        jax.ShapeDtypeStruct((NA_HEADS, n, HEAD_DIM), BF16),
        jax.ShapeDtypeStruct((NA_HEADS, n, HEAD_DIM), BF16),
    )
    out_specs = (
        heads(MLA_HEADS, MLA_QK_PAD), heads(MLA_HEADS, MLA_QK_PAD), heads_t(MLA_HEADS),
        heads(GQA_HEADS, HEAD_DIM), heads(GQA_KV_HEADS, HEAD_DIM), heads_t(GQA_KV_HEADS),
        heads(NA_HEADS, HEAD_DIM), heads(NA_HEADS, HEAD_DIM), heads(NA_HEADS, HEAD_DIM),
    )
    small = (wq, wkv, gql, gkvl, gmq, gmk, ggq, ggk, gnq, gnk)
    return pl.pallas_call(
        _prep_kernel,
        grid=(nb,),
        in_specs=[rows(D_IN_PAD)] + [full(a) for a in small] + [rows(LANES)] * 4,
        out_specs=out_specs,
        out_shape=out_shape,
        compiler_params=_cparams(1),
        name="prep",
    )(proj, *small, cm, sm, ca, sa)


FLASH_TQ = 512
FLASH_UNROLL = 4


def _flash_kernel(q_ref, k_ref, vt_ref, o_ref, m_ref, acc_ref, s_ref, cm_ref, *, n_chunks, tk, unroll):
    dv = acc_ref.shape[0] - (DV_EXT - HEAD_DIM)
    m_ref[...] = jnp.full(m_ref.shape, -jnp.inf, F32)
    acc_ref[...] = jnp.zeros(acc_ref.shape, F32)

    def scores(c, slot):
        kc = k_ref[0, pl.ds(pl.multiple_of(c * tk, tk), tk), :]
        s = lax.dot_general(kc, q_ref[0], (((1,), (1,)), ((), ())), preferred_element_type=F32)
        s_ref[slot] = s
        cm_ref[slot] = jnp.max(s, axis=0, keepdims=True)

    def update(c, slot):
        m_prev = m_ref[...]
        m_new = jnp.maximum(m_prev, cm_ref[slot])
        p = jnp.exp2(s_ref[slot] - m_new).astype(BF16)
        alpha = jnp.exp2(m_prev - m_new)
        acc_ref[...] = alpha * acc_ref[...] + jnp.dot(vt_ref[0, c], p, preferred_element_type=F32)
        m_ref[...] = m_new

    scores(0, 0)

    def body(j, carry):
        c0 = unroll * j
        for u in range(unroll):
            nxt = c0 + u + 1
            if u == unroll - 1:
                nxt = jnp.where(nxt == n_chunks, 0, nxt)
            scores(nxt, (u + 1) % 2)
            update(c0 + u, u % 2)
        return carry

    lax.fori_loop(0, n_chunks // unroll, body, 0)
    acc = acc_ref[...]
    o_ref[...] = (acc[:dv] / acc[dv:dv + 1]).T


def _flash(q, k, vt):
    H, S, dq = q.shape
    hk, n_chunks, dv_ext, tk = vt.shape
    dv = dv_ext - (DV_EXT - HEAD_DIM)
    group = H // hk
    tq = min(FLASH_TQ, S)
    unroll = FLASH_UNROLL if n_chunks % FLASH_UNROLL == 0 else 2
    assert n_chunks % unroll == 0
    return pl.pallas_call(
        functools.partial(_flash_kernel, n_chunks=n_chunks, tk=tk, unroll=unroll),
        grid=(H, S // tq),
        in_specs=[
            pl.BlockSpec((1, tq, dq), lambda h, i: (h, i, 0)),
            pl.BlockSpec((1, S, dq), lambda h, i: (h // group, 0, 0)),
            pl.BlockSpec((1, n_chunks, dv_ext, tk), lambda h, i: (h // group, 0, 0, 0)),
        ],
        out_specs=pl.BlockSpec((tq, dv), lambda h, i: (i, h)),
        out_shape=jax.ShapeDtypeStruct((S, H * dv), F32),
        scratch_shapes=[pltpu.VMEM((1, tq), F32), pltpu.VMEM((dv_ext, tq), F32),
                        pltpu.VMEM((2, tk, tq), F32), pltpu.VMEM((2, 1, tq), F32)],
        compiler_params=_cparams(2),
        name="flash",
    )(q, k, vt)


NEG_MIN = float(np.finfo(np.float32).min)
RPB_ROWS = 2 * NA_KH - 1
RPB_COLS = 2 * NA_KW - 1


def _na_rel(kind, i, j):
    if kind == 0:
        return j - i + NA_KH - 1, j < NA_KH
    if kind == 1:
        return j - i + NA_KH // 2 - 1, i <= j <= i + NA_KH - 1
    return j - i - 1, j >= NA_KROWS - NA_KH


def _na_bias_kernel(rpb_ref, o_ref):
    h = pl.program_id(0)
    shape = (GRID_W, LANES)
    c = lax.broadcasted_iota(I32, shape, 0)
    kc = lax.broadcasted_iota(I32, shape, 1) & (GRID_W - 1)
    cs = jnp.clip(c - NA_KW // 2, 0, GRID_W - NA_KW)
    in_win = (kc >= cs) & (kc < cs + NA_KW)
    dc = kc - c + NA_KW - 1
    lane_lo = lax.broadcasted_iota(I32, shape, 1) < GRID_W
    masked = jnp.full(shape, NEG_MIN, F32)
    cache = {}

    def toeplitz(dr):
        if dr not in cache:
            t = jnp.zeros(shape, F32)
            base = (h * RPB_ROWS + dr) * RPB_COLS
            for d in range(RPB_COLS):
                t = jnp.where(dc == d, rpb_ref[base + d], t)
            cache[dr] = jnp.where(in_win, t, NEG_MIN)
        return cache[dr]

    def half(kind, i, j):
        dr, valid = _na_rel(kind, i, j)
        return toeplitz(dr) if valid else masked

    for kind in range(3):
        for i in range(NA_QROWS):
            for jp in range(NA_KROWS // 2):
                piece = jnp.where(lane_lo, half(kind, i, 2 * jp), half(kind, i, 2 * jp + 1))
                o_ref[kind, 0, i * GRID_W:(i + 1) * GRID_W, jp * LANES:(jp + 1) * LANES] = piece


def _na_bias(rpb):
    return pl.pallas_call(
        _na_bias_kernel,
        grid=(NA_HEADS,),
        in_specs=[pl.BlockSpec(memory_space=pltpu.SMEM)],
        out_specs=pl.BlockSpec((3, 1, NA_TQ, NA_TK), lambda h: (0, h, 0, 0)),
        out_shape=jax.ShapeDtypeStruct((3, NA_HEADS, NA_TQ, NA_TK), F32),
        compiler_params=_cparams(1),
        name="na_bias",
    )(rpb.reshape(-1))


def _na_kernel(q_ref, k_ref, v_ref, b_ref, o_ref, *, grid_rows):
    i = pl.program_id(1)
    kstart = jnp.clip(i * NA_QROWS - NA_KH // 2, 0, grid_rows - NA_KROWS)
    off = pl.multiple_of(kstart * GRID_W, GRID_W)
    kw = k_ref[0, pl.ds(off, NA_TK), :]
    vw = v_ref[0, pl.ds(off, NA_TK), :]
    s = lax.dot_general(q_ref[0], kw, (((1,), (1,)), ((), ())), preferred_element_type=F32)
    s = s + b_ref[0, 0]
    p = jnp.exp(s - jnp.max(s, axis=-1, keepdims=True))
    l = jnp.sum(p, axis=-1, keepdims=True)
    o_ref[...] = jnp.dot(p.astype(BF16), vw, preferred_element_type=F32) / l


def _na(q, k, v, bias):
    H, S, d = q.shape
    grid_rows = S // GRID_W
    nblk = grid_rows // NA_QROWS

    def bias_idx(h, i):
        return (jnp.where(i == 0, 0, jnp.where(i == nblk - 1, 2, 1)), h, 0, 0)

    return pl.pallas_call(
        functools.partial(_na_kernel, grid_rows=grid_rows),
        grid=(H, nblk),
        in_specs=[
            pl.BlockSpec((1, NA_TQ, d), lambda h, i: (h, i, 0)),
            pl.BlockSpec((1, S, d), lambda h, i: (h, 0, 0)),
            pl.BlockSpec((1, S, d), lambda h, i: (h, 0, 0)),
            pl.BlockSpec((1, 1, NA_TQ, NA_TK), bias_idx),
        ],
        out_specs=pl.BlockSpec((NA_TQ, d), lambda h, i: (i, h)),
        out_shape=jax.ShapeDtypeStruct((S, H * d), F32),
        compiler_params=_cparams(2),
        name="na",
    )(q, k, v, bias)


W_MLA = MLA_HEADS * MLA_V
W_GQA = GQA_HEADS * HEAD_DIM
W_NA = NA_HEADS * HEAD_DIM


def _outproj_kernel(oa_ref, ob_ref, oc_ref, g_ref, w_ref, x_ref, g1_ref, o_ref):
    def gnorm(o, g):
        return (o * lax.rsqrt(jnp.mean(o * o, axis=-1, keepdims=True) + EPS) * g).astype(BF16)

    a = gnorm(oa_ref[...], g_ref[:, :W_MLA])
    b = gnorm(ob_ref[...], g_ref[:, W_MLA:W_MLA + W_GQA])
    c = gnorm(oc_ref[...], g_ref[:, W_MLA + W_GQA:])
    y = jnp.dot(a, w_ref[:W_MLA, :], preferred_element_type=F32)
    y = y + jnp.dot(b, w_ref[W_MLA:W_MLA + W_GQA, :], preferred_element_type=F32)
    y = y + jnp.dot(c, w_ref[W_MLA + W_GQA:, :], preferred_element_type=F32)
    o_ref[...] = x_ref[...] + g1_ref[...] * y


def _outproj(oa, ob, oc, gains, w_out, x, mod):
    n = x.shape[0]
    tm = min(256, n)
    return pl.pallas_call(
        _outproj_kernel,
        grid=(n // tm,),
        in_specs=[
            pl.BlockSpec((tm, W_MLA), lambda i: (i, 0)),
            pl.BlockSpec((tm, W_GQA), lambda i: (i, 0)),
            pl.BlockSpec((tm, W_NA), lambda i: (i, 0)),
            pl.BlockSpec((1, D_MODEL), lambda i: (0, 0)),
            pl.BlockSpec((D_MODEL, D_MODEL), lambda i: (0, 0)),
            pl.BlockSpec((tm, D_MODEL), lambda i: (i, 0)),
            pl.BlockSpec((1, D_MODEL), lambda i: (0, 2)),
        ],
        out_specs=pl.BlockSpec((tm, D_MODEL), lambda i: (i, 0)),
        out_shape=jax.ShapeDtypeStruct((n, D_MODEL), F32),
        compiler_params=_cparams(1),
        name="outproj",
    )(oa, ob, oc, gains, w_out, x, mod)


ROUTER_TM = 512


def _store_pitched(ref, val):
    rows = val.shape[0]
    for s in range(ROW_SLABS):
        ref[pl.ds(s, rows, stride=ROW_PITCH), :] = val[:, s * LANES:(s + 1) * LANES]
    for s in range(ROW_SLABS, ROW_PITCH):
        ref[pl.ds(s, rows, stride=ROW_PITCH), :] = jnp.zeros((rows, LANES), val.dtype)


def _load_pitched_slab(ref, s, rows):
    return ref[pl.ds(s, rows, stride=ROW_PITCH), :]


def _first_argmax(v, vmax, n):
    idx = lax.broadcasted_iota(I32, v.shape, 0)
    return jnp.min(jnp.where(v == vmax, idx, n), axis=0, keepdims=True)


def _router_kernel(x_ref, g_ref, sh_ref, sc_ref, w_ref, b_ref, u_ref,
                   h_ref, ri_ref, rw_ref, cnt_ref, carry_ref):
    @pl.when(pl.program_id(0) == 0)
    def _():
        carry_ref[...] = jnp.zeros(carry_ref.shape, F32)

    h = _modulated_norm(x_ref[...], g_ref[...], sc_ref[...], sh_ref[...])
    _store_pitched(h_ref, h)
    logits = lax.dot_general(w_ref[...], h, (((1,), (1,)), ((), ())),
                             precision=lax.Precision.HIGHEST,
                             preferred_element_type=F32) + b_ref[...]
    tm = logits.shape[1]

    gl = logits[0:N_GROUPS]
    gmax = jnp.max(gl, axis=0, keepdims=True)
    gexp = jnp.exp(gl - gmax)
    gprob = gexp / jnp.sum(gexp, axis=0, keepdims=True)
    p_g = jnp.max(gprob, axis=0, keepdims=True)
    g_top = _first_argmax(gprob, p_g, N_GROUPS)

    e_in = jnp.zeros((EXPERTS_PER_GROUP, tm), F32)
    for g in range(N_GROUPS):
        lo = ROUTER_E0 + g * EXPERTS_PER_GROUP
        e_in = jnp.where(g_top == g, logits[lo:lo + EXPERTS_PER_GROUP], e_in)
    eexp = jnp.exp(e_in - jnp.max(e_in, axis=0, keepdims=True))
    eprob = eexp / jnp.sum(eexp, axis=0, keepdims=True)
    v1 = jnp.max(eprob, axis=0, keepdims=True)
    i1 = _first_argmax(eprob, v1, EXPERTS_PER_GROUP)
    row = lax.broadcasted_iota(I32, eprob.shape, 0)
    rest = jnp.where(row == i1, -1.0, eprob)
    v2 = jnp.max(rest, axis=0, keepdims=True)
    i2 = _first_argmax(rest, v2, EXPERTS_PER_GROUP)
    wsum = v1 + v2
    w1 = v1 / wsum * p_g
    w2 = v2 / wsum * p_g
    e1 = g_top * EXPERTS_PER_GROUP + i1
    e2 = g_top * EXPERTS_PER_GROUP + i2

    erow = lax.broadcasted_iota(I32, (N_EXPERTS, tm), 0)
    oh1 = (erow == e1).astype(F32)
    oh2 = (erow == e2).astype(F32)
    oh = oh1 + oh2
    before = jnp.dot(oh.astype(BF16), u_ref[...], preferred_element_type=F32) + carry_ref[...]
    r1 = jnp.sum(oh1 * before, axis=0, keepdims=True)
    r2 = jnp.sum(oh2 * before, axis=0, keepdims=True)
    carry_ref[...] = carry_ref[...] + jnp.sum(oh, axis=1, keepdims=True)
    cnt_ref[...] = jnp.broadcast_to(carry_ref[...], cnt_ref.shape)

    ri_ref[...] = jnp.zeros(ri_ref.shape, I32)
    ri_ref[0:1, :] = e1
    ri_ref[1:2, :] = e2
    ri_ref[2:3, :] = r1.astype(I32)
    ri_ref[3:4, :] = r2.astype(I32)
    rw_ref[...] = jnp.zeros(rw_ref.shape, F32)
    rw_ref[0:1, :] = w1
    rw_ref[1:2, :] = w2


def _router(x, norm_g, mod, w_r, b_r, upper):
    n = x.shape[0]
    tm = upper.shape[0]
    return pl.pallas_call(
        _router_kernel,
        grid=(n // tm,),
        in_specs=[
            pl.BlockSpec((tm, D_MODEL), lambda i: (i, 0)),
            pl.BlockSpec((1, D_MODEL), lambda i: (0, 0)),
            pl.BlockSpec((1, D_MODEL), lambda i: (0, 3)),
            pl.BlockSpec((1, D_MODEL), lambda i: (0, 4)),
            pl.BlockSpec((ROUTER_ROWS, D_MODEL), lambda i: (0, 0)),
            pl.BlockSpec((ROUTER_ROWS, 1), lambda i: (0, 0)),
            pl.BlockSpec((tm, tm), lambda i: (0, 0)),
        ],
        out_specs=(
            pl.BlockSpec((tm * ROW_PITCH, LANES), lambda i: (i, 0)),
            pl.BlockSpec((SUBLANES, tm), lambda i: (0, i)),
            pl.BlockSpec((SUBLANES, tm), lambda i: (0, i)),
            pl.BlockSpec((N_EXPERTS, LANES), lambda i: (0, 0)),
        ),
        out_shape=(
            jax.ShapeDtypeStruct((n * ROW_PITCH, LANES), F32),
            jax.ShapeDtypeStruct((SUBLANES, n), I32),
            jax.ShapeDtypeStruct((SUBLANES, n), F32),
            jax.ShapeDtypeStruct((N_EXPERTS, LANES), F32),
        ),
        scratch_shapes=[pltpu.VMEM((N_EXPERTS, 1), F32)],
        compiler_params=_cparams(1),
        name="router",
    )(x, norm_g, mod, mod, w_r, b_r, upper)


GATHER_UNROLL = 8


def _row_gather_start(src_hbm, idx_ref, base, dst, sem, n_rows):
    def body(r, carry):
        off = idx_ref[base + r]
        pltpu.make_async_copy(src_hbm.at[pl.ds(off, ROW_SLABS)],
                              dst.at[pl.ds(r * ROW_PITCH, ROW_SLABS)], sem).start()
        return carry

    lax.fori_loop(0, n_rows, body, 0, unroll=GATHER_UNROLL)


def _row_gather_wait(src_hbm, dst, sem, n_rows):
    n = n_rows * ROW_SLABS
    pltpu.make_async_copy(src_hbm.at[pl.ds(0, n)], dst.at[pl.ds(0, n)], sem).wait()


def _expert_kernel(src_ref, blk_ref, nused_ref, h_hbm, wg_ref, wu_ref, wd_ref, y_ref, xg_ref, sem):
    b = pl.program_id(0)
    nused = nused_ref[0]
    slot = b % 2

    @pl.when((b == 0) & (nused > 0))
    def _():
        _row_gather_start(h_hbm, src_ref, 0, xg_ref.at[0], sem.at[0], MOE_ROWS)

    @pl.when(b + 1 < nused)
    def _():
        _row_gather_start(h_hbm, src_ref, (b + 1) * MOE_ROWS, xg_ref.at[1 - slot], sem.at[1 - slot],
                          MOE_ROWS)

    @pl.when(b < nused)
    def _():
        _row_gather_wait(h_hbm, xg_ref.at[slot], sem.at[slot], MOE_ROWS)
        xs = xg_ref.at[slot]
        x = jnp.concatenate(
            [_load_pitched_slab(xs, s, MOE_ROWS).astype(BF16) for s in range(ROW_SLABS)], axis=1)
        g = jnp.dot(x, wg_ref[0].astype(BF16), preferred_element_type=F32)
        u = jnp.dot(x, wu_ref[0].astype(BF16), preferred_element_type=F32)
        a = (g * (1.0 / (1.0 + jnp.exp(-g))) * u).astype(BF16)
        _store_pitched(y_ref, jnp.dot(a, wd_ref[0].astype(BF16), preferred_element_type=F32))

    @pl.when(b >= nused)
    def _():
        y_ref[...] = jnp.zeros(y_ref.shape, F32)


def _experts(src_off, blk_e, nused, h, w_gate, w_up, w_down):
    nb = blk_e.shape[0]
    grid_spec = pltpu.PrefetchScalarGridSpec(
        num_scalar_prefetch=3,
        grid=(nb,),
        in_specs=[
            pl.BlockSpec(memory_space=pl.ANY),
            pl.BlockSpec((1, D_MODEL, D_EXPERT), lambda b, src, blk, nu: (blk[b], 0, 0)),
            pl.BlockSpec((1, D_MODEL, D_EXPERT), lambda b, src, blk, nu: (blk[b], 0, 0)),
            pl.BlockSpec((1, D_EXPERT, D_MODEL), lambda b, src, blk, nu: (blk[b], 0, 0)),
        ],
        out_specs=pl.BlockSpec((MOE_ROWS * ROW_PITCH, LANES), lambda b, src, blk, nu: (b, 0)),
        scratch_shapes=[pltpu.VMEM((2, MOE_ROWS * ROW_PITCH, LANES), F32), pltpu.SemaphoreType.DMA((2,))],
    )
    return pl.pallas_call(
        _expert_kernel,
        grid_spec=grid_spec,
        out_shape=jax.ShapeDtypeStruct((nb * MOE_ROWS * ROW_PITCH, LANES), F32),
        compiler_params=_cparams(1),
        name="experts",
    )(src_off, blk_e, nused, h, w_gate, w_up, w_down)


COMB_TM = 256


def _combine_kernel(d1_ref, d2_ref, y_hbm, x_ref, w1_ref, w2_ref, g2_ref, o_ref, ya_ref, yb_ref, sem):
    i = pl.program_id(0)
    n = pl.num_programs(0)
    slot = i % 2

    def start(blk, s):
        _row_gather_start(y_hbm, d1_ref, blk * COMB_TM, ya_ref.at[s], sem.at[0, s], COMB_TM)
        _row_gather_start(y_hbm, d2_ref, blk * COMB_TM, yb_ref.at[s], sem.at[1, s], COMB_TM)

    @pl.when(i == 0)
    def _():
        start(0, 0)

    @pl.when(i + 1 < n)
    def _():
        start(i + 1, 1 - slot)

    _row_gather_wait(y_hbm, ya_ref.at[slot], sem.at[0, slot], COMB_TM)
    _row_gather_wait(y_hbm, yb_ref.at[slot], sem.at[1, slot], COMB_TM)
    w1, w2 = w1_ref[...], w2_ref[...]
    ya, yb = ya_ref.at[slot], yb_ref.at[slot]
    for s in range(ROW_SLABS):
        cols = slice(s * LANES, (s + 1) * LANES)
        y = w1 * _load_pitched_slab(ya, s, COMB_TM) + w2 * _load_pitched_slab(yb, s, COMB_TM)
        o_ref[:, cols] = x_ref[:, cols] + g2_ref[:, cols] * y


def _combine(dest1, dest2, ybuf, x, w1, w2, mod):
    n = x.shape[0]
    grid_spec = pltpu.PrefetchScalarGridSpec(
        num_scalar_prefetch=2,
        grid=(n // COMB_TM,),
        in_specs=[
            pl.BlockSpec(memory_space=pl.ANY),
            pl.BlockSpec((COMB_TM, D_MODEL), lambda i, d1, d2: (i, 0)),
            pl.BlockSpec((COMB_TM, 1), lambda i, d1, d2: (i, 0)),
            pl.BlockSpec((COMB_TM, 1), lambda i, d1, d2: (i, 0)),
            pl.BlockSpec((1, D_MODEL), lambda i, d1, d2: (0, 5)),
        ],
        out_specs=pl.BlockSpec((COMB_TM, D_MODEL), lambda i, d1, d2: (i, 0)),
        scratch_shapes=[
            pltpu.VMEM((2, COMB_TM * ROW_PITCH, LANES), F32),
            pltpu.VMEM((2, COMB_TM * ROW_PITCH, LANES), F32),
            pltpu.SemaphoreType.DMA((2, 2)),
        ],
    )
    return pl.pallas_call(
        _combine_kernel,
        grid_spec=grid_spec,
        out_shape=jax.ShapeDtypeStruct((n, D_MODEL), F32),
        compiler_params=_cparams(1),
        name="combine",
    )(dest1, dest2, ybuf, x, w1, w2, mod)


def _rope_angles(pos, d):
    inv = ROPE_THETA ** (-jnp.arange(0, d, 2, dtype=F32) / d)
    return pos.astype(F32)[:, None] * inv[None, :]


def _rope_tables(S):
    t = jnp.arange(S)
    z = jnp.zeros((S, MLA_ROPE), F32)
    a = _rope_angles(t, MLA_ROPE)
    cm = jnp.concatenate([jnp.cos(a), jnp.cos(a), z], axis=1)
    sm = jnp.concatenate([-jnp.sin(a), jnp.sin(a), z], axis=1)
    ar = _rope_angles(t // GRID_W, HEAD_DIM // 2)
    ac = _rope_angles(t % GRID_W, HEAD_DIM // 2)
    ca = jnp.concatenate([jnp.cos(ar), jnp.cos(ar), jnp.cos(ac), jnp.cos(ac)], axis=1)
    sa = jnp.concatenate([-jnp.sin(ar), jnp.sin(ar), -jnp.sin(ac), jnp.sin(ac)], axis=1)
    return cm, sm, ca, sa


def _pad_lanes(a, width):
    return jnp.concatenate([a, jnp.zeros(a.shape[:-1] + (width - a.shape[-1],), a.dtype)], axis=-1)


def _layer_weights(w_in, w_uq, w_ukv, qk_q_g, qk_k_g):
    split = C_APE + MLA_ROPE
    w_in_p = jnp.concatenate(
        [w_in[:, :split], jnp.zeros((D_MODEL, PE_PAD - MLA_ROPE), F32), w_in[:, split:]], axis=1
    ).astype(BF16)
    wq = w_uq.reshape(MLA_Q_RANK, MLA_HEADS, MLA_QK)
    wq_n = wq[:, :, :MLA_NOPE].reshape(MLA_Q_RANK, MLA_HEADS * LANES)
    wq_r = _pad_lanes(wq[:, :, MLA_NOPE:], LANES).reshape(MLA_Q_RANK, MLA_HEADS * LANES)
    wq_p = jnp.concatenate([wq_n, wq_r], axis=1).astype(BF16)
    wkv = w_ukv.reshape(MLA_KV_RANK, MLA_HEADS, MLA_NOPE + MLA_V)
    wkv_p = jnp.concatenate(
        [wkv[:, :, :MLA_NOPE].reshape(MLA_KV_RANK, -1), wkv[:, :, MLA_NOPE:].reshape(MLA_KV_RANK, -1)],
        axis=1).astype(BF16)
    gmq = _pad_lanes(qk_q_g[None, :], MLA_QK_PAD)
    gmk = _pad_lanes(qk_k_g[None, :], MLA_QK_PAD)
    return w_in_p, wq_p, wkv_p, gmq, gmk


def _routing_plan(ri, cnt, n_blocks):
    n = ri.shape[1]
    e1, e2, r1, r2 = ri[0], ri[1], ri[2], ri[3]
    counts = cnt[:, 0].astype(I32)
    padded = ((counts + MOE_ROWS - 1) // MOE_ROWS) * MOE_ROWS
    pad_end = jnp.cumsum(padded)
    pad_start = pad_end - padded
    dest1 = pad_start[e1] + r1
    dest2 = pad_start[e2] + r2
    tok_off = jnp.arange(n, dtype=I32) * ROW_PITCH
    src_off = jnp.zeros((n_blocks * MOE_ROWS,), I32).at[dest1].set(tok_off).at[dest2].set(tok_off)
    blk_start = jnp.arange(n_blocks, dtype=I32) * MOE_ROWS
    blk_e = jnp.minimum(jnp.sum((pad_end[None, :] <= blk_start[:, None]).astype(I32), axis=1), N_EXPERTS - 1)
    nused = (pad_end[-1:] // MOE_ROWS).astype(I32)
    return dest1 * ROW_PITCH, dest2 * ROW_PITCH, src_off, blk_e, nused


def kernel(x, c, ada_w, ada_b, norm1_g, norm2_g, w_in, mla_q_norm_g, mla_w_uq, mla_kv_norm_g, mla_w_ukv, mla_qk_q_g, mla_qk_k_g, gqa_q_g, gqa_k_g, na_q_g, na_k_g, na_rpb, mix_out_norm_g, w_out, router_group_w, router_group_b, router_expert_w, router_expert_b, expert_w_gate, expert_w_up, expert_w_down):
    B, S, D = x.shape
    assert B == 1 and D == D_MODEL and S % (GRID_W * NA_QROWS) == 0 and S // GRID_W >= NA_KROWS
    depth = ada_w.shape[0]
    n = B * S
    xt = x.reshape(n, D)

    mod_all = _adaln(c, ada_w, ada_b)
    cm, sm, ca, sa = _rope_tables(S)
    tm_r = min(ROUTER_TM, n)
    upper = (jnp.arange(tm_r)[:, None] < jnp.arange(tm_r)[None, :]).astype(BF16)
    n_blocks = -(-(n * 2 + N_EXPERTS * (MOE_ROWS - 1)) // MOE_ROWS)
    wg_all = expert_w_gate.reshape(depth * N_EXPERTS, D_MODEL, D_EXPERT)
    wu_all = expert_w_up.reshape(depth * N_EXPERTS, D_MODEL, D_EXPERT)
    wd_all = expert_w_down.reshape(depth * N_EXPERTS, D_EXPERT, D_MODEL)

    def row(v):
        return v[None, :]

    for l in range(depth):
        mod = mod_all[l]
        w_in_p, wq_p, wkv_p, gmq, gmk = _layer_weights(
            w_in[l], mla_w_uq[l], mla_w_ukv[l], mla_qk_q_g[l], mla_qk_k_g[l])

        proj = _inproj(xt, row(norm1_g[l]), mod, w_in_p)
        qm, km, vmt, qg, kg, vgt, qn, kn, vn = _prep(
            proj, wq_p, wkv_p, row(mla_q_norm_g[l]), row(mla_kv_norm_g[l]), gmq, gmk,
            row(gqa_q_g[l]), row(gqa_k_g[l]), row(na_q_g[l]), row(na_k_g[l]), cm, sm, ca, sa)
        o_a = _flash(qm, km, vmt)
        o_b = _flash(qg, kg, vgt)
        o_c = _na(qn, kn, vn, _na_bias(na_rpb[l]))
        xt = _outproj(o_a, o_b, o_c, row(mix_out_norm_g[l]), w_out[l].astype(BF16), xt, mod)

        w_r = jnp.concatenate(
            [router_group_w[l].T, jnp.zeros((ROUTER_E0 - N_GROUPS, D), F32), router_expert_w[l].T], axis=0)
        b_r = jnp.concatenate(
            [router_group_b[l], jnp.zeros((ROUTER_E0 - N_GROUPS,), F32), router_expert_b[l]])[:, None]
        h2, ri, rw, cnt = _router(xt, row(norm2_g[l]), mod, w_r, b_r, upper)
        off1, off2, src_off, blk_e, nused = _routing_plan(ri, cnt, n_blocks)
        ybuf = _experts(src_off, blk_e + l * N_EXPERTS, nused, h2, wg_all, wu_all, wd_all)
        xt = _combine(off1, off2, ybuf, xt, rw[0][:, None], rw[1][:, None], mod)

    return xt.reshape(B, S, D)
```

```python
import functools

import jax
import jax.numpy as jnp
import numpy as np
from jax import lax
from jax.experimental import pallas as pl
from jax.experimental.pallas import tpu as pltpu

F32 = jnp.float32
BF16 = jnp.bfloat16
I32 = jnp.int32

D_MODEL = 2048
GRID_W = 64
HEAD_DIM = 128
ROPE_THETA = 10000.0
EPS = 1e-6
MLA_HEADS = 6
MLA_Q_RANK = 384
MLA_KV_RANK = 256
MLA_NOPE = 128
MLA_ROPE = 64
MLA_V = 128
MLA_QK = MLA_NOPE + MLA_ROPE
GQA_HEADS = 4
GQA_KV_HEADS = 2
NA_HEADS = 6
NA_KH = 8
NA_KW = 16
N_GROUPS = 4
EXPERTS_PER_GROUP = 8
N_EXPERTS = N_GROUPS * EXPERTS_PER_GROUP
D_EXPERT = 512

LANES = 128
SUBLANES = 8
VMEM_LIMIT_BYTES = 56 * 1024 * 1024

PE_PAD = LANES
C_AQ = 0
C_AKV = C_AQ + MLA_Q_RANK
C_APE = C_AKV + MLA_KV_RANK
C_GQ = C_APE + PE_PAD
C_GK = C_GQ + GQA_HEADS * HEAD_DIM
C_GV = C_GK + GQA_KV_HEADS * HEAD_DIM
C_NQ = C_GV + GQA_KV_HEADS * HEAD_DIM
C_NK = C_NQ + NA_HEADS * HEAD_DIM
C_NV = C_NK + NA_HEADS * HEAD_DIM
D_IN_PAD = C_NV + NA_HEADS * HEAD_DIM
MLA_QK_PAD = 2 * LANES

NA_QROWS = 4
NA_KROWS = 12
NA_TQ = NA_QROWS * GRID_W
NA_TK = NA_KROWS * GRID_W

LOG2E = float(np.log2(np.e))
DV_EXT = HEAD_DIM + 16

MOE_ROWS = 256
ROW_SLABS = D_MODEL // LANES
ROW_PITCH = 20
ROUTER_ROWS = 40
ROUTER_E0 = 8


def _cparams(n_axes):
    return pltpu.CompilerParams(
        dimension_semantics=("arbitrary",) * n_axes,
        vmem_limit_bytes=VMEM_LIMIT_BYTES,
    )


ADA_TN = 1024
ADA_RC = 256


def _adaln_kernel(c_ref, w_ref, b_ref, o_ref):
    def body(i, acc):
        r0 = pl.multiple_of(i * ADA_RC, ADA_RC)
        c = c_ref[pl.ds(r0, ADA_RC), :]
        cond = c * (1.0 / (1.0 + jnp.exp(-c)))
        w = w_ref[0, pl.ds(r0, ADA_RC), :]
        return acc + jnp.sum(w * cond, axis=0, keepdims=True)

    acc = lax.fori_loop(0, D_MODEL // ADA_RC, body, jnp.zeros((1, ADA_TN), F32))
    o_ref[0] = acc + b_ref[0]


def _adaln(c, ada_w, ada_b):
    L = ada_w.shape[0]
    n_out = ada_w.shape[2]
    return pl.pallas_call(
        _adaln_kernel,
        grid=(L, n_out // ADA_TN),
        in_specs=[
            pl.BlockSpec((D_MODEL, 1), lambda l, j: (0, 0)),
            pl.BlockSpec((1, D_MODEL, ADA_TN), lambda l, j: (l, 0, j)),
            pl.BlockSpec((1, 1, ADA_TN), lambda l, j: (l, 0, j)),
        ],
        out_specs=pl.BlockSpec((1, 1, ADA_TN), lambda l, j: (l, 0, j)),
        out_shape=jax.ShapeDtypeStruct((L, 1, n_out), F32),
        compiler_params=_cparams(2),
        name="adaln",
    )(c.reshape(D_MODEL, 1), ada_w, ada_b.reshape(L, 1, n_out))


def _modulated_norm(x, g, sc, sh):
    ms = jnp.mean(x * x, axis=-1, keepdims=True)
    y = x * lax.rsqrt(ms + EPS) * g
    return y * (1.0 + sc) + sh


INPROJ_TM = 256
INPROJ_TN = 1024


def _inproj_kernel(x_ref, g_ref, sh_ref, sc_ref, w_ref, o_ref):
    h = _modulated_norm(x_ref[...], g_ref[...], sc_ref[...], sh_ref[...]).astype(BF16)
    for j in range(D_IN_PAD // INPROJ_TN):
        cols = slice(j * INPROJ_TN, (j + 1) * INPROJ_TN)
        o_ref[:, cols] = jnp.dot(h, w_ref[:, cols], preferred_element_type=F32)


def _inproj(x, norm_g, mod, w_in_p):
    n = x.shape[0]
    tm = min(INPROJ_TM, n)
    return pl.pallas_call(
        _inproj_kernel,
        grid=(n // tm,),
        in_specs=[
            pl.BlockSpec((tm, D_MODEL), lambda i: (i, 0)),
            pl.BlockSpec((1, D_MODEL), lambda i: (0, 0)),
            pl.BlockSpec((1, D_MODEL), lambda i: (0, 0)),
            pl.BlockSpec((1, D_MODEL), lambda i: (0, 1)),
            pl.BlockSpec((D_MODEL, D_IN_PAD), lambda i: (0, 0), pipeline_mode=pl.Buffered(1)),
        ],
        out_specs=pl.BlockSpec((tm, D_IN_PAD), lambda i: (i, 0)),
        out_shape=jax.ShapeDtypeStruct((n, D_IN_PAD), F32),
        compiler_params=_cparams(1),
        name="inproj",
    )(x, norm_g, mod, mod, w_in_p)


PREP_TM = 512


def _rot_swap(x):
    lane = lax.broadcasted_iota(I32, x.shape, 1)
    fwd = pltpu.roll(x, LANES - 32, 1)
    bwd = pltpu.roll(x, 32, 1)
    return jnp.where((lane & 63) < 32, fwd, bwd)


def _rope_tile(x, cos, sin_signed):
    return x * cos + _rot_swap(x) * sin_signed


def _sumsq(x):
    return jnp.sum(x * x, axis=-1, keepdims=True)


def _prep_kernel(p_ref, wq_ref, wkv_ref, gql_ref, gkvl_ref, gmq_ref, gmk_ref, ggq_ref, ggk_ref,
                 gnq_ref, gnk_ref, cm_ref, sm_ref, ca_ref, sa_ref,
                 qm_ref, km_ref, vmt_ref, qg_ref, kg_ref, vgt_ref, qn_ref, kn_ref, vn_ref):
    mla_scale = MLA_QK ** -0.5 * LOG2E
    gqa_scale = HEAD_DIM ** -0.5 * LOG2E
    hd_scale = HEAD_DIM ** -0.5
    tm = p_ref.shape[0]
    ones_rows = (lax.broadcasted_iota(I32, (DV_EXT - HEAD_DIM, tm), 0) == 0).astype(BF16)
    cm, sm = cm_ref[...], sm_ref[...]
    ca, sa = ca_ref[...], sa_ref[...]

    a_q = p_ref[:, C_AQ:C_AQ + MLA_Q_RANK]
    a_q = a_q * lax.rsqrt(_sumsq(a_q) / MLA_Q_RANK + EPS) * gql_ref[...]
    q = jnp.dot(a_q.astype(BF16), wq_ref[...], preferred_element_type=F32)
    gq_n, gq_r = gmq_ref[:, :LANES], gmq_ref[:, LANES:]
    for h in range(MLA_HEADS):
        nope = q[:, h * LANES:(h + 1) * LANES]
        ropp = q[:, (MLA_HEADS + h) * LANES:(MLA_HEADS + h + 1) * LANES]
        rinv = lax.rsqrt((_sumsq(nope) + _sumsq(ropp)) / MLA_QK + EPS)
        qm_ref[h, :, :LANES] = (nope * rinv * gq_n * mla_scale).astype(BF16)
        qm_ref[h, :, LANES:] = (_rope_tile(ropp * rinv * gq_r, cm, sm) * mla_scale).astype(BF16)

    a_kv = p_ref[:, C_AKV:C_AKV + MLA_KV_RANK]
    a_kv = a_kv * lax.rsqrt(_sumsq(a_kv) / MLA_KV_RANK + EPS) * gkvl_ref[...]
    kv = jnp.dot(a_kv.astype(BF16), wkv_ref[...], preferred_element_type=F32)
    pe = p_ref[:, C_APE:C_APE + PE_PAD]
    ss_pe = _sumsq(pe)
    gk_n, gk_r = gmk_ref[:, :LANES], gmk_ref[:, LANES:]
    pe_rot = _rope_tile(pe * gk_r, cm, sm)
    for h in range(MLA_HEADS):
        nope = kv[:, h * LANES:(h + 1) * LANES]
        v = kv[:, (MLA_HEADS + h) * LANES:(MLA_HEADS + h + 1) * LANES]
        rinv = lax.rsqrt((_sumsq(nope) + ss_pe) / MLA_QK + EPS)
        km_ref[h, :, :LANES] = (nope * rinv * gk_n).astype(BF16)
        km_ref[h, :, LANES:] = (pe_rot * rinv).astype(BF16)
        vmt_ref[h, 0, :HEAD_DIM] = v.T.astype(BF16)
        vmt_ref[h, 0, HEAD_DIM:] = ones_rows

    def head_norm(col, g):
        t = p_ref[:, col:col + HEAD_DIM]
        return t * lax.rsqrt(_sumsq(t) / HEAD_DIM + EPS) * g

    for h in range(GQA_HEADS):
        t = head_norm(C_GQ + h * HEAD_DIM, ggq_ref[...])
        qg_ref[h] = (_rope_tile(t, ca, sa) * gqa_scale).astype(BF16)
    for h in range(GQA_KV_HEADS):
        t = head_norm(C_GK + h * HEAD_DIM, ggk_ref[...])
        kg_ref[h] = _rope_tile(t, ca, sa).astype(BF16)
        vgt_ref[h, 0, :HEAD_DIM] = p_ref[:, C_GV + h * HEAD_DIM:C_GV + (h + 1) * HEAD_DIM].T.astype(BF16)
        vgt_ref[h, 0, HEAD_DIM:] = ones_rows

    for h in range(NA_HEADS):
        qn_ref[h] = (head_norm(C_NQ + h * HEAD_DIM, gnq_ref[...]) * hd_scale).astype(BF16)
        kn_ref[h] = head_norm(C_NK + h * HEAD_DIM, gnk_ref[...]).astype(BF16)
        vn_ref[h] = p_ref[:, C_NV + h * HEAD_DIM:C_NV + (h + 1) * HEAD_DIM].astype(BF16)


def _prep(proj, wq, wkv, gql, gkvl, gmq, gmk, ggq, ggk, gnq, gnk, cm, sm, ca, sa):
    n = proj.shape[0]
    tm = min(PREP_TM, n)
    nb = n // tm

    def full(a):
        return pl.BlockSpec(a.shape, lambda i: (0,) * a.ndim)

    def rows(width):
        return pl.BlockSpec((tm, width), lambda i: (i, 0))

    def heads(h, width):
        return pl.BlockSpec((h, tm, width), lambda i: (0, i, 0))

    def heads_t(h):
        return pl.BlockSpec((h, 1, DV_EXT, tm), lambda i: (0, i, 0, 0))

    out_shape = (
        jax.ShapeDtypeStruct((MLA_HEADS, n, MLA_QK_PAD), BF16),
        jax.ShapeDtypeStruct((MLA_HEADS, n, MLA_QK_PAD), BF16),
        jax.ShapeDtypeStruct((MLA_HEADS, nb, DV_EXT, tm), BF16),
        jax.ShapeDtypeStruct((GQA_HEADS, n, HEAD_DIM), BF16),
        jax.ShapeDtypeStruct((GQA_KV_HEADS, n, HEAD_DIM), BF16),
        jax.ShapeDtypeStruct((GQA_KV_HEADS, nb, DV_EXT, tm), BF16),
        jax.ShapeDtypeStruct((NA_HEADS, n, HEAD_DIM), BF16),
        jax.ShapeDtypeStruct((NA_HEADS, n, HEAD_DIM), BF16),
        jax.ShapeDtypeStruct((NA_HEADS, n, HEAD_DIM), BF16),
    )
    out_specs = (
        heads(MLA_HEADS, MLA_QK_PAD), heads(MLA_HEADS, MLA_QK_PAD), heads_t(MLA_HEADS),
        heads(GQA_HEADS, HEAD_DIM), heads(GQA_KV_HEADS, HEAD_DIM), heads_t(GQA_KV_HEADS),
        heads(NA_HEADS, HEAD_DIM), heads(NA_HEADS, HEAD_DIM), heads(NA_HEADS, HEAD_DIM),
    )
    small = (wq, wkv, gql, gkvl, gmq, gmk, ggq, ggk, gnq, gnk)
    return pl.pallas_call(
        _prep_kernel,
        grid=(nb,),
        in_specs=[rows(D_IN_PAD)] + [full(a) for a in small] + [rows(LANES)] * 4,
        out_specs=out_specs,
        out_shape=out_shape,
        compiler_params=_cparams(1),
        name="prep",
    )(proj, *small, cm, sm, ca, sa)


FLASH_TQ = 512
FLASH_UNROLL = 4


def _flash_kernel(q_ref, k_ref, vt_ref, o_ref, m_ref, acc_ref, s_ref, cm_ref, *, n_chunks, tk, unroll):
    dv = acc_ref.shape[0] - (DV_EXT - HEAD_DIM)
    m_ref[...] = jnp.full(m_ref.shape, -jnp.inf, F32)
    acc_ref[...] = jnp.zeros(acc_ref.shape, F32)

    def scores(c, slot):
        kc = k_ref[0, pl.ds(pl.multiple_of(c * tk, tk), tk), :]
        s = lax.dot_general(kc, q_ref[0], (((1,), (1,)), ((), ())), preferred_element_type=F32)
        s_ref[slot] = s
        cm_ref[slot] = jnp.max(s, axis=0, keepdims=True)

    def update(c, slot):
        m_prev = m_ref[...]
        m_new = jnp.maximum(m_prev, cm_ref[slot])
        p = jnp.exp2(s_ref[slot] - m_new).astype(BF16)
        alpha = jnp.exp2(m_prev - m_new)
        acc_ref[...] = alpha * acc_ref[...] + jnp.dot(vt_ref[0, c], p, preferred_element_type=F32)
        m_ref[...] = m_new

    scores(0, 0)

    def body(j, carry):
        c0 = unroll * j
        for u in range(unroll):
            nxt = c0 + u + 1
            if u == unroll - 1:
                nxt = jnp.where(nxt == n_chunks, 0, nxt)
            scores(nxt, (u + 1) % 2)
            update(c0 + u, u % 2)
        return carry

    lax.fori_loop(0, n_chunks // unroll, body, 0)
    acc = acc_ref[...]
    o_ref[...] = (acc[:dv] / acc[dv:dv + 1]).T


def _flash(q, k, vt):
    H, S, dq = q.shape
    hk, n_chunks, dv_ext, tk = vt.shape
    dv = dv_ext - (DV_EXT - HEAD_DIM)
    group = H // hk
    tq = min(FLASH_TQ, S)
    unroll = FLASH_UNROLL if n_chunks % FLASH_UNROLL == 0 else 2
    assert n_chunks % unroll == 0
    return pl.pallas_call(
        functools.partial(_flash_kernel, n_chunks=n_chunks, tk=tk, unroll=unroll),
        grid=(H, S // tq),
        in_specs=[
            pl.BlockSpec((1, tq, dq), lambda h, i: (h, i, 0)),
            pl.BlockSpec((1, S, dq), lambda h, i: (h // group, 0, 0)),
            pl.BlockSpec((1, n_chunks, dv_ext, tk), lambda h, i: (h // group, 0, 0, 0)),
        ],
        out_specs=pl.BlockSpec((tq, dv), lambda h, i: (i, h)),
        out_shape=jax.ShapeDtypeStruct((S, H * dv), F32),
        scratch_shapes=[pltpu.VMEM((1, tq), F32), pltpu.VMEM((dv_ext, tq), F32),
                        pltpu.VMEM((2, tk, tq), F32), pltpu.VMEM((2, 1, tq), F32)],
        compiler_params=_cparams(2),
        name="flash",
    )(q, k, vt)


NEG_MIN = float(np.finfo(np.float32).min)
RPB_ROWS = 2 * NA_KH - 1
RPB_COLS = 2 * NA_KW - 1


def _na_rel(kind, i, j):
    if kind == 0:
        return j - i + NA_KH - 1, j < NA_KH
    if kind == 1:
        return j - i + NA_KH // 2 - 1, i <= j <= i + NA_KH - 1
    return j - i - 1, j >= NA_KROWS - NA_KH


def _na_bias_kernel(rpb_ref, o_ref):
    h = pl.program_id(0)
    shape = (GRID_W, LANES)
    c = lax.broadcasted_iota(I32, shape, 0)
    kc = lax.broadcasted_iota(I32, shape, 1) & (GRID_W - 1)
    cs = jnp.clip(c - NA_KW // 2, 0, GRID_W - NA_KW)
    in_win = (kc >= cs) & (kc < cs + NA_KW)
    dc = kc - c + NA_KW - 1
    lane_lo = lax.broadcasted_iota(I32, shape, 1) < GRID_W
    masked = jnp.full(shape, NEG_MIN, F32)
    cache = {}

    def toeplitz(dr):
        if dr not in cache:
            t = jnp.zeros(shape, F32)
            base = (h * RPB_ROWS + dr) * RPB_COLS
            for d in range(RPB_COLS):
                t = jnp.where(dc == d, rpb_ref[base + d], t)
            cache[dr] = jnp.where(in_win, t, NEG_MIN)
        return cache[dr]

    def half(kind, i, j):
        dr, valid = _na_rel(kind, i, j)
        return toeplitz(dr) if valid else masked

    for kind in range(3):
        for i in range(NA_QROWS):
            for jp in range(NA_KROWS // 2):
                piece = jnp.where(lane_lo, half(kind, i, 2 * jp), half(kind, i, 2 * jp + 1))
                o_ref[kind, 0, i * GRID_W:(i + 1) * GRID_W, jp * LANES:(jp + 1) * LANES] = piece


def _na_bias(rpb):
    return pl.pallas_call(
        _na_bias_kernel,
        grid=(NA_HEADS,),
        in_specs=[pl.BlockSpec(memory_space=pltpu.SMEM)],
        out_specs=pl.BlockSpec((3, 1, NA_TQ, NA_TK), lambda h: (0, h, 0, 0)),
        out_shape=jax.ShapeDtypeStruct((3, NA_HEADS, NA_TQ, NA_TK), F32),
        compiler_params=_cparams(1),
        name="na_bias",
    )(rpb.reshape(-1))


NA_SUB = 8


def _na_kernel(q_ref, k_ref, v_ref, b_ref, o_ref, *, grid_rows, sub):
    i = pl.program_id(1)
    nblk = grid_rows // NA_QROWS

    def scores(j):
        blk = i * sub + j
        kstart = jnp.clip(blk * NA_QROWS - NA_KH // 2, 0, grid_rows - NA_KROWS)
        off = pl.multiple_of(kstart * GRID_W, GRID_W)
        kind = jnp.where(blk == 0, 0, jnp.where(blk == nblk - 1, 2, 1))
        kw = k_ref[0, pl.ds(off, NA_TK), :]
        q = q_ref[0, j * NA_TQ:(j + 1) * NA_TQ, :]
        s = lax.dot_general(q, kw, (((1,), (1,)), ((), ())), preferred_element_type=F32)
        return s + b_ref[kind, 0], off

    def finish(j, s, off):
        vw = v_ref[0, pl.ds(off, NA_TK), :]
        p = jnp.exp(s - jnp.max(s, axis=-1, keepdims=True))
        l = jnp.sum(p, axis=-1, keepdims=True)
        o_ref[j * NA_TQ:(j + 1) * NA_TQ, :] = jnp.dot(p.astype(BF16), vw, preferred_element_type=F32) / l

    cur = scores(0)
    for j in range(sub):
        nxt = scores(j + 1) if j + 1 < sub else None
        finish(j, *cur)
        cur = nxt


def _na(q, k, v, bias):
    H, S, d = q.shape
    grid_rows = S // GRID_W
    nblk = grid_rows // NA_QROWS
    sub = int(np.gcd(nblk, NA_SUB))
    return pl.pallas_call(
        functools.partial(_na_kernel, grid_rows=grid_rows, sub=sub),
        grid=(H, nblk // sub),
        in_specs=[
            pl.BlockSpec((1, NA_TQ * sub, d), lambda h, i: (h, i, 0)),
            pl.BlockSpec((1, S, d), lambda h, i: (h, 0, 0)),
            pl.BlockSpec((1, S, d), lambda h, i: (h, 0, 0)),
            pl.BlockSpec((3, 1, NA_TQ, NA_TK), lambda h, i: (0, h, 0, 0)),
        ],
        out_specs=pl.BlockSpec((NA_TQ * sub, d), lambda h, i: (i, h)),
        out_shape=jax.ShapeDtypeStruct((S, H * d), F32),
        compiler_params=_cparams(2),
        name="na",
    )(q, k, v, bias)


W_MLA = MLA_HEADS * MLA_V
W_GQA = GQA_HEADS * HEAD_DIM
W_NA = NA_HEADS * HEAD_DIM


def _outproj_kernel(oa_ref, ob_ref, oc_ref, g_ref, w_ref, x_ref, g1_ref, o_ref):
    def gnorm(o, g):
        return (o * lax.rsqrt(jnp.mean(o * o, axis=-1, keepdims=True) + EPS) * g).astype(BF16)

    a = gnorm(oa_ref[...], g_ref[:, :W_MLA])
    b = gnorm(ob_ref[...], g_ref[:, W_MLA:W_MLA + W_GQA])
    c = gnorm(oc_ref[...], g_ref[:, W_MLA + W_GQA:])
    y = jnp.dot(a, w_ref[:W_MLA, :], preferred_element_type=F32)
    y = y + jnp.dot(b, w_ref[W_MLA:W_MLA + W_GQA, :], preferred_element_type=F32)
    y = y + jnp.dot(c, w_ref[W_MLA + W_GQA:, :], preferred_element_type=F32)
    o_ref[...] = x_ref[...] + g1_ref[...] * y


def _outproj(oa, ob, oc, gains, w_out, x, mod):
    n = x.shape[0]
    tm = min(256, n)
    return pl.pallas_call(
        _outproj_kernel,
        grid=(n // tm,),
        in_specs=[
            pl.BlockSpec((tm, W_MLA), lambda i: (i, 0)),
            pl.BlockSpec((tm, W_GQA), lambda i: (i, 0)),
            pl.BlockSpec((tm, W_NA), lambda i: (i, 0)),
            pl.BlockSpec((1, D_MODEL), lambda i: (0, 0)),
            pl.BlockSpec((D_MODEL, D_MODEL), lambda i: (0, 0)),
            pl.BlockSpec((tm, D_MODEL), lambda i: (i, 0)),
            pl.BlockSpec((1, D_MODEL), lambda i: (0, 2)),
        ],
        out_specs=pl.BlockSpec((tm, D_MODEL), lambda i: (i, 0)),
        out_shape=jax.ShapeDtypeStruct((n, D_MODEL), F32),
        compiler_params=_cparams(1),
        name="outproj",
    )(oa, ob, oc, gains, w_out, x, mod)


ROUTER_TM = 512


def _store_pitched(ref, val):
    rows = val.shape[0]
    for s in range(ROW_SLABS):
        ref[pl.ds(s, rows, stride=ROW_PITCH), :] = val[:, s * LANES:(s + 1) * LANES]
    for s in range(ROW_SLABS, ROW_PITCH):
        ref[pl.ds(s, rows, stride=ROW_PITCH), :] = jnp.zeros((rows, LANES), val.dtype)


def _load_pitched_slab(ref, s, rows):
    return ref[pl.ds(s, rows, stride=ROW_PITCH), :]


def _first_argmax(v, vmax, n):
    idx = lax.broadcasted_iota(I32, v.shape, 0)
    return jnp.min(jnp.where(v == vmax, idx, n), axis=0, keepdims=True)


def _router_kernel(x_ref, g_ref, sh_ref, sc_ref, w_ref, b_ref, u_ref,
                   h_ref, ri_ref, rw_ref, cnt_ref, carry_ref):
    @pl.when(pl.program_id(0) == 0)
    def _():
        carry_ref[...] = jnp.zeros(carry_ref.shape, F32)

    h = _modulated_norm(x_ref[...], g_ref[...], sc_ref[...], sh_ref[...])
    _store_pitched(h_ref, h)
    logits = lax.dot_general(w_ref[...], h, (((1,), (1,)), ((), ())),
                             precision=lax.Precision.HIGHEST,
                             preferred_element_type=F32) + b_ref[...]
    tm = logits.shape[1]

    gl = logits[0:N_GROUPS]
    gmax = jnp.max(gl, axis=0, keepdims=True)
    gexp = jnp.exp(gl - gmax)
    gprob = gexp / jnp.sum(gexp, axis=0, keepdims=True)
    p_g = jnp.max(gprob, axis=0, keepdims=True)
    g_top = _first_argmax(gprob, p_g, N_GROUPS)

    e_in = jnp.zeros((EXPERTS_PER_GROUP, tm), F32)
    for g in range(N_GROUPS):
        lo = ROUTER_E0 + g * EXPERTS_PER_GROUP
        e_in = jnp.where(g_top == g, logits[lo:lo + EXPERTS_PER_GROUP], e_in)
    eexp = jnp.exp(e_in - jnp.max(e_in, axis=0, keepdims=True))
    eprob = eexp / jnp.sum(eexp, axis=0, keepdims=True)
    v1 = jnp.max(eprob, axis=0, keepdims=True)
    i1 = _first_argmax(eprob, v1, EXPERTS_PER_GROUP)
    row = lax.broadcasted_iota(I32, eprob.shape, 0)
    rest = jnp.where(row == i1, -1.0, eprob)
    v2 = jnp.max(rest, axis=0, keepdims=True)
    i2 = _first_argmax(rest, v2, EXPERTS_PER_GROUP)
    wsum = v1 + v2
    w1 = v1 / wsum * p_g
    w2 = v2 / wsum * p_g
    e1 = g_top * EXPERTS_PER_GROUP + i1
    e2 = g_top * EXPERTS_PER_GROUP + i2

    erow = lax.broadcasted_iota(I32, (N_EXPERTS, tm), 0)
    oh1 = (erow == e1).astype(F32)
    oh2 = (erow == e2).astype(F32)
    oh = oh1 + oh2
    before = jnp.dot(oh.astype(BF16), u_ref[...], preferred_element_type=F32) + carry_ref[...]
    r1 = jnp.sum(oh1 * before, axis=0, keepdims=True)
    r2 = jnp.sum(oh2 * before, axis=0, keepdims=True)
    carry_ref[...] = carry_ref[...] + jnp.sum(oh, axis=1, keepdims=True)
    cnt_ref[...] = jnp.broadcast_to(carry_ref[...], cnt_ref.shape)

    ri_ref[...] = jnp.zeros(ri_ref.shape, I32)
    ri_ref[0:1, :] = e1
    ri_ref[1:2, :] = e2
    ri_ref[2:3, :] = r1.astype(I32)
    ri_ref[3:4, :] = r2.astype(I32)
    rw_ref[...] = jnp.zeros(rw_ref.shape, F32)
    rw_ref[0:1, :] = w1
    rw_ref[1:2, :] = w2


def _router(x, norm_g, mod, w_r, b_r, upper):
    n = x.shape[0]
    tm = upper.shape[0]
    return pl.pallas_call(
        _router_kernel,
        grid=(n // tm,),
        in_specs=[
            pl.BlockSpec((tm, D_MODEL), lambda i: (i, 0)),
            pl.BlockSpec((1, D_MODEL), lambda i: (0, 0)),
            pl.BlockSpec((1, D_MODEL), lambda i: (0, 3)),
            pl.BlockSpec((1, D_MODEL), lambda i: (0, 4)),
            pl.BlockSpec((ROUTER_ROWS, D_MODEL), lambda i: (0, 0)),
            pl.BlockSpec((ROUTER_ROWS, 1), lambda i: (0, 0)),
            pl.BlockSpec((tm, tm), lambda i: (0, 0)),
        ],
        out_specs=(
            pl.BlockSpec((tm * ROW_PITCH, LANES), lambda i: (i, 0)),
            pl.BlockSpec((SUBLANES, tm), lambda i: (0, i)),
            pl.BlockSpec((SUBLANES, tm), lambda i: (0, i)),
            pl.BlockSpec((N_EXPERTS, LANES), lambda i: (0, 0)),
        ),
        out_shape=(
            jax.ShapeDtypeStruct((n * ROW_PITCH, LANES), F32),
            jax.ShapeDtypeStruct((SUBLANES, n), I32),
            jax.ShapeDtypeStruct((SUBLANES, n), F32),
            jax.ShapeDtypeStruct((N_EXPERTS, LANES), F32),
        ),
        scratch_shapes=[pltpu.VMEM((N_EXPERTS, 1), F32)],
        compiler_params=_cparams(1),
        name="router",
    )(x, norm_g, mod, mod, w_r, b_r, upper)


GATHER_UNROLL = 8


def _row_gather_start(src_hbm, idx_ref, base, dst, sem, n_rows):
    def body(r, carry):
        off = idx_ref[base + r]
        pltpu.make_async_copy(src_hbm.at[pl.ds(off, ROW_SLABS)],
                              dst.at[pl.ds(r * ROW_PITCH, ROW_SLABS)], sem).start()
        return carry

    lax.fori_loop(0, n_rows, body, 0, unroll=GATHER_UNROLL)


def _row_gather_wait(src_hbm, dst, sem, n_rows):
    n = n_rows * ROW_SLABS
    pltpu.make_async_copy(src_hbm.at[pl.ds(0, n)], dst.at[pl.ds(0, n)], sem).wait()


BLOCK_SLAB_ROWS = MOE_ROWS * ROW_PITCH


def _expert_kernel(src_ref, bstart_ref, h_hbm, wg_ref, wu_ref, wd_ref, y_hbm,
                   xg_ref, yo_ref, wgb_ref, wub_ref, wdb_ref, gsem, osem):
    e = pl.program_id(0)
    lo, hi = bstart_ref[e], bstart_ref[e + 1]
    total = bstart_ref[N_EXPERTS]

    def out_copy(g, slot):
        row0 = pl.multiple_of(g * BLOCK_SLAB_ROWS, BLOCK_SLAB_ROWS)
        return pltpu.make_async_copy(yo_ref.at[slot], y_hbm.at[pl.ds(row0, BLOCK_SLAB_ROWS)], osem.at[slot])

    @pl.when((e == 0) & (total > 0))
    def _():
        _row_gather_start(h_hbm, src_ref, 0, xg_ref.at[0], gsem.at[0], MOE_ROWS)

    @pl.when(hi > lo)
    def _():
        wgb_ref[...] = wg_ref[0].astype(BF16)
        wub_ref[...] = wu_ref[0].astype(BF16)
        wdb_ref[...] = wd_ref[0].astype(BF16)

    def block(g, carry):
        slot = g % 2

        @pl.when(g + 1 < total)
        def _():
            _row_gather_start(h_hbm, src_ref, (g + 1) * MOE_ROWS, xg_ref.at[1 - slot], gsem.at[1 - slot],
                              MOE_ROWS)

        _row_gather_wait(h_hbm, xg_ref.at[slot], gsem.at[slot], MOE_ROWS)
        xs = xg_ref.at[slot]
        x = jnp.concatenate(
            [_load_pitched_slab(xs, s, MOE_ROWS).astype(BF16) for s in range(ROW_SLABS)], axis=1)
        gate = jnp.dot(x, wgb_ref[...], preferred_element_type=F32)
        up = jnp.dot(x, wub_ref[...], preferred_element_type=F32)
        a = (gate * (1.0 / (1.0 + jnp.exp(-gate))) * up).astype(BF16)
        y = jnp.dot(a, wdb_ref[...], preferred_element_type=F32)

        @pl.when(g >= 2)
        def _():
            out_copy(g - 2, slot).wait()

        _store_pitched(yo_ref.at[slot], y)
        out_copy(g, slot).start()
        return carry

    lax.fori_loop(lo, hi, block, 0)

    @pl.when(e == N_EXPERTS - 1)
    def _():
        @pl.when(total >= 2)
        def _():
            out_copy(total - 2, total % 2).wait()

        @pl.when(total >= 1)
        def _():
            out_copy(total - 1, (total - 1) % 2).wait()

        n_blocks = y_hbm.shape[0] // BLOCK_SLAB_ROWS
        yo_ref[0] = jnp.zeros(yo_ref.shape[1:], F32)

        def zero_start(g, carry):
            out_copy(g, 0).start()
            return carry

        def zero_wait(g, carry):
            out_copy(g, 0).wait()
            return carry

        lax.fori_loop(total, n_blocks, zero_start, 0)
        lax.fori_loop(total, n_blocks, zero_wait, 0)


def _experts(src_off, bstart, h, w_gate, w_up, w_down, layer, n_blocks):
    base = layer * N_EXPERTS
    grid_spec = pltpu.PrefetchScalarGridSpec(
        num_scalar_prefetch=2,
        grid=(N_EXPERTS,),
        in_specs=[
            pl.BlockSpec(memory_space=pl.ANY),
            pl.BlockSpec((1, D_MODEL, D_EXPERT), lambda e, src, bs: (base + e, 0, 0)),
            pl.BlockSpec((1, D_MODEL, D_EXPERT), lambda e, src, bs: (base + e, 0, 0)),
            pl.BlockSpec((1, D_EXPERT, D_MODEL), lambda e, src, bs: (base + e, 0, 0)),
        ],
        out_specs=pl.BlockSpec(memory_space=pl.ANY),
        scratch_shapes=[
            pltpu.VMEM((2, BLOCK_SLAB_ROWS, LANES), F32),
            pltpu.VMEM((2, BLOCK_SLAB_ROWS, LANES), F32),
            pltpu.VMEM((D_MODEL, D_EXPERT), BF16),
            pltpu.VMEM((D_MODEL, D_EXPERT), BF16),
            pltpu.VMEM((D_EXPERT, D_MODEL), BF16),
            pltpu.SemaphoreType.DMA((2,)),
            pltpu.SemaphoreType.DMA((2,)),
        ],
    )
    return pl.pallas_call(
        _expert_kernel,
        grid_spec=grid_spec,
        out_shape=jax.ShapeDtypeStruct((n_blocks * BLOCK_SLAB_ROWS, LANES), F32),
        compiler_params=_cparams(1),
        name="experts",
    )(src_off, bstart, h, w_gate, w_up, w_down)


COMB_TM = 256


def _combine_kernel(d1_ref, d2_ref, y_hbm, x_ref, w1_ref, w2_ref, g2_ref, o_ref, ya_ref, yb_ref, sem):
    i = pl.program_id(0)
    n = pl.num_programs(0)
    slot = i % 2

    def start(blk, s):
        _row_gather_start(y_hbm, d1_ref, blk * COMB_TM, ya_ref.at[s], sem.at[0, s], COMB_TM)
        _row_gather_start(y_hbm, d2_ref, blk * COMB_TM, yb_ref.at[s], sem.at[1, s], COMB_TM)

    @pl.when(i == 0)
    def _():
        start(0, 0)

    @pl.when(i + 1 < n)
    def _():
        start(i + 1, 1 - slot)

    _row_gather_wait(y_hbm, ya_ref.at[slot], sem.at[0, slot], COMB_TM)
    _row_gather_wait(y_hbm, yb_ref.at[slot], sem.at[1, slot], COMB_TM)
    w1, w2 = w1_ref[...], w2_ref[...]
    ya, yb = ya_ref.at[slot], yb_ref.at[slot]
    for s in range(ROW_SLABS):
        cols = slice(s * LANES, (s + 1) * LANES)
        y = w1 * _load_pitched_slab(ya, s, COMB_TM) + w2 * _load_pitched_slab(yb, s, COMB_TM)
        o_ref[:, cols] = x_ref[:, cols] + g2_ref[:, cols] * y


def _combine(dest1, dest2, ybuf, x, w1, w2, mod):
    n = x.shape[0]
    grid_spec = pltpu.PrefetchScalarGridSpec(
        num_scalar_prefetch=2,
        grid=(n // COMB_TM,),
        in_specs=[
            pl.BlockSpec(memory_space=pl.ANY),
            pl.BlockSpec((COMB_TM, D_MODEL), lambda i, d1, d2: (i, 0)),
            pl.BlockSpec((COMB_TM, 1), lambda i, d1, d2: (i, 0)),
            pl.BlockSpec((COMB_TM, 1), lambda i, d1, d2: (i, 0)),
            pl.BlockSpec((1, D_MODEL), lambda i, d1, d2: (0, 5)),
        ],
        out_specs=pl.BlockSpec((COMB_TM, D_MODEL), lambda i, d1, d2: (i, 0)),
        scratch_shapes=[
            pltpu.VMEM((2, COMB_TM * ROW_PITCH, LANES), F32),
            pltpu.VMEM((2, COMB_TM * ROW_PITCH, LANES), F32),
            pltpu.SemaphoreType.DMA((2, 2)),
        ],
    )
    return pl.pallas_call(
        _combine_kernel,
        grid_spec=grid_spec,
        out_shape=jax.ShapeDtypeStruct((n, D_MODEL), F32),
        compiler_params=_cparams(1),
        name="combine",
    )(dest1, dest2, ybuf, x, w1, w2, mod)


def _rope_angles(pos, d):
    inv = ROPE_THETA ** (-jnp.arange(0, d, 2, dtype=F32) / d)
    return pos.astype(F32)[:, None] * inv[None, :]


def _rope_tables(S):
    t = jnp.arange(S)
    z = jnp.zeros((S, MLA_ROPE), F32)
    a = _rope_angles(t, MLA_ROPE)
    cm = jnp.concatenate([jnp.cos(a), jnp.cos(a), z], axis=1)
    sm = jnp.concatenate([-jnp.sin(a), jnp.sin(a), z], axis=1)
    ar = _rope_angles(t // GRID_W, HEAD_DIM // 2)
    ac = _rope_angles(t % GRID_W, HEAD_DIM // 2)
    ca = jnp.concatenate([jnp.cos(ar), jnp.cos(ar), jnp.cos(ac), jnp.cos(ac)], axis=1)
    sa = jnp.concatenate([-jnp.sin(ar), jnp.sin(ar), -jnp.sin(ac), jnp.sin(ac)], axis=1)
    return cm, sm, ca, sa


def _pad_lanes(a, width):
    return jnp.concatenate([a, jnp.zeros(a.shape[:-1] + (width - a.shape[-1],), a.dtype)], axis=-1)


def _layer_weights(w_in, w_uq, w_ukv, qk_q_g, qk_k_g):
    split = C_APE + MLA_ROPE
    w_in_p = jnp.concatenate(
        [w_in[:, :split], jnp.zeros((D_MODEL, PE_PAD - MLA_ROPE), F32), w_in[:, split:]], axis=1
    ).astype(BF16)
    wq = w_uq.reshape(MLA_Q_RANK, MLA_HEADS, MLA_QK)
    wq_n = wq[:, :, :MLA_NOPE].reshape(MLA_Q_RANK, MLA_HEADS * LANES)
    wq_r = _pad_lanes(wq[:, :, MLA_NOPE:], LANES).reshape(MLA_Q_RANK, MLA_HEADS * LANES)
    wq_p = jnp.concatenate([wq_n, wq_r], axis=1).astype(BF16)
    wkv = w_ukv.reshape(MLA_KV_RANK, MLA_HEADS, MLA_NOPE + MLA_V)
    wkv_p = jnp.concatenate(
        [wkv[:, :, :MLA_NOPE].reshape(MLA_KV_RANK, -1), wkv[:, :, MLA_NOPE:].reshape(MLA_KV_RANK, -1)],
        axis=1).astype(BF16)
    gmq = _pad_lanes(qk_q_g[None, :], MLA_QK_PAD)
    gmk = _pad_lanes(qk_k_g[None, :], MLA_QK_PAD)
    return w_in_p, wq_p, wkv_p, gmq, gmk


def _routing_plan(ri, cnt, n_blocks):
    n = ri.shape[1]
    e1, e2, r1, r2 = ri[0], ri[1], ri[2], ri[3]
    counts = cnt[:, 0].astype(I32)
    padded = ((counts + MOE_ROWS - 1) // MOE_ROWS) * MOE_ROWS
    pad_end = jnp.cumsum(padded)
    pad_start = pad_end - padded
    dest1 = pad_start[e1] + r1
    dest2 = pad_start[e2] + r2
    tok_off = jnp.arange(n, dtype=I32) * ROW_PITCH
    src_off = jnp.zeros((n_blocks * MOE_ROWS,), I32).at[jnp.concatenate([dest1, dest2])].set(
        jnp.concatenate([tok_off, tok_off]), unique_indices=True)
    bstart = jnp.concatenate([jnp.zeros((1,), I32), pad_end // MOE_ROWS]).astype(I32)
    return dest1 * ROW_PITCH, dest2 * ROW_PITCH, src_off, bstart


def kernel(x, c, ada_w, ada_b, norm1_g, norm2_g, w_in, mla_q_norm_g, mla_w_uq, mla_kv_norm_g, mla_w_ukv, mla_qk_q_g, mla_qk_k_g, gqa_q_g, gqa_k_g, na_q_g, na_k_g, na_rpb, mix_out_norm_g, w_out, router_group_w, router_group_b, router_expert_w, router_expert_b, expert_w_gate, expert_w_up, expert_w_down):
    B, S, D = x.shape
    assert B == 1 and D == D_MODEL and S % (GRID_W * NA_QROWS) == 0 and S // GRID_W >= NA_KROWS
    depth = ada_w.shape[0]
    n = B * S
    xt = x.reshape(n, D)

    mod_all = _adaln(c, ada_w, ada_b)
    cm, sm, ca, sa = _rope_tables(S)
    tm_r = min(ROUTER_TM, n)
    upper = (jnp.arange(tm_r)[:, None] < jnp.arange(tm_r)[None, :]).astype(BF16)
    n_blocks = -(-(n * 2 + N_EXPERTS * (MOE_ROWS - 1)) // MOE_ROWS)
    wg_all = expert_w_gate.reshape(depth * N_EXPERTS, D_MODEL, D_EXPERT)
    wu_all = expert_w_up.reshape(depth * N_EXPERTS, D_MODEL, D_EXPERT)
    wd_all = expert_w_down.reshape(depth * N_EXPERTS, D_EXPERT, D_MODEL)

    def row(v):
        return v[None, :]

    for l in range(depth):
        mod = mod_all[l]
        w_in_p, wq_p, wkv_p, gmq, gmk = _layer_weights(
            w_in[l], mla_w_uq[l], mla_w_ukv[l], mla_qk_q_g[l], mla_qk_k_g[l])

        proj = _inproj(xt, row(norm1_g[l]), mod, w_in_p)
        qm, km, vmt, qg, kg, vgt, qn, kn, vn = _prep(
            proj, wq_p, wkv_p, row(mla_q_norm_g[l]), row(mla_kv_norm_g[l]), gmq, gmk,
            row(gqa_q_g[l]), row(gqa_k_g[l]), row(na_q_g[l]), row(na_k_g[l]), cm, sm, ca, sa)
        o_a = _flash(qm, km, vmt)
        o_b = _flash(qg, kg, vgt)
        o_c = _na(qn, kn, vn, _na_bias(na_rpb[l]))
        xt = _outproj(o_a, o_b, o_c, row(mix_out_norm_g[l]), w_out[l].astype(BF16), xt, mod)

        w_r = jnp.concatenate(
            [router_group_w[l].T, jnp.zeros((ROUTER_E0 - N_GROUPS, D), F32), router_expert_w[l].T], axis=0)
        b_r = jnp.concatenate(
            [router_group_b[l], jnp.zeros((ROUTER_E0 - N_GROUPS,), F32), router_expert_b[l]])[:, None]
        h2, ri, rw, cnt = _router(xt, row(norm2_g[l]), mod, w_r, b_r, upper)
        off1, off2, src_off, bstart = _routing_plan(ri, cnt, n_blocks)
        ybuf = _experts(src_off, bstart, h2, wg_all, wu_all, wd_all, l, n_blocks)
        xt = _combine(off1, off2, ybuf, xt, rw[0][:, None], rw[1][:, None], mod)

    return xt.reshape(B, S, D)
```

```python
import functools

import jax
import jax.numpy as jnp
import numpy as np
from jax import lax
from jax.experimental import pallas as pl
from jax.experimental.pallas import tpu as pltpu

F32 = jnp.float32
BF16 = jnp.bfloat16
I32 = jnp.int32

D_MODEL = 2048
GRID_W = 64
HEAD_DIM = 128
ROPE_THETA = 10000.0
EPS = 1e-6
MLA_HEADS = 6
MLA_Q_RANK = 384
MLA_KV_RANK = 256
MLA_NOPE = 128
MLA_ROPE = 64
MLA_V = 128
MLA_QK = MLA_NOPE + MLA_ROPE
GQA_HEADS = 4
GQA_KV_HEADS = 2
NA_HEADS = 6
NA_KH = 8
NA_KW = 16
N_GROUPS = 4
EXPERTS_PER_GROUP = 8
N_EXPERTS = N_GROUPS * EXPERTS_PER_GROUP
D_EXPERT = 512

LANES = 128
SUBLANES = 8
VMEM_LIMIT_BYTES = 56 * 1024 * 1024

PE_PAD = LANES
C_AQ = 0
C_AKV = C_AQ + MLA_Q_RANK
C_APE = C_AKV + MLA_KV_RANK
C_GQ = C_APE + PE_PAD
C_GK = C_GQ + GQA_HEADS * HEAD_DIM
C_GV = C_GK + GQA_KV_HEADS * HEAD_DIM
C_NQ = C_GV + GQA_KV_HEADS * HEAD_DIM
C_NK = C_NQ + NA_HEADS * HEAD_DIM
C_NV = C_NK + NA_HEADS * HEAD_DIM
D_IN_PAD = C_NV + NA_HEADS * HEAD_DIM
MLA_QK_PAD = 2 * LANES

NA_QROWS = 4
NA_KROWS = 12
NA_TQ = NA_QROWS * GRID_W
NA_TK = NA_KROWS * GRID_W

LOG2E = float(np.log2(np.e))
DV_EXT = HEAD_DIM + 16

MOE_ROWS = 256
ROW_SLABS = D_MODEL // LANES
ROW_PITCH = 20
ROUTER_ROWS = 40
ROUTER_E0 = 8


def _cparams(n_axes):
    return pltpu.CompilerParams(
        dimension_semantics=("arbitrary",) * n_axes,
        vmem_limit_bytes=VMEM_LIMIT_BYTES,
    )


ADA_TN = 1024
ADA_RC = 256


def _adaln_kernel(c_ref, w_ref, b_ref, o_ref):
    def body(i, acc):
        r0 = pl.multiple_of(i * ADA_RC, ADA_RC)
        c = c_ref[pl.ds(r0, ADA_RC), :]
        cond = c * (1.0 / (1.0 + jnp.exp(-c)))
        w = w_ref[0, pl.ds(r0, ADA_RC), :]
        return acc + jnp.sum(w * cond, axis=0, keepdims=True)

    acc = lax.fori_loop(0, D_MODEL // ADA_RC, body, jnp.zeros((1, ADA_TN), F32))
    o_ref[0] = acc + b_ref[0]


def _adaln(c, ada_w, ada_b):
    L = ada_w.shape[0]
    n_out = ada_w.shape[2]
    return pl.pallas_call(
        _adaln_kernel,
        grid=(L, n_out // ADA_TN),
        in_specs=[
            pl.BlockSpec((D_MODEL, 1), lambda l, j: (0, 0)),
            pl.BlockSpec((1, D_MODEL, ADA_TN), lambda l, j: (l, 0, j)),
            pl.BlockSpec((1, 1, ADA_TN), lambda l, j: (l, 0, j)),
        ],
        out_specs=pl.BlockSpec((1, 1, ADA_TN), lambda l, j: (l, 0, j)),
        out_shape=jax.ShapeDtypeStruct((L, 1, n_out), F32),
        compiler_params=_cparams(2),
        name="adaln",
    )(c.reshape(D_MODEL, 1), ada_w, ada_b.reshape(L, 1, n_out))


def _modulated_norm(x, g, sc, sh):
    ms = jnp.mean(x * x, axis=-1, keepdims=True)
    y = x * lax.rsqrt(ms + EPS) * g
    return y * (1.0 + sc) + sh


INPROJ_TM = 256
INPROJ_TN = 1024


def _inproj_kernel(x_ref, g_ref, sh_ref, sc_ref, w_ref, o_ref):
    h = _modulated_norm(x_ref[...], g_ref[...], sc_ref[...], sh_ref[...]).astype(BF16)
    for j in range(D_IN_PAD // INPROJ_TN):
        cols = slice(j * INPROJ_TN, (j + 1) * INPROJ_TN)
        o_ref[:, cols] = jnp.dot(h, w_ref[:, cols], preferred_element_type=F32)


def _inproj(x, norm_g, mod, w_in_p):
    n = x.shape[0]
    tm = min(INPROJ_TM, n)
    return pl.pallas_call(
        _inproj_kernel,
        grid=(n // tm,),
        in_specs=[
            pl.BlockSpec((tm, D_MODEL), lambda i: (i, 0)),
            pl.BlockSpec((1, D_MODEL), lambda i: (0, 0)),
            pl.BlockSpec((1, D_MODEL), lambda i: (0, 0)),
            pl.BlockSpec((1, D_MODEL), lambda i: (0, 1)),
            pl.BlockSpec((D_MODEL, D_IN_PAD), lambda i: (0, 0), pipeline_mode=pl.Buffered(1)),
        ],
        out_specs=pl.BlockSpec((tm, D_IN_PAD), lambda i: (i, 0)),
        out_shape=jax.ShapeDtypeStruct((n, D_IN_PAD), F32),
        compiler_params=_cparams(1),
        name="inproj",
    )(x, norm_g, mod, mod, w_in_p)


PREP_TM = 512


def _rot_swap(x):
    lane = lax.broadcasted_iota(I32, x.shape, 1)
    fwd = pltpu.roll(x, LANES - 32, 1)
    bwd = pltpu.roll(x, 32, 1)
    return jnp.where((lane & 63) < 32, fwd, bwd)


def _rope_tile(x, cos, sin_signed):
    return x * cos + _rot_swap(x) * sin_signed


def _sumsq(x):
    return jnp.sum(x * x, axis=-1, keepdims=True)


def _prep_kernel(p_ref, wq_ref, wkv_ref, gql_ref, gkvl_ref, gmq_ref, gmk_ref, ggq_ref, ggk_ref,
                 gnq_ref, gnk_ref, cm_ref, sm_ref, ca_ref, sa_ref,
                 qm_ref, km_ref, vmt_ref, qg_ref, kg_ref, vgt_ref, qn_ref, kn_ref, vn_ref):
    mla_scale = MLA_QK ** -0.5 * LOG2E
    gqa_scale = HEAD_DIM ** -0.5 * LOG2E
    hd_scale = HEAD_DIM ** -0.5
    tm = p_ref.shape[0]
    ones_rows = (lax.broadcasted_iota(I32, (DV_EXT - HEAD_DIM, tm), 0) == 0).astype(BF16)
    cm, sm = cm_ref[...], sm_ref[...]
    ca, sa = ca_ref[...], sa_ref[...]

    a_q = p_ref[:, C_AQ:C_AQ + MLA_Q_RANK]
    a_q = a_q * lax.rsqrt(_sumsq(a_q) / MLA_Q_RANK + EPS) * gql_ref[...]
    q = jnp.dot(a_q.astype(BF16), wq_ref[...], preferred_element_type=F32)
    gq_n, gq_r = gmq_ref[:, :LANES], gmq_ref[:, LANES:]
    for h in range(MLA_HEADS):
        nope = q[:, h * LANES:(h + 1) * LANES]
        ropp = q[:, (MLA_HEADS + h) * LANES:(MLA_HEADS + h + 1) * LANES]
        rinv = lax.rsqrt((_sumsq(nope) + _sumsq(ropp)) / MLA_QK + EPS)
        qm_ref[h, :, :LANES] = (nope * rinv * gq_n * mla_scale).astype(BF16)
        qm_ref[h, :, LANES:] = (_rope_tile(ropp * rinv * gq_r, cm, sm) * mla_scale).astype(BF16)

    a_kv = p_ref[:, C_AKV:C_AKV + MLA_KV_RANK]
    a_kv = a_kv * lax.rsqrt(_sumsq(a_kv) / MLA_KV_RANK + EPS) * gkvl_ref[...]
    kv = jnp.dot(a_kv.astype(BF16), wkv_ref[...], preferred_element_type=F32)
    pe = p_ref[:, C_APE:C_APE + PE_PAD]
    ss_pe = _sumsq(pe)
    gk_n, gk_r = gmk_ref[:, :LANES], gmk_ref[:, LANES:]
    pe_rot = _rope_tile(pe * gk_r, cm, sm)
    for h in range(MLA_HEADS):
        nope = kv[:, h * LANES:(h + 1) * LANES]
        v = kv[:, (MLA_HEADS + h) * LANES:(MLA_HEADS + h + 1) * LANES]
        rinv = lax.rsqrt((_sumsq(nope) + ss_pe) / MLA_QK + EPS)
        km_ref[h, :, :LANES] = (nope * rinv * gk_n).astype(BF16)
        km_ref[h, :, LANES:] = (pe_rot * rinv).astype(BF16)
        vmt_ref[h, 0, :HEAD_DIM] = v.T.astype(BF16)
        vmt_ref[h, 0, HEAD_DIM:] = ones_rows

    def head_norm(col, g):
        t = p_ref[:, col:col + HEAD_DIM]
        return t * lax.rsqrt(_sumsq(t) / HEAD_DIM + EPS) * g

    for h in range(GQA_HEADS):
        t = head_norm(C_GQ + h * HEAD_DIM, ggq_ref[...])
        qg_ref[h] = (_rope_tile(t, ca, sa) * gqa_scale).astype(BF16)
    for h in range(GQA_KV_HEADS):
        t = head_norm(C_GK + h * HEAD_DIM, ggk_ref[...])
        kg_ref[h] = _rope_tile(t, ca, sa).astype(BF16)
        vgt_ref[h, 0, :HEAD_DIM] = p_ref[:, C_GV + h * HEAD_DIM:C_GV + (h + 1) * HEAD_DIM].T.astype(BF16)
        vgt_ref[h, 0, HEAD_DIM:] = ones_rows

    for h in range(NA_HEADS):
        qn_ref[h] = (head_norm(C_NQ + h * HEAD_DIM, gnq_ref[...]) * hd_scale).astype(BF16)
        kn_ref[h] = head_norm(C_NK + h * HEAD_DIM, gnk_ref[...]).astype(BF16)
        vn_ref[h] = p_ref[:, C_NV + h * HEAD_DIM:C_NV + (h + 1) * HEAD_DIM].astype(BF16)


def _prep(proj, wq, wkv, gql, gkvl, gmq, gmk, ggq, ggk, gnq, gnk, cm, sm, ca, sa):
    n = proj.shape[0]
    tm = min(PREP_TM, n)
    nb = n // tm

    def full(a):
        return pl.BlockSpec(a.shape, lambda i: (0,) * a.ndim)

    def rows(width):
        return pl.BlockSpec((tm, width), lambda i: (i, 0))

    def heads(h, width):
        return pl.BlockSpec((h, tm, width), lambda i: (0, i, 0))

    def heads_t(h):
        return pl.BlockSpec((h, 1, DV_EXT, tm), lambda i: (0, i, 0, 0))

    out_shape = (
        jax.ShapeDtypeStruct((MLA_HEADS, n, MLA_QK_PAD), BF16),
        jax.ShapeDtypeStruct((MLA_HEADS, n, MLA_QK_PAD), BF16),
        jax.ShapeDtypeStruct((MLA_HEADS, nb, DV_EXT, tm), BF16),
        jax.ShapeDtypeStruct((GQA_HEADS, n, HEAD_DIM), BF16),
        jax.ShapeDtypeStruct((GQA_KV_HEADS, n, HEAD_DIM), BF16),
        jax.ShapeDtypeStruct((GQA_KV_HEADS, nb, DV_EXT, tm), BF16),
        jax.ShapeDtypeStruct((NA_HEADS, n, HEAD_DIM), BF16),
        jax.ShapeDtypeStruct((NA_HEADS, n, HEAD_DIM), BF16),
        jax.ShapeDtypeStruct((NA_HEADS, n, HEAD_DIM), BF16),
    )
    out_specs = (
        heads(MLA_HEADS, MLA_QK_PAD), heads(MLA_HEADS, MLA_QK_PAD), heads_t(MLA_HEADS),
        heads(GQA_HEADS, HEAD_DIM), heads(GQA_KV_HEADS, HEAD_DIM), heads_t(GQA_KV_HEADS),
        heads(NA_HEADS, HEAD_DIM), heads(NA_HEADS, HEAD_DIM), heads(NA_HEADS, HEAD_DIM),
    )
    small = (wq, wkv, gql, gkvl, gmq, gmk, ggq, ggk, gnq, gnk)
    return pl.pallas_call(
        _prep_kernel,
        grid=(nb,),
        in_specs=[rows(D_IN_PAD)] + [full(a) for a in small] + [rows(LANES)] * 4,
        out_specs=out_specs,
        out_shape=out_shape,
        compiler_params=_cparams(1),
        name="prep",
    )(proj, *small, cm, sm, ca, sa)


FLASH_TQ = 1024
FLASH_UNROLL = 8


def _flash_kernel(q_ref, k_ref, vt_ref, o_ref, m_ref, acc_ref, s_ref, cm_ref, *, n_chunks, tk, unroll):
    dv = acc_ref.shape[0] - (DV_EXT - HEAD_DIM)
    m_ref[...] = jnp.full(m_ref.shape, -jnp.inf, F32)
    acc_ref[...] = jnp.zeros(acc_ref.shape, F32)

    def scores(c, slot):
        kc = k_ref[0, pl.ds(pl.multiple_of(c * tk, tk), tk), :]
        s = lax.dot_general(kc, q_ref[0], (((1,), (1,)), ((), ())), preferred_element_type=F32)
        s_ref[slot] = s
        cm_ref[slot] = jnp.max(s, axis=0, keepdims=True)

    def update(c, slot):
        m_prev = m_ref[...]
        m_new = jnp.maximum(m_prev, cm_ref[slot])
        p = jnp.exp2(s_ref[slot] - m_new).astype(BF16)
        alpha = jnp.exp2(m_prev - m_new)
        acc_ref[...] = alpha * acc_ref[...] + jnp.dot(vt_ref[0, c], p, preferred_element_type=F32)
        m_ref[...] = m_new

    scores(0, 0)

    def body(j, carry):
        c0 = unroll * j
        for u in range(unroll):
            nxt = c0 + u + 1
            if u == unroll - 1:
                nxt = jnp.where(nxt == n_chunks, 0, nxt)
            scores(nxt, (u + 1) % 2)
            update(c0 + u, u % 2)
        return carry

    lax.fori_loop(0, n_chunks // unroll, body, 0)
    acc = acc_ref[...]
    o_ref[...] = (acc[:dv] / acc[dv:dv + 1]).T


def _flash(q, k, vt):
    H, S, dq = q.shape
    hk, n_chunks, dv_ext, tk = vt.shape
    dv = dv_ext - (DV_EXT - HEAD_DIM)
    group = H // hk
    tq = min(FLASH_TQ, S)
    unroll = FLASH_UNROLL if n_chunks % FLASH_UNROLL == 0 else 2
    assert n_chunks % unroll == 0
    return pl.pallas_call(
        functools.partial(_flash_kernel, n_chunks=n_chunks, tk=tk, unroll=unroll),
        grid=(H, S // tq),
        in_specs=[
            pl.BlockSpec((1, tq, dq), lambda h, i: (h, i, 0)),
            pl.BlockSpec((1, S, dq), lambda h, i: (h // group, 0, 0)),
            pl.BlockSpec((1, n_chunks, dv_ext, tk), lambda h, i: (h // group, 0, 0, 0)),
        ],
        out_specs=pl.BlockSpec((tq, dv), lambda h, i: (i, h)),
        out_shape=jax.ShapeDtypeStruct((S, H * dv), F32),
        scratch_shapes=[pltpu.VMEM((1, tq), F32), pltpu.VMEM((dv_ext, tq), F32),
                        pltpu.VMEM((2, tk, tq), F32), pltpu.VMEM((2, 1, tq), F32)],
        compiler_params=_cparams(2),
        name="flash",
    )(q, k, vt)


NEG_MIN = float(np.finfo(np.float32).min)
RPB_ROWS = 2 * NA_KH - 1
RPB_COLS = 2 * NA_KW - 1


def _na_rel(kind, i, j):
    if kind == 0:
        return j - i + NA_KH - 1, j < NA_KH
    if kind == 1:
        return j - i + NA_KH // 2 - 1, i <= j <= i + NA_KH - 1
    return j - i - 1, j >= NA_KROWS - NA_KH


def _na_bias_kernel(rpb_ref, o_ref):
    h = pl.program_id(0)
    shape = (GRID_W, LANES)
    c = lax.broadcasted_iota(I32, shape, 0)
    kc = lax.broadcasted_iota(I32, shape, 1) & (GRID_W - 1)
    cs = jnp.clip(c - NA_KW // 2, 0, GRID_W - NA_KW)
    in_win = (kc >= cs) & (kc < cs + NA_KW)
    dc = kc - c + NA_KW - 1
    lane_lo = lax.broadcasted_iota(I32, shape, 1) < GRID_W
    masked = jnp.full(shape, NEG_MIN, F32)
    cache = {}

    def toeplitz(dr):
        if dr not in cache:
            t = jnp.zeros(shape, F32)
            base = (h * RPB_ROWS + dr) * RPB_COLS
            for d in range(RPB_COLS):
                t = jnp.where(dc == d, rpb_ref[base + d], t)
            cache[dr] = jnp.where(in_win, t, NEG_MIN)
        return cache[dr]

    def half(kind, i, j):
        dr, valid = _na_rel(kind, i, j)
        return toeplitz(dr) if valid else masked

    for kind in range(3):
        for i in range(NA_QROWS):
            for jp in range(NA_KROWS // 2):
                piece = jnp.where(lane_lo, half(kind, i, 2 * jp), half(kind, i, 2 * jp + 1))
                o_ref[kind, 0, i * GRID_W:(i + 1) * GRID_W, jp * LANES:(jp + 1) * LANES] = piece


def _na_bias(rpb):
    return pl.pallas_call(
        _na_bias_kernel,
        grid=(NA_HEADS,),
        in_specs=[pl.BlockSpec(memory_space=pltpu.SMEM)],
        out_specs=pl.BlockSpec((3, 1, NA_TQ, NA_TK), lambda h: (0, h, 0, 0)),
        out_shape=jax.ShapeDtypeStruct((3, NA_HEADS, NA_TQ, NA_TK), F32),
        compiler_params=_cparams(1),
        name="na_bias",
    )(rpb.reshape(-1))


NA_SUB = 8


def _na_kernel(q_ref, k_ref, v_ref, b_ref, o_ref, *, grid_rows, sub):
    i = pl.program_id(1)
    nblk = grid_rows // NA_QROWS

    def scores(j):
        blk = i * sub + j
        kstart = jnp.clip(blk * NA_QROWS - NA_KH // 2, 0, grid_rows - NA_KROWS)
        off = pl.multiple_of(kstart * GRID_W, GRID_W)
        kind = jnp.where(blk == 0, 0, jnp.where(blk == nblk - 1, 2, 1))
        kw = k_ref[0, pl.ds(off, NA_TK), :]
        q = q_ref[0, j * NA_TQ:(j + 1) * NA_TQ, :]
        s = lax.dot_general(q, kw, (((1,), (1,)), ((), ())), preferred_element_type=F32)
        return s + b_ref[kind, 0], off

    def finish(j, s, off):
        vw = v_ref[0, pl.ds(off, NA_TK), :]
        p = jnp.exp(s - jnp.max(s, axis=-1, keepdims=True))
        l = jnp.sum(p, axis=-1, keepdims=True)
        o_ref[j * NA_TQ:(j + 1) * NA_TQ, :] = jnp.dot(p.astype(BF16), vw, preferred_element_type=F32) / l

    cur = scores(0)
    for j in range(sub):
        nxt = scores(j + 1) if j + 1 < sub else None
        finish(j, *cur)
        cur = nxt


def _na(q, k, v, bias):
    H, S, d = q.shape
    grid_rows = S // GRID_W
    nblk = grid_rows // NA_QROWS
    sub = int(np.gcd(nblk, NA_SUB))
    return pl.pallas_call(
        functools.partial(_na_kernel, grid_rows=grid_rows, sub=sub),
        grid=(H, nblk // sub),
        in_specs=[
            pl.BlockSpec((1, NA_TQ * sub, d), lambda h, i: (h, i, 0)),
            pl.BlockSpec((1, S, d), lambda h, i: (h, 0, 0)),
            pl.BlockSpec((1, S, d), lambda h, i: (h, 0, 0)),
            pl.BlockSpec((3, 1, NA_TQ, NA_TK), lambda h, i: (0, h, 0, 0)),
        ],
        out_specs=pl.BlockSpec((NA_TQ * sub, d), lambda h, i: (i, h)),
        out_shape=jax.ShapeDtypeStruct((S, H * d), F32),
        compiler_params=_cparams(2),
        name="na",
    )(q, k, v, bias)


W_MLA = MLA_HEADS * MLA_V
W_GQA = GQA_HEADS * HEAD_DIM
W_NA = NA_HEADS * HEAD_DIM


def _outproj_kernel(oa_ref, ob_ref, oc_ref, g_ref, w_ref, x_ref, g1_ref, o_ref):
    def gnorm(o, g):
        return (o * lax.rsqrt(jnp.mean(o * o, axis=-1, keepdims=True) + EPS) * g).astype(BF16)

    a = gnorm(oa_ref[...], g_ref[:, :W_MLA])
    b = gnorm(ob_ref[...], g_ref[:, W_MLA:W_MLA + W_GQA])
    c = gnorm(oc_ref[...], g_ref[:, W_MLA + W_GQA:])
    y = jnp.dot(a, w_ref[:W_MLA, :], preferred_element_type=F32)
    y = y + jnp.dot(b, w_ref[W_MLA:W_MLA + W_GQA, :], preferred_element_type=F32)
    y = y + jnp.dot(c, w_ref[W_MLA + W_GQA:, :], preferred_element_type=F32)
    o_ref[...] = x_ref[...] + g1_ref[...] * y


def _outproj(oa, ob, oc, gains, w_out, x, mod):
    n = x.shape[0]
    tm = min(256, n)
    return pl.pallas_call(
        _outproj_kernel,
        grid=(n // tm,),
        in_specs=[
            pl.BlockSpec((tm, W_MLA), lambda i: (i, 0)),
            pl.BlockSpec((tm, W_GQA), lambda i: (i, 0)),
            pl.BlockSpec((tm, W_NA), lambda i: (i, 0)),
            pl.BlockSpec((1, D_MODEL), lambda i: (0, 0)),
            pl.BlockSpec((D_MODEL, D_MODEL), lambda i: (0, 0)),
            pl.BlockSpec((tm, D_MODEL), lambda i: (i, 0)),
            pl.BlockSpec((1, D_MODEL), lambda i: (0, 2)),
        ],
        out_specs=pl.BlockSpec((tm, D_MODEL), lambda i: (i, 0)),
        out_shape=jax.ShapeDtypeStruct((n, D_MODEL), F32),
        compiler_params=_cparams(1),
        name="outproj",
    )(oa, ob, oc, gains, w_out, x, mod)


ROUTER_TM = 512


def _store_pitched(ref, val):
    rows = val.shape[0]
    for s in range(ROW_SLABS):
        ref[pl.ds(s, rows, stride=ROW_PITCH), :] = val[:, s * LANES:(s + 1) * LANES]
    for s in range(ROW_SLABS, ROW_PITCH):
        ref[pl.ds(s, rows, stride=ROW_PITCH), :] = jnp.zeros((rows, LANES), val.dtype)


def _load_pitched_slab(ref, s, rows):
    return ref[pl.ds(s, rows, stride=ROW_PITCH), :]


def _first_argmax(v, vmax, n):
    idx = lax.broadcasted_iota(I32, v.shape, 0)
    return jnp.min(jnp.where(v == vmax, idx, n), axis=0, keepdims=True)


def _router_kernel(x_ref, g_ref, sh_ref, sc_ref, w_ref, b_ref, u_ref,
                   h_ref, ri_ref, rw_ref, cnt_ref, carry_ref):
    @pl.when(pl.program_id(0) == 0)
    def _():
        carry_ref[...] = jnp.zeros(carry_ref.shape, F32)

    h = _modulated_norm(x_ref[...], g_ref[...], sc_ref[...], sh_ref[...])
    _store_pitched(h_ref, h)
    logits = lax.dot_general(w_ref[...], h, (((1,), (1,)), ((), ())),
                             precision=lax.Precision.HIGHEST,
                             preferred_element_type=F32) + b_ref[...]
    tm = logits.shape[1]

    gl = logits[0:N_GROUPS]
    gmax = jnp.max(gl, axis=0, keepdims=True)
    gexp = jnp.exp(gl - gmax)
    gprob = gexp / jnp.sum(gexp, axis=0, keepdims=True)
    p_g = jnp.max(gprob, axis=0, keepdims=True)
    g_top = _first_argmax(gprob, p_g, N_GROUPS)

    e_in = jnp.zeros((EXPERTS_PER_GROUP, tm), F32)
    for g in range(N_GROUPS):
        lo = ROUTER_E0 + g * EXPERTS_PER_GROUP
        e_in = jnp.where(g_top == g, logits[lo:lo + EXPERTS_PER_GROUP], e_in)
    eexp = jnp.exp(e_in - jnp.max(e_in, axis=0, keepdims=True))
    eprob = eexp / jnp.sum(eexp, axis=0, keepdims=True)
    v1 = jnp.max(eprob, axis=0, keepdims=True)
    i1 = _first_argmax(eprob, v1, EXPERTS_PER_GROUP)
    row = lax.broadcasted_iota(I32, eprob.shape, 0)
    rest = jnp.where(row == i1, -1.0, eprob)
    v2 = jnp.max(rest, axis=0, keepdims=True)
    i2 = _first_argmax(rest, v2, EXPERTS_PER_GROUP)
    wsum = v1 + v2
    w1 = v1 / wsum * p_g
    w2 = v2 / wsum * p_g
    e1 = g_top * EXPERTS_PER_GROUP + i1
    e2 = g_top * EXPERTS_PER_GROUP + i2

    erow = lax.broadcasted_iota(I32, (N_EXPERTS, tm), 0)
    oh1 = (erow == e1).astype(F32)
    oh2 = (erow == e2).astype(F32)
    oh = oh1 + oh2
    before = jnp.dot(oh.astype(BF16), u_ref[...], preferred_element_type=F32) + carry_ref[...]
    r1 = jnp.sum(oh1 * before, axis=0, keepdims=True)
    r2 = jnp.sum(oh2 * before, axis=0, keepdims=True)
    carry_ref[...] = carry_ref[...] + jnp.sum(oh, axis=1, keepdims=True)
    cnt_ref[...] = jnp.broadcast_to(carry_ref[...], cnt_ref.shape)

    ri_ref[...] = jnp.zeros(ri_ref.shape, I32)
    ri_ref[0:1, :] = e1
    ri_ref[1:2, :] = e2
    ri_ref[2:3, :] = r1.astype(I32)
    ri_ref[3:4, :] = r2.astype(I32)
    rw_ref[...] = jnp.zeros(rw_ref.shape, F32)
    rw_ref[0:1, :] = w1
    rw_ref[1:2, :] = w2


def _router(x, norm_g, mod, w_r, b_r, upper):
    n = x.shape[0]
    tm = upper.shape[0]
    return pl.pallas_call(
        _router_kernel,
        grid=(n // tm,),
        in_specs=[
            pl.BlockSpec((tm, D_MODEL), lambda i: (i, 0)),
            pl.BlockSpec((1, D_MODEL), lambda i: (0, 0)),
            pl.BlockSpec((1, D_MODEL), lambda i: (0, 3)),
            pl.BlockSpec((1, D_MODEL), lambda i: (0, 4)),
            pl.BlockSpec((ROUTER_ROWS, D_MODEL), lambda i: (0, 0)),
            pl.BlockSpec((ROUTER_ROWS, 1), lambda i: (0, 0)),
            pl.BlockSpec((tm, tm), lambda i: (0, 0)),
        ],
        out_specs=(
            pl.BlockSpec((tm * ROW_PITCH, LANES), lambda i: (i, 0)),
            pl.BlockSpec((SUBLANES, tm), lambda i: (0, i)),
            pl.BlockSpec((SUBLANES, tm), lambda i: (0, i)),
            pl.BlockSpec((N_EXPERTS, LANES), lambda i: (0, 0)),
        ),
        out_shape=(
            jax.ShapeDtypeStruct((n * ROW_PITCH, LANES), F32),
            jax.ShapeDtypeStruct((SUBLANES, n), I32),
            jax.ShapeDtypeStruct((SUBLANES, n), F32),
            jax.ShapeDtypeStruct((N_EXPERTS, LANES), F32),
        ),
        scratch_shapes=[pltpu.VMEM((N_EXPERTS, 1), F32)],
        compiler_params=_cparams(1),
        name="router",
    )(x, norm_g, mod, mod, w_r, b_r, upper)


GATHER_UNROLL = 8


def _row_gather_start(src_hbm, idx_ref, base, dst, sem, n_rows, priority=0):
    def body(r, carry):
        off = idx_ref[base + r]
        pltpu.make_async_copy(src_hbm.at[pl.ds(off, ROW_SLABS)],
                              dst.at[pl.ds(r * ROW_PITCH, ROW_SLABS)], sem).start(priority=priority)
        return carry

    lax.fori_loop(0, n_rows, body, 0, unroll=GATHER_UNROLL)


def _row_gather_wait(src_hbm, dst, sem, n_rows):
    n = n_rows * ROW_SLABS
    pltpu.make_async_copy(src_hbm.at[pl.ds(0, n)], dst.at[pl.ds(0, n)], sem).wait()


BLOCK_SLAB_ROWS = MOE_ROWS * ROW_PITCH
GATHER_PRIORITY = 1


def _expert_kernel(src_ref, bstart_ref, h_hbm, wg_ref, wu_ref, wd_ref, y_hbm,
                   xg_ref, yo_ref, wgb_ref, wub_ref, wdb_ref, gsem, osem):
    e = pl.program_id(0)
    lo, hi = bstart_ref[e], bstart_ref[e + 1]
    total = bstart_ref[N_EXPERTS]

    def out_copy(g, slot):
        row0 = pl.multiple_of(g * BLOCK_SLAB_ROWS, BLOCK_SLAB_ROWS)
        return pltpu.make_async_copy(yo_ref.at[slot], y_hbm.at[pl.ds(row0, BLOCK_SLAB_ROWS)], osem.at[slot])

    @pl.when((e == 0) & (total > 0))
    def _():
        _row_gather_start(h_hbm, src_ref, 0, xg_ref.at[0], gsem.at[0], MOE_ROWS, GATHER_PRIORITY)

    @pl.when(hi > lo)
    def _():
        wgb_ref[...] = wg_ref[0].astype(BF16)
        wub_ref[...] = wu_ref[0].astype(BF16)
        wdb_ref[...] = wd_ref[0].astype(BF16)

    def block(g, carry):
        slot = g % 2

        @pl.when(g + 1 < total)
        def _():
            _row_gather_start(h_hbm, src_ref, (g + 1) * MOE_ROWS, xg_ref.at[1 - slot], gsem.at[1 - slot],
                              MOE_ROWS, GATHER_PRIORITY)

        _row_gather_wait(h_hbm, xg_ref.at[slot], gsem.at[slot], MOE_ROWS)
        xs = xg_ref.at[slot]
        x = jnp.concatenate(
            [_load_pitched_slab(xs, s, MOE_ROWS).astype(BF16) for s in range(ROW_SLABS)], axis=1)
        gate = jnp.dot(x, wgb_ref[...], preferred_element_type=F32)
        up = jnp.dot(x, wub_ref[...], preferred_element_type=F32)
        a = (gate * (1.0 / (1.0 + jnp.exp(-gate))) * up).astype(BF16)
        y = jnp.dot(a, wdb_ref[...], preferred_element_type=F32)

        @pl.when(g >= 2)
        def _():
            out_copy(g - 2, slot).wait()

        _store_pitched(yo_ref.at[slot], y)
        out_copy(g, slot).start()
        return carry

    lax.fori_loop(lo, hi, block, 0)

    @pl.when(e == N_EXPERTS - 1)
    def _():
        @pl.when(total >= 2)
        def _():
            out_copy(total - 2, total % 2).wait()

        @pl.when(total >= 1)
        def _():
            out_copy(total - 1, (total - 1) % 2).wait()

        n_blocks = y_hbm.shape[0] // BLOCK_SLAB_ROWS
        yo_ref[0] = jnp.zeros(yo_ref.shape[1:], F32)

        def zero_start(g, carry):
            out_copy(g, 0).start()
            return carry

        def zero_wait(g, carry):
            out_copy(g, 0).wait()
            return carry

        lax.fori_loop(total, n_blocks, zero_start, 0)
        lax.fori_loop(total, n_blocks, zero_wait, 0)


def _experts(src_off, bstart, h, w_gate, w_up, w_down, layer, n_blocks):
    base = layer * N_EXPERTS
    grid_spec = pltpu.PrefetchScalarGridSpec(
        num_scalar_prefetch=2,
        grid=(N_EXPERTS,),
        in_specs=[
            pl.BlockSpec(memory_space=pl.ANY),
            pl.BlockSpec((1, D_MODEL, D_EXPERT), lambda e, src, bs: (base + e, 0, 0)),
            pl.BlockSpec((1, D_MODEL, D_EXPERT), lambda e, src, bs: (base + e, 0, 0)),
            pl.BlockSpec((1, D_EXPERT, D_MODEL), lambda e, src, bs: (base + e, 0, 0)),
        ],
        out_specs=pl.BlockSpec(memory_space=pl.ANY),
        scratch_shapes=[
            pltpu.VMEM((2, BLOCK_SLAB_ROWS, LANES), F32),
            pltpu.VMEM((2, BLOCK_SLAB_ROWS, LANES), F32),
            pltpu.VMEM((D_MODEL, D_EXPERT), BF16),
            pltpu.VMEM((D_MODEL, D_EXPERT), BF16),
            pltpu.VMEM((D_EXPERT, D_MODEL), BF16),
            pltpu.SemaphoreType.DMA((2,)),
            pltpu.SemaphoreType.DMA((2,)),
        ],
    )
    return pl.pallas_call(
        _expert_kernel,
        grid_spec=grid_spec,
        out_shape=jax.ShapeDtypeStruct((n_blocks * BLOCK_SLAB_ROWS, LANES), F32),
        compiler_params=_cparams(1),
        name="experts",
    )(src_off, bstart, h, w_gate, w_up, w_down)


COMB_TM = 256


def _combine_kernel(d1_ref, d2_ref, y_hbm, x_ref, w1_ref, w2_ref, g2_ref, o_ref, ya_ref, yb_ref, sem):
    i = pl.program_id(0)
    n = pl.num_programs(0)
    slot = i % 2

    def start(blk, s):
        _row_gather_start(y_hbm, d1_ref, blk * COMB_TM, ya_ref.at[s], sem.at[0, s], COMB_TM)
        _row_gather_start(y_hbm, d2_ref, blk * COMB_TM, yb_ref.at[s], sem.at[1, s], COMB_TM)

    @pl.when(i == 0)
    def _():
        start(0, 0)

    @pl.when(i + 1 < n)
    def _():
        start(i + 1, 1 - slot)

    _row_gather_wait(y_hbm, ya_ref.at[slot], sem.at[0, slot], COMB_TM)
    _row_gather_wait(y_hbm, yb_ref.at[slot], sem.at[1, slot], COMB_TM)
    w1, w2 = w1_ref[...], w2_ref[...]
    ya, yb = ya_ref.at[slot], yb_ref.at[slot]
    for s in range(ROW_SLABS):
        cols = slice(s * LANES, (s + 1) * LANES)
        y = w1 * _load_pitched_slab(ya, s, COMB_TM) + w2 * _load_pitched_slab(yb, s, COMB_TM)
        o_ref[:, cols] = x_ref[:, cols] + g2_ref[:, cols] * y


def _combine(dest1, dest2, ybuf, x, w1, w2, mod):
    n = x.shape[0]
    grid_spec = pltpu.PrefetchScalarGridSpec(
        num_scalar_prefetch=2,
        grid=(n // COMB_TM,),
        in_specs=[
            pl.BlockSpec(memory_space=pl.ANY),
            pl.BlockSpec((COMB_TM, D_MODEL), lambda i, d1, d2: (i, 0)),
            pl.BlockSpec((COMB_TM, 1), lambda i, d1, d2: (i, 0)),
            pl.BlockSpec((COMB_TM, 1), lambda i, d1, d2: (i, 0)),
            pl.BlockSpec((1, D_MODEL), lambda i, d1, d2: (0, 5)),
        ],
        out_specs=pl.BlockSpec((COMB_TM, D_MODEL), lambda i, d1, d2: (i, 0)),
        scratch_shapes=[
            pltpu.VMEM((2, COMB_TM * ROW_PITCH, LANES), F32),
            pltpu.VMEM((2, COMB_TM * ROW_PITCH, LANES), F32),
            pltpu.SemaphoreType.DMA((2, 2)),
        ],
    )
    return pl.pallas_call(
        _combine_kernel,
        grid_spec=grid_spec,
        out_shape=jax.ShapeDtypeStruct((n, D_MODEL), F32),
        compiler_params=_cparams(1),
        name="combine",
    )(dest1, dest2, ybuf, x, w1, w2, mod)


def _rope_angles(pos, d):
    inv = ROPE_THETA ** (-jnp.arange(0, d, 2, dtype=F32) / d)
    return pos.astype(F32)[:, None] * inv[None, :]


def _rope_tables(S):
    t = jnp.arange(S)
    z = jnp.zeros((S, MLA_ROPE), F32)
    a = _rope_angles(t, MLA_ROPE)
    cm = jnp.concatenate([jnp.cos(a), jnp.cos(a), z], axis=1)
    sm = jnp.concatenate([-jnp.sin(a), jnp.sin(a), z], axis=1)
    ar = _rope_angles(t // GRID_W, HEAD_DIM // 2)
    ac = _rope_angles(t % GRID_W, HEAD_DIM // 2)
    ca = jnp.concatenate([jnp.cos(ar), jnp.cos(ar), jnp.cos(ac), jnp.cos(ac)], axis=1)
    sa = jnp.concatenate([-jnp.sin(ar), jnp.sin(ar), -jnp.sin(ac), jnp.sin(ac)], axis=1)
    return cm, sm, ca, sa


def _pad_lanes(a, width):
    return jnp.concatenate([a, jnp.zeros(a.shape[:-1] + (width - a.shape[-1],), a.dtype)], axis=-1)


def _layer_weights(w_in, w_uq, w_ukv, qk_q_g, qk_k_g):
    split = C_APE + MLA_ROPE
    w_in_p = jnp.concatenate(
        [w_in[:, :split], jnp.zeros((D_MODEL, PE_PAD - MLA_ROPE), F32), w_in[:, split:]], axis=1
    ).astype(BF16)
    wq = w_uq.reshape(MLA_Q_RANK, MLA_HEADS, MLA_QK)
    wq_n = wq[:, :, :MLA_NOPE].reshape(MLA_Q_RANK, MLA_HEADS * LANES)
    wq_r = _pad_lanes(wq[:, :, MLA_NOPE:], LANES).reshape(MLA_Q_RANK, MLA_HEADS * LANES)
    wq_p = jnp.concatenate([wq_n, wq_r], axis=1).astype(BF16)
    wkv = w_ukv.reshape(MLA_KV_RANK, MLA_HEADS, MLA_NOPE + MLA_V)
    wkv_p = jnp.concatenate(
        [wkv[:, :, :MLA_NOPE].reshape(MLA_KV_RANK, -1), wkv[:, :, MLA_NOPE:].reshape(MLA_KV_RANK, -1)],
        axis=1).astype(BF16)
    gmq = _pad_lanes(qk_q_g[None, :], MLA_QK_PAD)
    gmk = _pad_lanes(qk_k_g[None, :], MLA_QK_PAD)
    return w_in_p, wq_p, wkv_p, gmq, gmk


def _routing_plan(ri, cnt, n_blocks):
    n = ri.shape[1]
    e1, e2, r1, r2 = ri[0], ri[1], ri[2], ri[3]
    counts = cnt[:, 0].astype(I32)
    padded = ((counts + MOE_ROWS - 1) // MOE_ROWS) * MOE_ROWS
    pad_end = jnp.cumsum(padded)
    pad_start = pad_end - padded
    dest1 = pad_start[e1] + r1
    dest2 = pad_start[e2] + r2
    tok_off = jnp.arange(n, dtype=I32) * ROW_PITCH
    src_off = jnp.zeros((n_blocks * MOE_ROWS,), I32).at[jnp.concatenate([dest1, dest2])].set(
        jnp.concatenate([tok_off, tok_off]), unique_indices=True)
    bstart = jnp.concatenate([jnp.zeros((1,), I32), pad_end // MOE_ROWS]).astype(I32)
    return dest1 * ROW_PITCH, dest2 * ROW_PITCH, src_off, bstart


def kernel(x, c, ada_w, ada_b, norm1_g, norm2_g, w_in, mla_q_norm_g, mla_w_uq, mla_kv_norm_g, mla_w_ukv, mla_qk_q_g, mla_qk_k_g, gqa_q_g, gqa_k_g, na_q_g, na_k_g, na_rpb, mix_out_norm_g, w_out, router_group_w, router_group_b, router_expert_w, router_expert_b, expert_w_gate, expert_w_up, expert_w_down):
    B, S, D = x.shape
    assert B == 1 and D == D_MODEL and S % (GRID_W * NA_QROWS) == 0 and S // GRID_W >= NA_KROWS
    depth = ada_w.shape[0]
    n = B * S
    xt = x.reshape(n, D)

    mod_all = _adaln(c, ada_w, ada_b)
    cm, sm, ca, sa = _rope_tables(S)
    tm_r = min(ROUTER_TM, n)
    upper = (jnp.arange(tm_r)[:, None] < jnp.arange(tm_r)[None, :]).astype(BF16)
    n_blocks = -(-(n * 2 + N_EXPERTS * (MOE_ROWS - 1)) // MOE_ROWS)
    wg_all = expert_w_gate.reshape(depth * N_EXPERTS, D_MODEL, D_EXPERT)
    wu_all = expert_w_up.reshape(depth * N_EXPERTS, D_MODEL, D_EXPERT)
    wd_all = expert_w_down.reshape(depth * N_EXPERTS, D_EXPERT, D_MODEL)

    def row(v):
        return v[None, :]

    for l in range(depth):
        mod = mod_all[l]
        w_in_p, wq_p, wkv_p, gmq, gmk = _layer_weights(
            w_in[l], mla_w_uq[l], mla_w_ukv[l], mla_qk_q_g[l], mla_qk_k_g[l])

        proj = _inproj(xt, row(norm1_g[l]), mod, w_in_p)
        qm, km, vmt, qg, kg, vgt, qn, kn, vn = _prep(
            proj, wq_p, wkv_p, row(mla_q_norm_g[l]), row(mla_kv_norm_g[l]), gmq, gmk,
            row(gqa_q_g[l]), row(gqa_k_g[l]), row(na_q_g[l]), row(na_k_g[l]), cm, sm, ca, sa)
        o_a = _flash(qm, km, vmt)
        o_b = _flash(qg, kg, vgt)
        o_c = _na(qn, kn, vn, _na_bias(na_rpb[l]))
        xt = _outproj(o_a, o_b, o_c, row(mix_out_norm_g[l]), w_out[l].astype(BF16), xt, mod)

        w_r = jnp.concatenate(
            [router_group_w[l].T, jnp.zeros((ROUTER_E0 - N_GROUPS, D), F32), router_expert_w[l].T], axis=0)
        b_r = jnp.concatenate(
            [router_group_b[l], jnp.zeros((ROUTER_E0 - N_GROUPS,), F32), router_expert_b[l]])[:, None]
        h2, ri, rw, cnt = _router(xt, row(norm2_g[l]), mod, w_r, b_r, upper)
        off1, off2, src_off, bstart = _routing_plan(ri, cnt, n_blocks)
        ybuf = _experts(src_off, bstart, h2, wg_all, wu_all, wd_all, l, n_blocks)
        xt = _combine(off1, off2, ybuf, xt, rw[0][:, None], rw[1][:, None], mod)

    return xt.reshape(B, S, D)
```

```python
import functools

import jax
import jax.numpy as jnp
import numpy as np
from jax import lax
from jax.experimental import pallas as pl
from jax.experimental.pallas import tpu as pltpu

F32 = jnp.float32
BF16 = jnp.bfloat16
I32 = jnp.int32

D_MODEL = 2048
GRID_W = 64
HEAD_DIM = 128
ROPE_THETA = 10000.0
EPS = 1e-6
MLA_HEADS = 6
MLA_Q_RANK = 384
MLA_KV_RANK = 256
MLA_NOPE = 128
MLA_ROPE = 64
MLA_V = 128
MLA_QK = MLA_NOPE + MLA_ROPE
GQA_HEADS = 4
GQA_KV_HEADS = 2
NA_HEADS = 6
NA_KH = 8
NA_KW = 16
N_GROUPS = 4
EXPERTS_PER_GROUP = 8
N_EXPERTS = N_GROUPS * EXPERTS_PER_GROUP
D_EXPERT = 512

LANES = 128
SUBLANES = 8
VMEM_LIMIT_BYTES = 56 * 1024 * 1024

PE_PAD = LANES
C_AQ = 0
C_AKV = C_AQ + MLA_Q_RANK
C_APE = C_AKV + MLA_KV_RANK
C_GQ = C_APE + PE_PAD
C_GK = C_GQ + GQA_HEADS * HEAD_DIM
C_GV = C_GK + GQA_KV_HEADS * HEAD_DIM
C_NQ = C_GV + GQA_KV_HEADS * HEAD_DIM
C_NK = C_NQ + NA_HEADS * HEAD_DIM
C_NV = C_NK + NA_HEADS * HEAD_DIM
D_IN_PAD = C_NV + NA_HEADS * HEAD_DIM
MLA_QK_PAD = 2 * LANES

NA_QROWS = 4
NA_KROWS = 12
NA_TQ = NA_QROWS * GRID_W
NA_TK = NA_KROWS * GRID_W

LOG2E = float(np.log2(np.e))
DV_EXT = HEAD_DIM + 16

MOE_ROWS = 256
ROW_SLABS = D_MODEL // LANES
ROW_PITCH = 20
ROUTER_ROWS = 40
ROUTER_E0 = 8


def _cparams(n_axes):
    return pltpu.CompilerParams(
        dimension_semantics=("arbitrary",) * n_axes,
        vmem_limit_bytes=VMEM_LIMIT_BYTES,
    )


ADA_TN = 1024
ADA_RC = 256


def _adaln_kernel(c_ref, w_ref, b_ref, o_ref):
    def body(i, acc):
        r0 = pl.multiple_of(i * ADA_RC, ADA_RC)
        c = c_ref[pl.ds(r0, ADA_RC), :]
        cond = c * (1.0 / (1.0 + jnp.exp(-c)))
        w = w_ref[0, pl.ds(r0, ADA_RC), :]
        return acc + jnp.sum(w * cond, axis=0, keepdims=True)

    acc = lax.fori_loop(0, D_MODEL // ADA_RC, body, jnp.zeros((1, ADA_TN), F32))
    o_ref[0] = acc + b_ref[0]


def _adaln(c, ada_w, ada_b):
    L = ada_w.shape[0]
    n_out = ada_w.shape[2]
    return pl.pallas_call(
        _adaln_kernel,
        grid=(L, n_out // ADA_TN),
        in_specs=[
            pl.BlockSpec((D_MODEL, 1), lambda l, j: (0, 0)),
            pl.BlockSpec((1, D_MODEL, ADA_TN), lambda l, j: (l, 0, j)),
            pl.BlockSpec((1, 1, ADA_TN), lambda l, j: (l, 0, j)),
        ],
        out_specs=pl.BlockSpec((1, 1, ADA_TN), lambda l, j: (l, 0, j)),
        out_shape=jax.ShapeDtypeStruct((L, 1, n_out), F32),
        compiler_params=_cparams(2),
        name="adaln",
    )(c.reshape(D_MODEL, 1), ada_w, ada_b.reshape(L, 1, n_out))


def _modulated_norm(x, g, sc, sh):
    ms = jnp.mean(x * x, axis=-1, keepdims=True)
    y = x * lax.rsqrt(ms + EPS) * g
    return y * (1.0 + sc) + sh


INPROJ_TM = 256
INPROJ_TN = 1024


def _inproj_kernel(x_ref, g_ref, sh_ref, sc_ref, w_ref, o_ref):
    h = _modulated_norm(x_ref[...], g_ref[...], sc_ref[...], sh_ref[...]).astype(BF16)
    for j in range(D_IN_PAD // INPROJ_TN):
        cols = slice(j * INPROJ_TN, (j + 1) * INPROJ_TN)
        o_ref[:, cols] = jnp.dot(h, w_ref[:, cols], preferred_element_type=F32)


def _inproj(x, norm_g, mod, w_in_p):
    n = x.shape[0]
    tm = min(INPROJ_TM, n)
    return pl.pallas_call(
        _inproj_kernel,
        grid=(n // tm,),
        in_specs=[
            pl.BlockSpec((tm, D_MODEL), lambda i: (i, 0)),
            pl.BlockSpec((1, D_MODEL), lambda i: (0, 0)),
            pl.BlockSpec((1, D_MODEL), lambda i: (0, 0)),
            pl.BlockSpec((1, D_MODEL), lambda i: (0, 1)),
            pl.BlockSpec((D_MODEL, D_IN_PAD), lambda i: (0, 0), pipeline_mode=pl.Buffered(1)),
        ],
        out_specs=pl.BlockSpec((tm, D_IN_PAD), lambda i: (i, 0)),
        out_shape=jax.ShapeDtypeStruct((n, D_IN_PAD), F32),
        compiler_params=_cparams(1),
        name="inproj",
    )(x, norm_g, mod, mod, w_in_p)


PREP_TM = 512


def _rot_swap(x):
    lane = lax.broadcasted_iota(I32, x.shape, 1)
    fwd = pltpu.roll(x, LANES - 32, 1)
    bwd = pltpu.roll(x, 32, 1)
    return jnp.where((lane & 63) < 32, fwd, bwd)


def _rope_tile(x, cos, sin_signed):
    return x * cos + _rot_swap(x) * sin_signed


def _sumsq(x):
    return jnp.sum(x * x, axis=-1, keepdims=True)


def _prep_kernel(p_ref, wq_ref, wkv_ref, gql_ref, gkvl_ref, gmq_ref, gmk_ref, ggq_ref, ggk_ref,
                 gnq_ref, gnk_ref, cm_ref, sm_ref, ca_ref, sa_ref,
                 qm_ref, km_ref, vmt_ref, qg_ref, kg_ref, vgt_ref, qn_ref, kn_ref, vn_ref):
    mla_scale = MLA_QK ** -0.5 * LOG2E
    gqa_scale = HEAD_DIM ** -0.5 * LOG2E
    hd_scale = HEAD_DIM ** -0.5
    tm = p_ref.shape[0]
    ones_rows = (lax.broadcasted_iota(I32, (DV_EXT - HEAD_DIM, tm), 0) == 0).astype(BF16)
    cm, sm = cm_ref[...], sm_ref[...]
    ca, sa = ca_ref[...], sa_ref[...]

    a_q = p_ref[:, C_AQ:C_AQ + MLA_Q_RANK]
    a_q = a_q * lax.rsqrt(_sumsq(a_q) / MLA_Q_RANK + EPS) * gql_ref[...]
    q = jnp.dot(a_q.astype(BF16), wq_ref[...], preferred_element_type=F32)
    gq_n, gq_r = gmq_ref[:, :LANES], gmq_ref[:, LANES:]
    for h in range(MLA_HEADS):
        nope = q[:, h * LANES:(h + 1) * LANES]
        ropp = q[:, (MLA_HEADS + h) * LANES:(MLA_HEADS + h + 1) * LANES]
        rinv = lax.rsqrt((_sumsq(nope) + _sumsq(ropp)) / MLA_QK + EPS)
        qm_ref[h, :, :LANES] = (nope * rinv * gq_n * mla_scale).astype(BF16)
        qm_ref[h, :, LANES:] = (_rope_tile(ropp * rinv * gq_r, cm, sm) * mla_scale).astype(BF16)

    a_kv = p_ref[:, C_AKV:C_AKV + MLA_KV_RANK]
    a_kv = a_kv * lax.rsqrt(_sumsq(a_kv) / MLA_KV_RANK + EPS) * gkvl_ref[...]
    kv = jnp.dot(a_kv.astype(BF16), wkv_ref[...], preferred_element_type=F32)
    pe = p_ref[:, C_APE:C_APE + PE_PAD]
    ss_pe = _sumsq(pe)
    gk_n, gk_r = gmk_ref[:, :LANES], gmk_ref[:, LANES:]
    pe_rot = _rope_tile(pe * gk_r, cm, sm)
    for h in range(MLA_HEADS):
        nope = kv[:, h * LANES:(h + 1) * LANES]
        v = kv[:, (MLA_HEADS + h) * LANES:(MLA_HEADS + h + 1) * LANES]
        rinv = lax.rsqrt((_sumsq(nope) + ss_pe) / MLA_QK + EPS)
        km_ref[h, :, :LANES] = (nope * rinv * gk_n).astype(BF16)
        km_ref[h, :, LANES:] = (pe_rot * rinv).astype(BF16)
        vmt_ref[h, 0, :HEAD_DIM] = v.T.astype(BF16)
        vmt_ref[h, 0, HEAD_DIM:] = ones_rows

    def head_norm(col, g):
        t = p_ref[:, col:col + HEAD_DIM]
        return t * lax.rsqrt(_sumsq(t) / HEAD_DIM + EPS) * g

    for h in range(GQA_HEADS):
        t = head_norm(C_GQ + h * HEAD_DIM, ggq_ref[...])
        qg_ref[h] = (_rope_tile(t, ca, sa) * gqa_scale).astype(BF16)
    for h in range(GQA_KV_HEADS):
        t = head_norm(C_GK + h * HEAD_DIM, ggk_ref[...])
        kg_ref[h] = _rope_tile(t, ca, sa).astype(BF16)
        vgt_ref[h, 0, :HEAD_DIM] = p_ref[:, C_GV + h * HEAD_DIM:C_GV + (h + 1) * HEAD_DIM].T.astype(BF16)
        vgt_ref[h, 0, HEAD_DIM:] = ones_rows

    for h in range(NA_HEADS):
        qn_ref[h] = (head_norm(C_NQ + h * HEAD_DIM, gnq_ref[...]) * hd_scale).astype(BF16)
        kn_ref[h] = head_norm(C_NK + h * HEAD_DIM, gnk_ref[...]).astype(BF16)
        vn_ref[h] = p_ref[:, C_NV + h * HEAD_DIM:C_NV + (h + 1) * HEAD_DIM].astype(BF16)


def _prep(proj, wq, wkv, gql, gkvl, gmq, gmk, ggq, ggk, gnq, gnk, cm, sm, ca, sa):
    n = proj.shape[0]
    tm = min(PREP_TM, n)
    nb = n // tm

    def full(a):
        return pl.BlockSpec(a.shape, lambda i: (0,) * a.ndim)

    def rows(width):
        return pl.BlockSpec((tm, width), lambda i: (i, 0))

    def heads(h, width):
        return pl.BlockSpec((h, tm, width), lambda i: (0, i, 0))

    def heads_t(h):
        return pl.BlockSpec((h, 1, DV_EXT, tm), lambda i: (0, i, 0, 0))

    out_shape = (
        jax.ShapeDtypeStruct((MLA_HEADS, n, MLA_QK_PAD), BF16),
        jax.ShapeDtypeStruct((MLA_HEADS, n, MLA_QK_PAD), BF16),
        jax.ShapeDtypeStruct((MLA_HEADS, nb, DV_EXT, tm), BF16),
        jax.ShapeDtypeStruct((GQA_HEADS, n, HEAD_DIM), BF16),
        jax.ShapeDtypeStruct((GQA_KV_HEADS, n, HEAD_DIM), BF16),
        jax.ShapeDtypeStruct((GQA_KV_HEADS, nb, DV_EXT, tm), BF16),
        jax.ShapeDtypeStruct((NA_HEADS, n, HEAD_DIM), BF16),
        jax.ShapeDtypeStruct((NA_HEADS, n, HEAD_DIM), BF16),
        jax.ShapeDtypeStruct((NA_HEADS, n, HEAD_DIM), BF16),
    )
    out_specs = (
        heads(MLA_HEADS, MLA_QK_PAD), heads(MLA_HEADS, MLA_QK_PAD), heads_t(MLA_HEADS),
        heads(GQA_HEADS, HEAD_DIM), heads(GQA_KV_HEADS, HEAD_DIM), heads_t(GQA_KV_HEADS),
        heads(NA_HEADS, HEAD_DIM), heads(NA_HEADS, HEAD_DIM), heads(NA_HEADS, HEAD_DIM),
    )
    small = (wq, wkv, gql, gkvl, gmq, gmk, ggq, ggk, gnq, gnk)
    return pl.pallas_call(
        _prep_kernel,
        grid=(nb,),
        in_specs=[rows(D_IN_PAD)] + [full(a) for a in small] + [rows(LANES)] * 4,
        out_specs=out_specs,
        out_shape=out_shape,
        compiler_params=_cparams(1),
        name="prep",
    )(proj, *small, cm, sm, ca, sa)


FLASH_TQ = 1024
FLASH_UNROLL = 8


def _flash_kernel(q_ref, k_ref, vt_ref, o_ref, m_ref, acc_ref, s_ref, cm_ref, *, n_chunks, tk, unroll):
    dv = acc_ref.shape[0] - (DV_EXT - HEAD_DIM)
    m_ref[...] = jnp.full(m_ref.shape, -jnp.inf, F32)
    acc_ref[...] = jnp.zeros(acc_ref.shape, F32)

    def scores(c, slot):
        kc = k_ref[0, pl.ds(pl.multiple_of(c * tk, tk), tk), :]
        s = lax.dot_general(kc, q_ref[0], (((1,), (1,)), ((), ())), preferred_element_type=F32)
        s_ref[slot] = s
        cm_ref[slot] = jnp.max(s, axis=0, keepdims=True)

    def update(c, slot):
        m_prev = m_ref[...]
        m_new = jnp.maximum(m_prev, cm_ref[slot])
        p = jnp.exp2(s_ref[slot] - m_new).astype(BF16)
        alpha = jnp.exp2(m_prev - m_new)
        acc_ref[...] = alpha * acc_ref[...] + jnp.dot(vt_ref[0, c], p, preferred_element_type=F32)
        m_ref[...] = m_new

    scores(0, 0)

    def body(j, carry):
        c0 = unroll * j
        for u in range(unroll):
            nxt = c0 + u + 1
            if u == unroll - 1:
                nxt = jnp.where(nxt == n_chunks, 0, nxt)
            scores(nxt, (u + 1) % 2)
            update(c0 + u, u % 2)
        return carry

    lax.fori_loop(0, n_chunks // unroll, body, 0)
    acc = acc_ref[...]
    o_ref[...] = (acc[:dv] / acc[dv:dv + 1]).T


def _flash(q, k, vt):
    H, S, dq = q.shape
    hk, n_chunks, dv_ext, tk = vt.shape
    dv = dv_ext - (DV_EXT - HEAD_DIM)
    group = H // hk
    tq = min(FLASH_TQ, S)
    unroll = FLASH_UNROLL if n_chunks % FLASH_UNROLL == 0 else 2
    assert n_chunks % unroll == 0
    return pl.pallas_call(
        functools.partial(_flash_kernel, n_chunks=n_chunks, tk=tk, unroll=unroll),
        grid=(H, S // tq),
        in_specs=[
            pl.BlockSpec((1, tq, dq), lambda h, i: (h, i, 0)),
            pl.BlockSpec((1, S, dq), lambda h, i: (h // group, 0, 0)),
            pl.BlockSpec((1, n_chunks, dv_ext, tk), lambda h, i: (h // group, 0, 0, 0)),
        ],
        out_specs=pl.BlockSpec((tq, dv), lambda h, i: (i, h)),
        out_shape=jax.ShapeDtypeStruct((S, H * dv), F32),
        scratch_shapes=[pltpu.VMEM((1, tq), F32), pltpu.VMEM((dv_ext, tq), F32),
                        pltpu.VMEM((2, tk, tq), F32), pltpu.VMEM((2, 1, tq), F32)],
        compiler_params=_cparams(2),
        name="flash",
    )(q, k, vt)


NEG_MIN = float(np.finfo(np.float32).min)
RPB_ROWS = 2 * NA_KH - 1
RPB_COLS = 2 * NA_KW - 1


def _na_rel(kind, i, j):
    if kind == 0:
        return j - i + NA_KH - 1, j < NA_KH
    if kind == 1:
        return j - i + NA_KH // 2 - 1, i <= j <= i + NA_KH - 1
    return j - i - 1, j >= NA_KROWS - NA_KH


def _na_bias_kernel(rpb_ref, o_ref):
    h = pl.program_id(0)
    shape = (GRID_W, LANES)
    c = lax.broadcasted_iota(I32, shape, 0)
    kc = lax.broadcasted_iota(I32, shape, 1) & (GRID_W - 1)
    cs = jnp.clip(c - NA_KW // 2, 0, GRID_W - NA_KW)
    in_win = (kc >= cs) & (kc < cs + NA_KW)
    dc = kc - c + NA_KW - 1
    lane_lo = lax.broadcasted_iota(I32, shape, 1) < GRID_W
    masked = jnp.full(shape, NEG_MIN, F32)
    cache = {}

    def toeplitz(dr):
        if dr not in cache:
            t = jnp.zeros(shape, F32)
            base = (h * RPB_ROWS + dr) * RPB_COLS
            for d in range(RPB_COLS):
                t = jnp.where(dc == d, rpb_ref[base + d], t)
            cache[dr] = jnp.where(in_win, t, NEG_MIN)
        return cache[dr]

    def half(kind, i, j):
        dr, valid = _na_rel(kind, i, j)
        return toeplitz(dr) if valid else masked

    for kind in range(3):
        for i in range(NA_QROWS):
            for jp in range(NA_KROWS // 2):
                piece = jnp.where(lane_lo, half(kind, i, 2 * jp), half(kind, i, 2 * jp + 1))
                o_ref[kind, 0, i * GRID_W:(i + 1) * GRID_W, jp * LANES:(jp + 1) * LANES] = piece


def _na_bias(rpb):
    return pl.pallas_call(
        _na_bias_kernel,
        grid=(NA_HEADS,),
        in_specs=[pl.BlockSpec(memory_space=pltpu.SMEM)],
        out_specs=pl.BlockSpec((3, 1, NA_TQ, NA_TK), lambda h: (0, h, 0, 0)),
        out_shape=jax.ShapeDtypeStruct((3, NA_HEADS, NA_TQ, NA_TK), F32),
        compiler_params=_cparams(1),
        name="na_bias",
    )(rpb.reshape(-1))


NA_SUB = 8


def _na_kernel(q_ref, k_ref, v_ref, b_ref, o_ref, *, grid_rows, sub):
    i = pl.program_id(1)
    nblk = grid_rows // NA_QROWS

    def scores(j):
        blk = i * sub + j
        kstart = jnp.clip(blk * NA_QROWS - NA_KH // 2, 0, grid_rows - NA_KROWS)
        off = pl.multiple_of(kstart * GRID_W, GRID_W)
        kind = jnp.where(blk == 0, 0, jnp.where(blk == nblk - 1, 2, 1))
        kw = k_ref[0, pl.ds(off, NA_TK), :]
        q = q_ref[0, j * NA_TQ:(j + 1) * NA_TQ, :]
        s = lax.dot_general(q, kw, (((1,), (1,)), ((), ())), preferred_element_type=F32)
        return s + b_ref[kind, 0], off

    def finish(j, s, off):
        vw = v_ref[0, pl.ds(off, NA_TK), :]
        p = jnp.exp(s - jnp.max(s, axis=-1, keepdims=True))
        l = jnp.sum(p, axis=-1, keepdims=True)
        o_ref[j * NA_TQ:(j + 1) * NA_TQ, :] = jnp.dot(p.astype(BF16), vw, preferred_element_type=F32) / l

    cur = scores(0)
    for j in range(sub):
        nxt = scores(j + 1) if j + 1 < sub else None
        finish(j, *cur)
        cur = nxt


def _na(q, k, v, bias):
    H, S, d = q.shape
    grid_rows = S // GRID_W
    nblk = grid_rows // NA_QROWS
    sub = int(np.gcd(nblk, NA_SUB))
    return pl.pallas_call(
        functools.partial(_na_kernel, grid_rows=grid_rows, sub=sub),
        grid=(H, nblk // sub),
        in_specs=[
            pl.BlockSpec((1, NA_TQ * sub, d), lambda h, i: (h, i, 0)),
            pl.BlockSpec((1, S, d), lambda h, i: (h, 0, 0)),
            pl.BlockSpec((1, S, d), lambda h, i: (h, 0, 0)),
            pl.BlockSpec((3, 1, NA_TQ, NA_TK), lambda h, i: (0, h, 0, 0)),
        ],
        out_specs=pl.BlockSpec((NA_TQ * sub, d), lambda h, i: (i, h)),
        out_shape=jax.ShapeDtypeStruct((S, H * d), F32),
        compiler_params=_cparams(2),
        name="na",
    )(q, k, v, bias)


W_MLA = MLA_HEADS * MLA_V
W_GQA = GQA_HEADS * HEAD_DIM
W_NA = NA_HEADS * HEAD_DIM


def _outproj_kernel(oa_ref, ob_ref, oc_ref, g_ref, w_ref, x_ref, g1_ref, o_ref):
    def gnorm(o, g):
        return (o * lax.rsqrt(jnp.mean(o * o, axis=-1, keepdims=True) + EPS) * g).astype(BF16)

    a = gnorm(oa_ref[...], g_ref[:, :W_MLA])
    b = gnorm(ob_ref[...], g_ref[:, W_MLA:W_MLA + W_GQA])
    c = gnorm(oc_ref[...], g_ref[:, W_MLA + W_GQA:])
    y = jnp.dot(a, w_ref[:W_MLA, :], preferred_element_type=F32)
    y = y + jnp.dot(b, w_ref[W_MLA:W_MLA + W_GQA, :], preferred_element_type=F32)
    y = y + jnp.dot(c, w_ref[W_MLA + W_GQA:, :], preferred_element_type=F32)
    o_ref[...] = x_ref[...] + g1_ref[...] * y


def _outproj(oa, ob, oc, gains, w_out, x, mod):
    n = x.shape[0]
    tm = min(256, n)
    return pl.pallas_call(
        _outproj_kernel,
        grid=(n // tm,),
        in_specs=[
            pl.BlockSpec((tm, W_MLA), lambda i: (i, 0)),
            pl.BlockSpec((tm, W_GQA), lambda i: (i, 0)),
            pl.BlockSpec((tm, W_NA), lambda i: (i, 0)),
            pl.BlockSpec((1, D_MODEL), lambda i: (0, 0)),
            pl.BlockSpec((D_MODEL, D_MODEL), lambda i: (0, 0)),
            pl.BlockSpec((tm, D_MODEL), lambda i: (i, 0)),
            pl.BlockSpec((1, D_MODEL), lambda i: (0, 2)),
        ],
        out_specs=pl.BlockSpec((tm, D_MODEL), lambda i: (i, 0)),
        out_shape=jax.ShapeDtypeStruct((n, D_MODEL), F32),
        compiler_params=_cparams(1),
        name="outproj",
    )(oa, ob, oc, gains, w_out, x, mod)


ROUTER_TM = 512


def _store_pitched(ref, val):
    rows = val.shape[0]
    for s in range(ROW_SLABS):
        ref[pl.ds(s, rows, stride=ROW_PITCH), :] = val[:, s * LANES:(s + 1) * LANES]
    for s in range(ROW_SLABS, ROW_PITCH):
        ref[pl.ds(s, rows, stride=ROW_PITCH), :] = jnp.zeros((rows, LANES), val.dtype)


def _load_pitched_slab(ref, s, rows):
    return ref[pl.ds(s, rows, stride=ROW_PITCH), :]


def _first_argmax(v, vmax, n):
    idx = lax.broadcasted_iota(I32, v.shape, 0)
    return jnp.min(jnp.where(v == vmax, idx, n), axis=0, keepdims=True)


def _router_kernel(x_ref, g_ref, sh_ref, sc_ref, w_ref, b_ref, u_ref,
                   h_ref, ri_ref, rw_ref, cnt_ref, carry_ref):
    @pl.when(pl.program_id(0) == 0)
    def _():
        carry_ref[...] = jnp.zeros(carry_ref.shape, F32)

    h = _modulated_norm(x_ref[...], g_ref[...], sc_ref[...], sh_ref[...])
    _store_pitched(h_ref, h)
    logits = lax.dot_general(w_ref[...], h, (((1,), (1,)), ((), ())),
                             precision=lax.Precision.HIGHEST,
                             preferred_element_type=F32) + b_ref[...]
    tm = logits.shape[1]

    gl = logits[0:N_GROUPS]
    gmax = jnp.max(gl, axis=0, keepdims=True)
    gexp = jnp.exp(gl - gmax)
    gprob = gexp / jnp.sum(gexp, axis=0, keepdims=True)
    p_g = jnp.max(gprob, axis=0, keepdims=True)
    g_top = _first_argmax(gprob, p_g, N_GROUPS)

    e_in = jnp.zeros((EXPERTS_PER_GROUP, tm), F32)
    for g in range(N_GROUPS):
        lo = ROUTER_E0 + g * EXPERTS_PER_GROUP
        e_in = jnp.where(g_top == g, logits[lo:lo + EXPERTS_PER_GROUP], e_in)
    eexp = jnp.exp(e_in - jnp.max(e_in, axis=0, keepdims=True))
    eprob = eexp / jnp.sum(eexp, axis=0, keepdims=True)
    v1 = jnp.max(eprob, axis=0, keepdims=True)
    i1 = _first_argmax(eprob, v1, EXPERTS_PER_GROUP)
    row = lax.broadcasted_iota(I32, eprob.shape, 0)
    rest = jnp.where(row == i1, -1.0, eprob)
    v2 = jnp.max(rest, axis=0, keepdims=True)
    i2 = _first_argmax(rest, v2, EXPERTS_PER_GROUP)
    wsum = v1 + v2
    w1 = v1 / wsum * p_g
    w2 = v2 / wsum * p_g
    e1 = g_top * EXPERTS_PER_GROUP + i1
    e2 = g_top * EXPERTS_PER_GROUP + i2

    erow = lax.broadcasted_iota(I32, (N_EXPERTS, tm), 0)
    oh1 = (erow == e1).astype(F32)
    oh2 = (erow == e2).astype(F32)
    oh = oh1 + oh2
    before = jnp.dot(oh.astype(BF16), u_ref[...], preferred_element_type=F32) + carry_ref[...]
    r1 = jnp.sum(oh1 * before, axis=0, keepdims=True)
    r2 = jnp.sum(oh2 * before, axis=0, keepdims=True)
    carry_ref[...] = carry_ref[...] + jnp.sum(oh, axis=1, keepdims=True)
    cnt_ref[...] = jnp.broadcast_to(carry_ref[...], cnt_ref.shape)

    ri_ref[...] = jnp.zeros(ri_ref.shape, I32)
    ri_ref[0:1, :] = e1
    ri_ref[1:2, :] = e2
    ri_ref[2:3, :] = r1.astype(I32)
    ri_ref[3:4, :] = r2.astype(I32)
    rw_ref[...] = jnp.zeros(rw_ref.shape, F32)
    rw_ref[0:1, :] = w1
    rw_ref[1:2, :] = w2


def _router(x, norm_g, mod, w_r, b_r, upper):
    n = x.shape[0]
    tm = upper.shape[0]
    return pl.pallas_call(
        _router_kernel,
        grid=(n // tm,),
        in_specs=[
            pl.BlockSpec((tm, D_MODEL), lambda i: (i, 0)),
            pl.BlockSpec((1, D_MODEL), lambda i: (0, 0)),
            pl.BlockSpec((1, D_MODEL), lambda i: (0, 3)),
            pl.BlockSpec((1, D_MODEL), lambda i: (0, 4)),
            pl.BlockSpec((ROUTER_ROWS, D_MODEL), lambda i: (0, 0)),
            pl.BlockSpec((ROUTER_ROWS, 1), lambda i: (0, 0)),
            pl.BlockSpec((tm, tm), lambda i: (0, 0)),
        ],
        out_specs=(
            pl.BlockSpec((tm * ROW_PITCH, LANES), lambda i: (i, 0)),
            pl.BlockSpec((SUBLANES, tm), lambda i: (0, i)),
            pl.BlockSpec((SUBLANES, tm), lambda i: (0, i)),
            pl.BlockSpec((N_EXPERTS, LANES), lambda i: (0, 0)),
        ),
        out_shape=(
            jax.ShapeDtypeStruct((n * ROW_PITCH, LANES), F32),
            jax.ShapeDtypeStruct((SUBLANES, n), I32),
            jax.ShapeDtypeStruct((SUBLANES, n), F32),
            jax.ShapeDtypeStruct((N_EXPERTS, LANES), F32),
        ),
        scratch_shapes=[pltpu.VMEM((N_EXPERTS, 1), F32)],
        compiler_params=_cparams(1),
        name="router",
    )(x, norm_g, mod, mod, w_r, b_r, upper)


GATHER_UNROLL = 8


def _row_gather_start(src_hbm, idx_ref, base, dst, sem, n_rows):
    def body(r, carry):
        off = idx_ref[base + r]
        pltpu.make_async_copy(src_hbm.at[pl.ds(off, ROW_SLABS)],
                              dst.at[pl.ds(r * ROW_PITCH, ROW_SLABS)], sem).start()
        return carry

    lax.fori_loop(0, n_rows, body, 0, unroll=GATHER_UNROLL)


def _row_gather_wait(src_hbm, dst, sem, n_rows):
    n = n_rows * ROW_SLABS
    pltpu.make_async_copy(src_hbm.at[pl.ds(0, n)], dst.at[pl.ds(0, n)], sem).wait()


BLOCK_SLAB_ROWS = MOE_ROWS * ROW_PITCH
GATHER_SLOTS = 4
GATHER_DEPTH = GATHER_SLOTS


def _expert_kernel(src_ref, bstart_ref, h_hbm, wg_ref, wu_ref, wd_ref, y_hbm,
                   xg_ref, yo_ref, wgb_ref, wub_ref, wdb_ref, gsem, osem):
    e = pl.program_id(0)
    lo, hi = bstart_ref[e], bstart_ref[e + 1]
    total = bstart_ref[N_EXPERTS]

    def out_copy(g, slot):
        row0 = pl.multiple_of(g * BLOCK_SLAB_ROWS, BLOCK_SLAB_ROWS)
        return pltpu.make_async_copy(yo_ref.at[slot], y_hbm.at[pl.ds(row0, BLOCK_SLAB_ROWS)], osem.at[slot])

    @pl.when(e == 0)
    def _():
        for d in range(GATHER_DEPTH):
            @pl.when(d < total)
            def _():
                _row_gather_start(h_hbm, src_ref, d * MOE_ROWS, xg_ref.at[d], gsem.at[d], MOE_ROWS)

    @pl.when(hi > lo)
    def _():
        wgb_ref[...] = wg_ref[0].astype(BF16)
        wub_ref[...] = wu_ref[0].astype(BF16)
        wdb_ref[...] = wd_ref[0].astype(BF16)

    def block(g, carry):
        slot = g % 2
        gslot = g % GATHER_SLOTS

        _row_gather_wait(h_hbm, xg_ref.at[gslot], gsem.at[gslot], MOE_ROWS)
        xs = xg_ref.at[gslot]
        x = jnp.concatenate(
            [_load_pitched_slab(xs, s, MOE_ROWS).astype(BF16) for s in range(ROW_SLABS)], axis=1)
        gate = jnp.dot(x, wgb_ref[...], preferred_element_type=F32)
        up = jnp.dot(x, wub_ref[...], preferred_element_type=F32)
        a = (gate * (1.0 / (1.0 + jnp.exp(-gate))) * up).astype(BF16)
        y = jnp.dot(a, wdb_ref[...], preferred_element_type=F32)

        @pl.when(g + GATHER_DEPTH < total)
        def _():
            nslot = (g + GATHER_DEPTH) % GATHER_SLOTS
            _row_gather_start(h_hbm, src_ref, (g + GATHER_DEPTH) * MOE_ROWS, xg_ref.at[nslot], gsem.at[nslot],
                              MOE_ROWS)

        @pl.when(g >= 2)
        def _():
            out_copy(g - 2, slot).wait()

        _store_pitched(yo_ref.at[slot], y)
        out_copy(g, slot).start()
        return carry

    lax.fori_loop(lo, hi, block, 0)

    @pl.when(e == N_EXPERTS - 1)
    def _():
        @pl.when(total >= 2)
        def _():
            out_copy(total - 2, total % 2).wait()

        @pl.when(total >= 1)
        def _():
            out_copy(total - 1, (total - 1) % 2).wait()

        n_blocks = y_hbm.shape[0] // BLOCK_SLAB_ROWS
        yo_ref[0] = jnp.zeros(yo_ref.shape[1:], F32)

        def zero_start(g, carry):
            out_copy(g, 0).start()
            return carry

        def zero_wait(g, carry):
            out_copy(g, 0).wait()
            return carry

        lax.fori_loop(total, n_blocks, zero_start, 0)
        lax.fori_loop(total, n_blocks, zero_wait, 0)


def _experts(src_off, bstart, h, w_gate, w_up, w_down, layer, n_blocks):
    base = layer * N_EXPERTS
    grid_spec = pltpu.PrefetchScalarGridSpec(
        num_scalar_prefetch=2,
        grid=(N_EXPERTS,),
        in_specs=[
            pl.BlockSpec(memory_space=pl.ANY),
            pl.BlockSpec((1, D_MODEL, D_EXPERT), lambda e, src, bs: (base + e, 0, 0)),
            pl.BlockSpec((1, D_MODEL, D_EXPERT), lambda e, src, bs: (base + e, 0, 0)),
            pl.BlockSpec((1, D_EXPERT, D_MODEL), lambda e, src, bs: (base + e, 0, 0)),
        ],
        out_specs=pl.BlockSpec(memory_space=pl.ANY),
        scratch_shapes=[
            pltpu.VMEM((GATHER_SLOTS, BLOCK_SLAB_ROWS, LANES), F32),
            pltpu.VMEM((2, BLOCK_SLAB_ROWS, LANES), F32),
            pltpu.VMEM((D_MODEL, D_EXPERT), BF16),
            pltpu.VMEM((D_MODEL, D_EXPERT), BF16),
            pltpu.VMEM((D_EXPERT, D_MODEL), BF16),
            pltpu.SemaphoreType.DMA((GATHER_SLOTS,)),
            pltpu.SemaphoreType.DMA((2,)),
        ],
    )
    return pl.pallas_call(
        _expert_kernel,
        grid_spec=grid_spec,
        out_shape=jax.ShapeDtypeStruct((n_blocks * BLOCK_SLAB_ROWS, LANES), F32),
        compiler_params=_cparams(1),
        name="experts",
    )(src_off, bstart, h, w_gate, w_up, w_down)


COMB_TM = 256


def _combine_kernel(d1_ref, d2_ref, y_hbm, x_ref, w1_ref, w2_ref, g2_ref, o_ref, ya_ref, yb_ref, sem):
    i = pl.program_id(0)
    n = pl.num_programs(0)
    slot = i % 2

    def start(blk, s):
        _row_gather_start(y_hbm, d1_ref, blk * COMB_TM, ya_ref.at[s], sem.at[0, s], COMB_TM)
        _row_gather_start(y_hbm, d2_ref, blk * COMB_TM, yb_ref.at[s], sem.at[1, s], COMB_TM)

    @pl.when(i == 0)
    def _():
        start(0, 0)

    @pl.when(i + 1 < n)
    def _():
        start(i + 1, 1 - slot)

    _row_gather_wait(y_hbm, ya_ref.at[slot], sem.at[0, slot], COMB_TM)
    _row_gather_wait(y_hbm, yb_ref.at[slot], sem.at[1, slot], COMB_TM)
    w1, w2 = w1_ref[...], w2_ref[...]
    ya, yb = ya_ref.at[slot], yb_ref.at[slot]
    for s in range(ROW_SLABS):
        cols = slice(s * LANES, (s + 1) * LANES)
        y = w1 * _load_pitched_slab(ya, s, COMB_TM) + w2 * _load_pitched_slab(yb, s, COMB_TM)
        o_ref[:, cols] = x_ref[:, cols] + g2_ref[:, cols] * y


def _combine(dest1, dest2, ybuf, x, w1, w2, mod):
    n = x.shape[0]
    grid_spec = pltpu.PrefetchScalarGridSpec(
        num_scalar_prefetch=2,
        grid=(n // COMB_TM,),
        in_specs=[
            pl.BlockSpec(memory_space=pl.ANY),
            pl.BlockSpec((COMB_TM, D_MODEL), lambda i, d1, d2: (i, 0)),
            pl.BlockSpec((COMB_TM, 1), lambda i, d1, d2: (i, 0)),
            pl.BlockSpec((COMB_TM, 1), lambda i, d1, d2: (i, 0)),
            pl.BlockSpec((1, D_MODEL), lambda i, d1, d2: (0, 5)),
        ],
        out_specs=pl.BlockSpec((COMB_TM, D_MODEL), lambda i, d1, d2: (i, 0)),
        scratch_shapes=[
            pltpu.VMEM((2, COMB_TM * ROW_PITCH, LANES), F32),
            pltpu.VMEM((2, COMB_TM * ROW_PITCH, LANES), F32),
            pltpu.SemaphoreType.DMA((2, 2)),
        ],
    )
    return pl.pallas_call(
        _combine_kernel,
        grid_spec=grid_spec,
        out_shape=jax.ShapeDtypeStruct((n, D_MODEL), F32),
        compiler_params=_cparams(1),
        name="combine",
    )(dest1, dest2, ybuf, x, w1, w2, mod)


def _rope_angles(pos, d):
    inv = ROPE_THETA ** (-jnp.arange(0, d, 2, dtype=F32) / d)
    return pos.astype(F32)[:, None] * inv[None, :]


def _rope_tables(S):
    t = jnp.arange(S)
    z = jnp.zeros((S, MLA_ROPE), F32)
    a = _rope_angles(t, MLA_ROPE)
    cm = jnp.concatenate([jnp.cos(a), jnp.cos(a), z], axis=1)
    sm = jnp.concatenate([-jnp.sin(a), jnp.sin(a), z], axis=1)
    ar = _rope_angles(t // GRID_W, HEAD_DIM // 2)
    ac = _rope_angles(t % GRID_W, HEAD_DIM // 2)
    ca = jnp.concatenate([jnp.cos(ar), jnp.cos(ar), jnp.cos(ac), jnp.cos(ac)], axis=1)
    sa = jnp.concatenate([-jnp.sin(ar), jnp.sin(ar), -jnp.sin(ac), jnp.sin(ac)], axis=1)
    return cm, sm, ca, sa


def _pad_lanes(a, width):
    return jnp.concatenate([a, jnp.zeros(a.shape[:-1] + (width - a.shape[-1],), a.dtype)], axis=-1)


def _layer_weights(w_in, w_uq, w_ukv, qk_q_g, qk_k_g):
    split = C_APE + MLA_ROPE
    w_in_p = jnp.concatenate(
        [w_in[:, :split], jnp.zeros((D_MODEL, PE_PAD - MLA_ROPE), F32), w_in[:, split:]], axis=1
    ).astype(BF16)
    wq = w_uq.reshape(MLA_Q_RANK, MLA_HEADS, MLA_QK)
    wq_n = wq[:, :, :MLA_NOPE].reshape(MLA_Q_RANK, MLA_HEADS * LANES)
    wq_r = _pad_lanes(wq[:, :, MLA_NOPE:], LANES).reshape(MLA_Q_RANK, MLA_HEADS * LANES)
    wq_p = jnp.concatenate([wq_n, wq_r], axis=1).astype(BF16)
    wkv = w_ukv.reshape(MLA_KV_RANK, MLA_HEADS, MLA_NOPE + MLA_V)
    wkv_p = jnp.concatenate(
        [wkv[:, :, :MLA_NOPE].reshape(MLA_KV_RANK, -1), wkv[:, :, MLA_NOPE:].reshape(MLA_KV_RANK, -1)],
        axis=1).astype(BF16)
    gmq = _pad_lanes(qk_q_g[None, :], MLA_QK_PAD)
    gmk = _pad_lanes(qk_k_g[None, :], MLA_QK_PAD)
    return w_in_p, wq_p, wkv_p, gmq, gmk


def _routing_plan(ri, cnt, n_blocks):
    n = ri.shape[1]
    e1, e2, r1, r2 = ri[0], ri[1], ri[2], ri[3]
    counts = cnt[:, 0].astype(I32)
    padded = ((counts + MOE_ROWS - 1) // MOE_ROWS) * MOE_ROWS
    pad_end = jnp.cumsum(padded)
    pad_start = pad_end - padded
    dest1 = pad_start[e1] + r1
    dest2 = pad_start[e2] + r2
    tok_off = jnp.arange(n, dtype=I32) * ROW_PITCH
    pad_src = (jnp.arange(n_blocks * MOE_ROWS, dtype=I32) % n) * ROW_PITCH
    src_off = pad_src.at[jnp.concatenate([dest1, dest2])].set(
        jnp.concatenate([tok_off, tok_off]), unique_indices=True)
    bstart = jnp.concatenate([jnp.zeros((1,), I32), pad_end // MOE_ROWS]).astype(I32)
    return dest1 * ROW_PITCH, dest2 * ROW_PITCH, src_off, bstart


def kernel(x, c, ada_w, ada_b, norm1_g, norm2_g, w_in, mla_q_norm_g, mla_w_uq, mla_kv_norm_g, mla_w_ukv, mla_qk_q_g, mla_qk_k_g, gqa_q_g, gqa_k_g, na_q_g, na_k_g, na_rpb, mix_out_norm_g, w_out, router_group_w, router_group_b, router_expert_w, router_expert_b, expert_w_gate, expert_w_up, expert_w_down):
    B, S, D = x.shape
    assert B == 1 and D == D_MODEL and S % (GRID_W * NA_QROWS) == 0 and S // GRID_W >= NA_KROWS
    depth = ada_w.shape[0]
    n = B * S
    xt = x.reshape(n, D)

    mod_all = _adaln(c, ada_w, ada_b)
    cm, sm, ca, sa = _rope_tables(S)
    tm_r = min(ROUTER_TM, n)
    upper = (jnp.arange(tm_r)[:, None] < jnp.arange(tm_r)[None, :]).astype(BF16)
    n_blocks = -(-(n * 2 + N_EXPERTS * (MOE_ROWS - 1)) // MOE_ROWS)
    wg_all = expert_w_gate.reshape(depth * N_EXPERTS, D_MODEL, D_EXPERT)
    wu_all = expert_w_up.reshape(depth * N_EXPERTS, D_MODEL, D_EXPERT)
    wd_all = expert_w_down.reshape(depth * N_EXPERTS, D_EXPERT, D_MODEL)

    def row(v):
        return v[None, :]

    for l in range(depth):
        mod = mod_all[l]
        w_in_p, wq_p, wkv_p, gmq, gmk = _layer_weights(
            w_in[l], mla_w_uq[l], mla_w_ukv[l], mla_qk_q_g[l], mla_qk_k_g[l])

        proj = _inproj(xt, row(norm1_g[l]), mod, w_in_p)
        qm, km, vmt, qg, kg, vgt, qn, kn, vn = _prep(
            proj, wq_p, wkv_p, row(mla_q_norm_g[l]), row(mla_kv_norm_g[l]), gmq, gmk,
            row(gqa_q_g[l]), row(gqa_k_g[l]), row(na_q_g[l]), row(na_k_g[l]), cm, sm, ca, sa)
        o_a = _flash(qm, km, vmt)
        o_b = _flash(qg, kg, vgt)
        o_c = _na(qn, kn, vn, _na_bias(na_rpb[l]))
        xt = _outproj(o_a, o_b, o_c, row(mix_out_norm_g[l]), w_out[l].astype(BF16), xt, mod)

        w_r = jnp.concatenate(
            [router_group_w[l].T, jnp.zeros((ROUTER_E0 - N_GROUPS, D), F32), router_expert_w[l].T], axis=0)
        b_r = jnp.concatenate(
            [router_group_b[l], jnp.zeros((ROUTER_E0 - N_GROUPS,), F32), router_expert_b[l]])[:, None]
        h2, ri, rw, cnt = _router(xt, row(norm2_g[l]), mod, w_r, b_r, upper)
        off1, off2, src_off, bstart = _routing_plan(ri, cnt, n_blocks)
        ybuf = _experts(src_off, bstart, h2, wg_all, wu_all, wd_all, l, n_blocks)
        xt = _combine(off1, off2, ybuf, xt, rw[0][:, None], rw[1][:, None], mod)

    return xt.reshape(B, S, D)
```

```python
import functools

import jax
import jax.numpy as jnp
import numpy as np
from jax import lax
from jax.experimental import pallas as pl
from jax.experimental.pallas import tpu as pltpu

F32 = jnp.float32
BF16 = jnp.bfloat16
I32 = jnp.int32

D_MODEL = 2048
GRID_W = 64
HEAD_DIM = 128
ROPE_THETA = 10000.0
EPS = 1e-6
MLA_HEADS = 6
MLA_Q_RANK = 384
MLA_KV_RANK = 256
MLA_NOPE = 128
MLA_ROPE = 64
MLA_V = 128
MLA_QK = MLA_NOPE + MLA_ROPE
GQA_HEADS = 4
GQA_KV_HEADS = 2
NA_HEADS = 6
NA_KH = 8
NA_KW = 16
N_GROUPS = 4
EXPERTS_PER_GROUP = 8
N_EXPERTS = N_GROUPS * EXPERTS_PER_GROUP
D_EXPERT = 512

LANES = 128
SUBLANES = 8
VMEM_LIMIT_BYTES = 56 * 1024 * 1024

PE_PAD = LANES
C_AQ = 0
C_AKV = C_AQ + MLA_Q_RANK
C_APE = C_AKV + MLA_KV_RANK
C_GQ = C_APE + PE_PAD
C_GK = C_GQ + GQA_HEADS * HEAD_DIM
C_GV = C_GK + GQA_KV_HEADS * HEAD_DIM
C_NQ = C_GV + GQA_KV_HEADS * HEAD_DIM
C_NK = C_NQ + NA_HEADS * HEAD_DIM
C_NV = C_NK + NA_HEADS * HEAD_DIM
D_IN_PAD = C_NV + NA_HEADS * HEAD_DIM
MLA_QK_PAD = 2 * LANES

NA_QROWS = 4
NA_KROWS = 12
NA_TQ = NA_QROWS * GRID_W
NA_TK = NA_KROWS * GRID_W

LOG2E = float(np.log2(np.e))
DV_EXT = HEAD_DIM + 16

MOE_ROWS = 256
ROW_SLABS = D_MODEL // LANES
ROW_PITCH = 20
ROUTER_ROWS = 40
ROUTER_E0 = 8


def _cparams(n_axes):
    return pltpu.CompilerParams(
        dimension_semantics=("arbitrary",) * n_axes,
        vmem_limit_bytes=VMEM_LIMIT_BYTES,
    )


ADA_TN = 1024
ADA_RC = 256


def _adaln_kernel(c_ref, w_ref, b_ref, o_ref):
    def body(i, acc):
        r0 = pl.multiple_of(i * ADA_RC, ADA_RC)
        c = c_ref[pl.ds(r0, ADA_RC), :]
        cond = c * (1.0 / (1.0 + jnp.exp(-c)))
        w = w_ref[0, pl.ds(r0, ADA_RC), :]
        return acc + jnp.sum(w * cond, axis=0, keepdims=True)

    acc = lax.fori_loop(0, D_MODEL // ADA_RC, body, jnp.zeros((1, ADA_TN), F32))
    o_ref[0] = acc + b_ref[0]


def _adaln(c, ada_w, ada_b):
    L = ada_w.shape[0]
    n_out = ada_w.shape[2]
    return pl.pallas_call(
        _adaln_kernel,
        grid=(L, n_out // ADA_TN),
        in_specs=[
            pl.BlockSpec((D_MODEL, 1), lambda l, j: (0, 0)),
            pl.BlockSpec((1, D_MODEL, ADA_TN), lambda l, j: (l, 0, j)),
            pl.BlockSpec((1, 1, ADA_TN), lambda l, j: (l, 0, j)),
        ],
        out_specs=pl.BlockSpec((1, 1, ADA_TN), lambda l, j: (l, 0, j)),
        out_shape=jax.ShapeDtypeStruct((L, 1, n_out), F32),
        compiler_params=_cparams(2),
        name="adaln",
    )(c.reshape(D_MODEL, 1), ada_w, ada_b.reshape(L, 1, n_out))


def _modulated_norm(x, g, sc, sh):
    ms = jnp.mean(x * x, axis=-1, keepdims=True)
    y = x * lax.rsqrt(ms + EPS) * g
    return y * (1.0 + sc) + sh


MIX_TM = 256
INPROJ_TN = 1024


def _rot_swap(x):
    lane = lax.broadcasted_iota(I32, x.shape, 1)
    fwd = pltpu.roll(x, LANES - 32, 1)
    bwd = pltpu.roll(x, 32, 1)
    return jnp.where((lane & 63) < 32, fwd, bwd)


def _rope_tile(x, cos, sin_signed):
    return x * cos + _rot_swap(x) * sin_signed


def _sumsq(x):
    return jnp.sum(x * x, axis=-1, keepdims=True)


def _prep_body(p_ref, wq_ref, wkv_ref, gql_ref, gkvl_ref, gmq_ref, gmk_ref, ggq_ref, ggk_ref,
                 gnq_ref, gnk_ref, cm_ref, sm_ref, ca_ref, sa_ref,
                 qm_ref, km_ref, vmt_ref, qg_ref, kg_ref, vgt_ref, qn_ref, kn_ref, vn_ref):
    mla_scale = MLA_QK ** -0.5 * LOG2E
    gqa_scale = HEAD_DIM ** -0.5 * LOG2E
    hd_scale = HEAD_DIM ** -0.5
    tm = p_ref.shape[0]
    ones_rows = (lax.broadcasted_iota(I32, (DV_EXT - HEAD_DIM, tm), 0) == 0).astype(BF16)
    cm, sm = cm_ref[...], sm_ref[...]
    ca, sa = ca_ref[...], sa_ref[...]

    a_q = p_ref[:, C_AQ:C_AQ + MLA_Q_RANK]
    a_q = a_q * lax.rsqrt(_sumsq(a_q) / MLA_Q_RANK + EPS) * gql_ref[...]
    q = jnp.dot(a_q.astype(BF16), wq_ref[...], preferred_element_type=F32)
    gq_n, gq_r = gmq_ref[:, :LANES], gmq_ref[:, LANES:]
    for h in range(MLA_HEADS):
        nope = q[:, h * LANES:(h + 1) * LANES]
        ropp = q[:, (MLA_HEADS + h) * LANES:(MLA_HEADS + h + 1) * LANES]
        rinv = lax.rsqrt((_sumsq(nope) + _sumsq(ropp)) / MLA_QK + EPS)
        qm_ref[h, :, :LANES] = (nope * rinv * gq_n * mla_scale).astype(BF16)
        qm_ref[h, :, LANES:] = (_rope_tile(ropp * rinv * gq_r, cm, sm) * mla_scale).astype(BF16)

    a_kv = p_ref[:, C_AKV:C_AKV + MLA_KV_RANK]
    a_kv = a_kv * lax.rsqrt(_sumsq(a_kv) / MLA_KV_RANK + EPS) * gkvl_ref[...]
    kv = jnp.dot(a_kv.astype(BF16), wkv_ref[...], preferred_element_type=F32)
    pe = p_ref[:, C_APE:C_APE + PE_PAD]
    ss_pe = _sumsq(pe)
    gk_n, gk_r = gmk_ref[:, :LANES], gmk_ref[:, LANES:]
    pe_rot = _rope_tile(pe * gk_r, cm, sm)
    for h in range(MLA_HEADS):
        nope = kv[:, h * LANES:(h + 1) * LANES]
        v = kv[:, (MLA_HEADS + h) * LANES:(MLA_HEADS + h + 1) * LANES]
        rinv = lax.rsqrt((_sumsq(nope) + ss_pe) / MLA_QK + EPS)
        km_ref[h, :, :LANES] = (nope * rinv * gk_n).astype(BF16)
        km_ref[h, :, LANES:] = (pe_rot * rinv).astype(BF16)
        vmt_ref[h, 0, :HEAD_DIM] = v.T.astype(BF16)
        vmt_ref[h, 0, HEAD_DIM:] = ones_rows

    def head_norm(col, g):
        t = p_ref[:, col:col + HEAD_DIM]
        return t * lax.rsqrt(_sumsq(t) / HEAD_DIM + EPS) * g

    for h in range(GQA_HEADS):
        t = head_norm(C_GQ + h * HEAD_DIM, ggq_ref[...])
        qg_ref[h] = (_rope_tile(t, ca, sa) * gqa_scale).astype(BF16)
    for h in range(GQA_KV_HEADS):
        t = head_norm(C_GK + h * HEAD_DIM, ggk_ref[...])
        kg_ref[h] = _rope_tile(t, ca, sa).astype(BF16)
        vgt_ref[h, 0, :HEAD_DIM] = p_ref[:, C_GV + h * HEAD_DIM:C_GV + (h + 1) * HEAD_DIM].T.astype(BF16)
        vgt_ref[h, 0, HEAD_DIM:] = ones_rows

    for h in range(NA_HEADS):
        qn_ref[h] = (head_norm(C_NQ + h * HEAD_DIM, gnq_ref[...]) * hd_scale).astype(BF16)
        kn_ref[h] = head_norm(C_NK + h * HEAD_DIM, gnk_ref[...]).astype(BF16)
        vn_ref[h] = p_ref[:, C_NV + h * HEAD_DIM:C_NV + (h + 1) * HEAD_DIM].astype(BF16)


def _mixin_kernel(x_ref, g_ref, sh_ref, sc_ref, w_ref, *rest):
    prep_args, proj = rest[:-2], rest[-2:]
    i = pl.program_id(0)

    @pl.when(i == 0)
    def _():
        proj[1][...] = jnp.zeros(proj[1].shape, F32)

    def step(dst, src):
        h = _modulated_norm(x_ref[...], g_ref[...], sc_ref[...], sh_ref[...]).astype(BF16)
        for j in range(D_IN_PAD // INPROJ_TN):
            cols = slice(j * INPROJ_TN, (j + 1) * INPROJ_TN)
            dst[:, cols] = jnp.dot(h, w_ref[:, cols], preferred_element_type=F32)
        _prep_body(src, *prep_args)

    @pl.when(i % 2 == 0)
    def _():
        step(proj[0], proj[1])

    @pl.when(i % 2 == 1)
    def _():
        step(proj[1], proj[0])


def _mixin(x, norm_g, mod, w_in_p, wq, wkv, gql, gkvl, gmq, gmk, ggq, ggk, gnq, gnk, cm, sm, ca, sa):
    n = x.shape[0]
    tm = min(MIX_TM, n)
    nb = n // tm

    def cur(i):
        return jnp.minimum(i, nb - 1)

    def prev(i):
        return jnp.maximum(i - 1, 0)

    def full(a):
        return pl.BlockSpec(a.shape, lambda i: (0,) * a.ndim)

    def rows(width):
        return pl.BlockSpec((tm, width), lambda i: (prev(i), 0))

    def heads(h, width):
        return pl.BlockSpec((h, tm, width), lambda i: (0, prev(i), 0))

    def heads_t(h):
        return pl.BlockSpec((h, 1, DV_EXT, tm), lambda i: (0, prev(i), 0, 0))

    out_shape = (
        jax.ShapeDtypeStruct((MLA_HEADS, n, MLA_QK_PAD), BF16),
        jax.ShapeDtypeStruct((MLA_HEADS, n, MLA_QK_PAD), BF16),
        jax.ShapeDtypeStruct((MLA_HEADS, nb, DV_EXT, tm), BF16),
        jax.ShapeDtypeStruct((GQA_HEADS, n, HEAD_DIM), BF16),
        jax.ShapeDtypeStruct((GQA_KV_HEADS, n, HEAD_DIM), BF16),
        jax.ShapeDtypeStruct((GQA_KV_HEADS, nb, DV_EXT, tm), BF16),
        jax.ShapeDtypeStruct((NA_HEADS, n, HEAD_DIM), BF16),
        jax.ShapeDtypeStruct((NA_HEADS, n, HEAD_DIM), BF16),
        jax.ShapeDtypeStruct((NA_HEADS, n, HEAD_DIM), BF16),
    )
    out_specs = (
        heads(MLA_HEADS, MLA_QK_PAD), heads(MLA_HEADS, MLA_QK_PAD), heads_t(MLA_HEADS),
        heads(GQA_HEADS, HEAD_DIM), heads(GQA_KV_HEADS, HEAD_DIM), heads_t(GQA_KV_HEADS),
        heads(NA_HEADS, HEAD_DIM), heads(NA_HEADS, HEAD_DIM), heads(NA_HEADS, HEAD_DIM),
    )
    small = (wq, wkv, gql, gkvl, gmq, gmk, ggq, ggk, gnq, gnk)
    return pl.pallas_call(
        _mixin_kernel,
        grid=(nb + 1,),
        in_specs=[
            pl.BlockSpec((tm, D_MODEL), lambda i: (cur(i), 0)),
            pl.BlockSpec((1, D_MODEL), lambda i: (0, 0)),
            pl.BlockSpec((1, D_MODEL), lambda i: (0, 0)),
            pl.BlockSpec((1, D_MODEL), lambda i: (0, 1)),
            pl.BlockSpec((D_MODEL, D_IN_PAD), lambda i: (0, 0), pipeline_mode=pl.Buffered(1)),
        ] + [full(a) for a in small] + [rows(LANES)] * 4,
        out_specs=out_specs,
        out_shape=out_shape,
        scratch_shapes=[pltpu.VMEM((tm, D_IN_PAD), F32), pltpu.VMEM((tm, D_IN_PAD), F32)],
        compiler_params=_cparams(1),
        name="mixin",
    )(x, norm_g, mod, mod, w_in_p, *small, cm, sm, ca, sa)


FLASH_TQ = 1024
FLASH_TK = 512
FLASH_UNROLL = 8


def _flash_kernel(q_ref, k_ref, vt_ref, o_ref, m_ref, acc_ref, s_ref, cm_ref, *, n_chunks, tk, n_sub, unroll):
    dv = acc_ref.shape[0] - (DV_EXT - HEAD_DIM)
    tsub = tk // n_sub
    m_ref[...] = jnp.full(m_ref.shape, -jnp.inf, F32)
    acc_ref[...] = jnp.zeros(acc_ref.shape, F32)

    def scores(c, slot):
        kc = k_ref[0, pl.ds(pl.multiple_of(c * tk, tk), tk), :]
        s = lax.dot_general(kc, q_ref[0], (((1,), (1,)), ((), ())), preferred_element_type=F32)
        s_ref[slot] = s
        cm_ref[slot] = jnp.max(s, axis=0, keepdims=True)

    def update(c, slot):
        m_prev = m_ref[...]
        m_new = jnp.maximum(m_prev, cm_ref[slot])
        p = jnp.exp2(s_ref[slot] - m_new).astype(BF16)
        alpha = jnp.exp2(m_prev - m_new)
        pv = jnp.dot(vt_ref[0, n_sub * c], p[:tsub], preferred_element_type=F32)
        for u in range(1, n_sub):
            pv = pv + jnp.dot(vt_ref[0, n_sub * c + u], p[u * tsub:(u + 1) * tsub], preferred_element_type=F32)
        acc_ref[...] = alpha * acc_ref[...] + pv
        m_ref[...] = m_new

    scores(0, 0)

    def body(j, carry):
        c0 = unroll * j
        for u in range(unroll):
            nxt = c0 + u + 1
            if u == unroll - 1:
                nxt = jnp.where(nxt == n_chunks, 0, nxt)
            scores(nxt, (u + 1) % 2)
            update(c0 + u, u % 2)
        return carry

    lax.fori_loop(0, n_chunks // unroll, body, 0)
    acc = acc_ref[...]
    o_ref[...] = (acc[:dv] / acc[dv:dv + 1]).T


def _flash(q, k, vt):
    H, S, dq = q.shape
    hk, n_vt, dv_ext, tsub = vt.shape
    tk = max(min(FLASH_TK, S), tsub)
    n_sub = tk // tsub
    n_chunks = S // tk
    dv = dv_ext - (DV_EXT - HEAD_DIM)
    group = H // hk
    tq = min(FLASH_TQ, S)
    unroll = FLASH_UNROLL if n_chunks % FLASH_UNROLL == 0 else 2
    assert n_chunks % unroll == 0
    return pl.pallas_call(
        functools.partial(_flash_kernel, n_chunks=n_chunks, tk=tk, n_sub=n_sub, unroll=unroll),
        grid=(H, S // tq),
        in_specs=[
            pl.BlockSpec((1, tq, dq), lambda h, i: (h, i, 0)),
            pl.BlockSpec((1, S, dq), lambda h, i: (h // group, 0, 0)),
            pl.BlockSpec((1, n_vt, dv_ext, tsub), lambda h, i: (h // group, 0, 0, 0)),
        ],
        out_specs=pl.BlockSpec((tq, dv), lambda h, i: (i, h)),
        out_shape=jax.ShapeDtypeStruct((S, H * dv), F32),
        scratch_shapes=[pltpu.VMEM((1, tq), F32), pltpu.VMEM((dv_ext, tq), F32),
                        pltpu.VMEM((2, tk, tq), F32), pltpu.VMEM((2, 1, tq), F32)],
        compiler_params=_cparams(2),
        name="flash",
    )(q, k, vt)


NEG_MIN = float(np.finfo(np.float32).min)
RPB_ROWS = 2 * NA_KH - 1
RPB_COLS = 2 * NA_KW - 1


def _na_rel(kind, i, j):
    if kind == 0:
        return j - i + NA_KH - 1, j < NA_KH
    if kind == 1:
        return j - i + NA_KH // 2 - 1, i <= j <= i + NA_KH - 1
    return j - i - 1, j >= NA_KROWS - NA_KH


def _na_bias_kernel(rpb_ref, o_ref):
    h = pl.program_id(0)
    shape = (GRID_W, LANES)
    c = lax.broadcasted_iota(I32, shape, 0)
    kc = lax.broadcasted_iota(I32, shape, 1) & (GRID_W - 1)
    cs = jnp.clip(c - NA_KW // 2, 0, GRID_W - NA_KW)
    in_win = (kc >= cs) & (kc < cs + NA_KW)
    dc = kc - c + NA_KW - 1
    lane_lo = lax.broadcasted_iota(I32, shape, 1) < GRID_W
    masked = jnp.full(shape, NEG_MIN, F32)
    cache = {}

    def toeplitz(dr):
        if dr not in cache:
            t = jnp.zeros(shape, F32)
            base = (h * RPB_ROWS + dr) * RPB_COLS
            for d in range(RPB_COLS):
                t = jnp.where(dc == d, rpb_ref[base + d], t)
            cache[dr] = jnp.where(in_win, t, NEG_MIN)
        return cache[dr]

    def half(kind, i, j):
        dr, valid = _na_rel(kind, i, j)
        return toeplitz(dr) if valid else masked

    for kind in range(3):
        for i in range(NA_QROWS):
            for jp in range(NA_KROWS // 2):
                piece = jnp.where(lane_lo, half(kind, i, 2 * jp), half(kind, i, 2 * jp + 1))
                o_ref[kind, 0, i * GRID_W:(i + 1) * GRID_W, jp * LANES:(jp + 1) * LANES] = piece


def _na_bias(rpb):
    return pl.pallas_call(
        _na_bias_kernel,
        grid=(NA_HEADS,),
        in_specs=[pl.BlockSpec(memory_space=pltpu.SMEM)],
        out_specs=pl.BlockSpec((3, 1, NA_TQ, NA_TK), lambda h: (0, h, 0, 0)),
        out_shape=jax.ShapeDtypeStruct((3, NA_HEADS, NA_TQ, NA_TK), F32),
        compiler_params=_cparams(1),
        name="na_bias",
    )(rpb.reshape(-1))


NA_SUB = 8


def _na_kernel(q_ref, k_ref, v_ref, b_ref, o_ref, *, grid_rows, sub):
    i = pl.program_id(1)
    nblk = grid_rows // NA_QROWS

    def scores(j):
        blk = i * sub + j
        kstart = jnp.clip(blk * NA_QROWS - NA_KH // 2, 0, grid_rows - NA_KROWS)
        off = pl.multiple_of(kstart * GRID_W, GRID_W)
        kind = jnp.where(blk == 0, 0, jnp.where(blk == nblk - 1, 2, 1))
        kw = k_ref[0, pl.ds(off, NA_TK), :]
        q = q_ref[0, j * NA_TQ:(j + 1) * NA_TQ, :]
        s = lax.dot_general(q, kw, (((1,), (1,)), ((), ())), preferred_element_type=F32)
        return s + b_ref[kind, 0], off

    def finish(j, s, off):
        vw = v_ref[0, pl.ds(off, NA_TK), :]
        p = jnp.exp(s - jnp.max(s, axis=-1, keepdims=True))
        l = jnp.sum(p, axis=-1, keepdims=True)
        o_ref[j * NA_TQ:(j + 1) * NA_TQ, :] = jnp.dot(p.astype(BF16), vw, preferred_element_type=F32) / l

    cur = scores(0)
    for j in range(sub):
        nxt = scores(j + 1) if j + 1 < sub else None
        finish(j, *cur)
        cur = nxt


def _na(q, k, v, bias):
    H, S, d = q.shape
    grid_rows = S // GRID_W
    nblk = grid_rows // NA_QROWS
    sub = int(np.gcd(nblk, NA_SUB))
    return pl.pallas_call(
        functools.partial(_na_kernel, grid_rows=grid_rows, sub=sub),
        grid=(H, nblk // sub),
        in_specs=[
            pl.BlockSpec((1, NA_TQ * sub, d), lambda h, i: (h, i, 0)),
            pl.BlockSpec((1, S, d), lambda h, i: (h, 0, 0)),
            pl.BlockSpec((1, S, d), lambda h, i: (h, 0, 0)),
            pl.BlockSpec((3, 1, NA_TQ, NA_TK), lambda h, i: (0, h, 0, 0)),
        ],
        out_specs=pl.BlockSpec((NA_TQ * sub, d), lambda h, i: (i, h)),
        out_shape=jax.ShapeDtypeStruct((S, H * d), F32),
        compiler_params=_cparams(2),
        name="na",
    )(q, k, v, bias)


W_MLA = MLA_HEADS * MLA_V
W_GQA = GQA_HEADS * HEAD_DIM
W_NA = NA_HEADS * HEAD_DIM


def _outproj_kernel(oa_ref, ob_ref, oc_ref, g_ref, w_ref, x_ref, g1_ref, o_ref, wb_ref):
    @pl.when(pl.program_id(0) == 0)
    def _():
        wb_ref[...] = w_ref[0].astype(BF16)

    def gnorm(o, g):
        return (o * lax.rsqrt(jnp.mean(o * o, axis=-1, keepdims=True) + EPS) * g).astype(BF16)

    a = gnorm(oa_ref[...], g_ref[:, :W_MLA])
    b = gnorm(ob_ref[...], g_ref[:, W_MLA:W_MLA + W_GQA])
    c = gnorm(oc_ref[...], g_ref[:, W_MLA + W_GQA:])
    y = jnp.dot(a, wb_ref[:W_MLA, :], preferred_element_type=F32)
    y = y + jnp.dot(b, wb_ref[W_MLA:W_MLA + W_GQA, :], preferred_element_type=F32)
    y = y + jnp.dot(c, wb_ref[W_MLA + W_GQA:, :], preferred_element_type=F32)
    o_ref[...] = x_ref[...] + g1_ref[...] * y


def _outproj(oa, ob, oc, gains, w_out_all, layer, x, mod):
    n = x.shape[0]
    tm = min(256, n)
    return pl.pallas_call(
        _outproj_kernel,
        grid=(n // tm,),
        in_specs=[
            pl.BlockSpec((tm, W_MLA), lambda i: (i, 0)),
            pl.BlockSpec((tm, W_GQA), lambda i: (i, 0)),
            pl.BlockSpec((tm, W_NA), lambda i: (i, 0)),
            pl.BlockSpec((1, D_MODEL), lambda i: (0, 0)),
            pl.BlockSpec((1, D_MODEL, D_MODEL), lambda i: (layer, 0, 0), pipeline_mode=pl.Buffered(1)),
            pl.BlockSpec((tm, D_MODEL), lambda i: (i, 0)),
            pl.BlockSpec((1, D_MODEL), lambda i: (0, 2)),
        ],
        out_specs=pl.BlockSpec((tm, D_MODEL), lambda i: (i, 0)),
        out_shape=jax.ShapeDtypeStruct((n, D_MODEL), F32),
        scratch_shapes=[pltpu.VMEM((D_MODEL, D_MODEL), BF16)],
        compiler_params=_cparams(1),
        name="outproj",
    )(oa, ob, oc, gains, w_out_all, x, mod)


ROUTER_TM = 512


def _store_pitched(ref, val):
    rows = val.shape[0]
    for s in range(ROW_SLABS):
        ref[pl.ds(s, rows, stride=ROW_PITCH), :] = val[:, s * LANES:(s + 1) * LANES]
    for s in range(ROW_SLABS, ROW_PITCH):
        ref[pl.ds(s, rows, stride=ROW_PITCH), :] = jnp.zeros((rows, LANES), val.dtype)


def _load_pitched_slab(ref, s, rows):
    return ref[pl.ds(s, rows, stride=ROW_PITCH), :]


def _first_argmax(v, vmax, n):
    idx = lax.broadcasted_iota(I32, v.shape, 0)
    return jnp.min(jnp.where(v == vmax, idx, n), axis=0, keepdims=True)


def _router_kernel(x_ref, g_ref, sh_ref, sc_ref, w_ref, b_ref, u_ref,
                   h_ref, ri_ref, rw_ref, cnt_ref, carry_ref):
    @pl.when(pl.program_id(0) == 0)
    def _():
        carry_ref[...] = jnp.zeros(carry_ref.shape, F32)

    h = _modulated_norm(x_ref[...], g_ref[...], sc_ref[...], sh_ref[...])
    _store_pitched(h_ref, h)
    def split(v):
        hi = v.astype(BF16)
        return hi, (v - hi.astype(F32)).astype(BF16)

    def nt_dot(a, b):
        return lax.dot_general(a, b, (((1,), (1,)), ((), ())), preferred_element_type=F32)

    w_hi, w_lo = split(w_ref[...])
    h_hi, h_lo = split(h)
    logits = nt_dot(w_hi, h_hi) + (nt_dot(w_hi, h_lo) + nt_dot(w_lo, h_hi)) + b_ref[...]
    tm = logits.shape[1]

    gl = logits[0:N_GROUPS]
    gmax = jnp.max(gl, axis=0, keepdims=True)
    gexp = jnp.exp(gl - gmax)
    gprob = gexp / jnp.sum(gexp, axis=0, keepdims=True)
    p_g = jnp.max(gprob, axis=0, keepdims=True)
    g_top = _first_argmax(gprob, p_g, N_GROUPS)

    e_in = jnp.zeros((EXPERTS_PER_GROUP, tm), F32)
    for g in range(N_GROUPS):
        lo = ROUTER_E0 + g * EXPERTS_PER_GROUP
        e_in = jnp.where(g_top == g, logits[lo:lo + EXPERTS_PER_GROUP], e_in)
    eexp = jnp.exp(e_in - jnp.max(e_in, axis=0, keepdims=True))
    eprob = eexp / jnp.sum(eexp, axis=0, keepdims=True)
    v1 = jnp.max(eprob, axis=0, keepdims=True)
    i1 = _first_argmax(eprob, v1, EXPERTS_PER_GROUP)
    row = lax.broadcasted_iota(I32, eprob.shape, 0)
    rest = jnp.where(row == i1, -1.0, eprob)
    v2 = jnp.max(rest, axis=0, keepdims=True)
    i2 = _first_argmax(rest, v2, EXPERTS_PER_GROUP)
    wsum = v1 + v2
    w1 = v1 / wsum * p_g
    w2 = v2 / wsum * p_g
    e1 = g_top * EXPERTS_PER_GROUP + i1
    e2 = g_top * EXPERTS_PER_GROUP + i2

    erow = lax.broadcasted_iota(I32, (N_EXPERTS, tm), 0)
    oh1 = (erow == e1).astype(F32)
    oh2 = (erow == e2).astype(F32)
    oh = oh1 + oh2
    before = jnp.dot(oh.astype(BF16), u_ref[...], preferred_element_type=F32) + carry_ref[...]
    r1 = jnp.sum(oh1 * before, axis=0, keepdims=True)
    r2 = jnp.sum(oh2 * before, axis=0, keepdims=True)
    carry_ref[...] = carry_ref[...] + jnp.sum(oh, axis=1, keepdims=True)
    cnt_ref[...] = jnp.broadcast_to(carry_ref[...], cnt_ref.shape)

    ri_ref[...] = jnp.zeros(ri_ref.shape, I32)
    ri_ref[0:1, :] = e1
    ri_ref[1:2, :] = e2
    ri_ref[2:3, :] = r1.astype(I32)
    ri_ref[3:4, :] = r2.astype(I32)
    rw_ref[...] = jnp.zeros(rw_ref.shape, F32)
    rw_ref[0:1, :] = w1
    rw_ref[1:2, :] = w2


def _router(x, norm_g, mod, w_r, b_r, upper):
    n = x.shape[0]
    tm = upper.shape[0]
    return pl.pallas_call(
        _router_kernel,
        grid=(n // tm,),
        in_specs=[
            pl.BlockSpec((tm, D_MODEL), lambda i: (i, 0)),
            pl.BlockSpec((1, D_MODEL), lambda i: (0, 0)),
            pl.BlockSpec((1, D_MODEL), lambda i: (0, 3)),
            pl.BlockSpec((1, D_MODEL), lambda i: (0, 4)),
            pl.BlockSpec((ROUTER_ROWS, D_MODEL), lambda i: (0, 0)),
            pl.BlockSpec((ROUTER_ROWS, 1), lambda i: (0, 0)),
            pl.BlockSpec((tm, tm), lambda i: (0, 0)),
        ],
        out_specs=(
            pl.BlockSpec((tm * ROW_PITCH, LANES), lambda i: (i, 0)),
            pl.BlockSpec((SUBLANES, tm), lambda i: (0, i)),
            pl.BlockSpec((SUBLANES, tm), lambda i: (0, i)),
            pl.BlockSpec((N_EXPERTS, LANES), lambda i: (0, 0)),
        ),
        out_shape=(
            jax.ShapeDtypeStruct((n * ROW_PITCH, LANES), F32),
            jax.ShapeDtypeStruct((SUBLANES, n), I32),
            jax.ShapeDtypeStruct((SUBLANES, n), F32),
            jax.ShapeDtypeStruct((N_EXPERTS, LANES), F32),
        ),
        scratch_shapes=[pltpu.VMEM((N_EXPERTS, 1), F32)],
        compiler_params=_cparams(1),
        name="router",
    )(x, norm_g, mod, mod, w_r, b_r, upper)


GATHER_UNROLL = 8


def _row_gather_start(src_hbm, idx_ref, base, dst, sem, n_rows):
    def body(r, carry):
        off = idx_ref[base + r]
        pltpu.make_async_copy(src_hbm.at[pl.ds(off, ROW_SLABS)],
                              dst.at[pl.ds(r * ROW_PITCH, ROW_SLABS)], sem).start()
        return carry

    lax.fori_loop(0, n_rows, body, 0, unroll=GATHER_UNROLL)


def _row_gather_wait(src_hbm, dst, sem, n_rows):
    n = n_rows * ROW_SLABS
    pltpu.make_async_copy(src_hbm.at[pl.ds(0, n)], dst.at[pl.ds(0, n)], sem).wait()


BLOCK_SLAB_ROWS = MOE_ROWS * ROW_PITCH
GATHER_SLOTS = 4
GATHER_DEPTH = GATHER_SLOTS


def _expert_kernel(src_ref, bstart_ref, h_hbm, wg_ref, wu_ref, wd_ref, y_hbm,
                   xg_ref, yo_ref, wgb_ref, wub_ref, wdb_ref, gsem, osem):
    e = pl.program_id(0)
    lo, hi = bstart_ref[e], bstart_ref[e + 1]
    total = bstart_ref[N_EXPERTS]

    def out_copy(g, slot):
        row0 = pl.multiple_of(g * BLOCK_SLAB_ROWS, BLOCK_SLAB_ROWS)
        return pltpu.make_async_copy(yo_ref.at[slot], y_hbm.at[pl.ds(row0, BLOCK_SLAB_ROWS)], osem.at[slot])

    @pl.when(e == 0)
    def _():
        for d in range(GATHER_DEPTH):
            @pl.when(d < total)
            def _():
                _row_gather_start(h_hbm, src_ref, d * MOE_ROWS, xg_ref.at[d], gsem.at[d], MOE_ROWS)

    @pl.when(hi > lo)
    def _():
        wgb_ref[...] = wg_ref[0].astype(BF16)
        wub_ref[...] = wu_ref[0].astype(BF16)
        wdb_ref[...] = wd_ref[0].astype(BF16)

    def block(g, carry):
        slot = g % 2
        gslot = g % GATHER_SLOTS

        _row_gather_wait(h_hbm, xg_ref.at[gslot], gsem.at[gslot], MOE_ROWS)
        xs = xg_ref.at[gslot]
        x = jnp.concatenate(
            [_load_pitched_slab(xs, s, MOE_ROWS).astype(BF16) for s in range(ROW_SLABS)], axis=1)
        gate = jnp.dot(x, wgb_ref[...], preferred_element_type=F32)
        up = jnp.dot(x, wub_ref[...], preferred_element_type=F32)
        a = (gate * (1.0 / (1.0 + jnp.exp(-gate))) * up).astype(BF16)
        y = jnp.dot(a, wdb_ref[...], preferred_element_type=F32)

        @pl.when(g + GATHER_DEPTH < total)
        def _():
            nslot = (g + GATHER_DEPTH) % GATHER_SLOTS
            _row_gather_start(h_hbm, src_ref, (g + GATHER_DEPTH) * MOE_ROWS, xg_ref.at[nslot], gsem.at[nslot],
                              MOE_ROWS)

        @pl.when(g >= 2)
        def _():
            out_copy(g - 2, slot).wait()

        _store_pitched(yo_ref.at[slot], y)
        out_copy(g, slot).start()
        return carry

    lax.fori_loop(lo, hi, block, 0)

    @pl.when(e == N_EXPERTS - 1)
    def _():
        @pl.when(total >= 2)
        def _():
            out_copy(total - 2, total % 2).wait()

        @pl.when(total >= 1)
        def _():
            out_copy(total - 1, (total - 1) % 2).wait()

        n_blocks = y_hbm.shape[0] // BLOCK_SLAB_ROWS
        yo_ref[0] = jnp.zeros(yo_ref.shape[1:], F32)

        def zero_start(g, carry):
            out_copy(g, 0).start()
            return carry

        def zero_wait(g, carry):
            out_copy(g, 0).wait()
            return carry

        lax.fori_loop(total, n_blocks, zero_start, 0)
        lax.fori_loop(total, n_blocks, zero_wait, 0)


def _experts(src_off, bstart, h, w_gate, w_up, w_down, layer, n_blocks):
    base = layer * N_EXPERTS
    grid_spec = pltpu.PrefetchScalarGridSpec(
        num_scalar_prefetch=2,
        grid=(N_EXPERTS,),
        in_specs=[
            pl.BlockSpec(memory_space=pl.ANY),
            pl.BlockSpec((1, D_MODEL, D_EXPERT), lambda e, src, bs: (base + e, 0, 0)),
            pl.BlockSpec((1, D_MODEL, D_EXPERT), lambda e, src, bs: (base + e, 0, 0)),
            pl.BlockSpec((1, D_EXPERT, D_MODEL), lambda e, src, bs: (base + e, 0, 0)),
        ],
        out_specs=pl.BlockSpec(memory_space=pl.ANY),
        scratch_shapes=[
            pltpu.VMEM((GATHER_SLOTS, BLOCK_SLAB_ROWS, LANES), F32),
            pltpu.VMEM((2, BLOCK_SLAB_ROWS, LANES), F32),
            pltpu.VMEM((D_MODEL, D_EXPERT), BF16),
            pltpu.VMEM((D_MODEL, D_EXPERT), BF16),
            pltpu.VMEM((D_EXPERT, D_MODEL), BF16),
            pltpu.SemaphoreType.DMA((GATHER_SLOTS,)),
            pltpu.SemaphoreType.DMA((2,)),
        ],
    )
    return pl.pallas_call(
        _expert_kernel,
        grid_spec=grid_spec,
        out_shape=jax.ShapeDtypeStruct((n_blocks * BLOCK_SLAB_ROWS, LANES), F32),
        compiler_params=_cparams(1),
        name="experts",
    )(src_off, bstart, h, w_gate, w_up, w_down)


COMB_TM = 256


def _combine_kernel(d1_ref, d2_ref, y_hbm, x_ref, w1_ref, w2_ref, g2_ref, o_ref, ya_ref, yb_ref, sem):
    i = pl.program_id(0)
    n = pl.num_programs(0)
    slot = i % 2

    def start(blk, s):
        _row_gather_start(y_hbm, d1_ref, blk * COMB_TM, ya_ref.at[s], sem.at[0, s], COMB_TM)
        _row_gather_start(y_hbm, d2_ref, blk * COMB_TM, yb_ref.at[s], sem.at[1, s], COMB_TM)

    @pl.when(i == 0)
    def _():
        start(0, 0)

    @pl.when(i + 1 < n)
    def _():
        start(i + 1, 1 - slot)

    _row_gather_wait(y_hbm, ya_ref.at[slot], sem.at[0, slot], COMB_TM)
    _row_gather_wait(y_hbm, yb_ref.at[slot], sem.at[1, slot], COMB_TM)
    w1, w2 = w1_ref[...], w2_ref[...]
    ya, yb = ya_ref.at[slot], yb_ref.at[slot]
    for s in range(ROW_SLABS):
        cols = slice(s * LANES, (s + 1) * LANES)
        y = w1 * _load_pitched_slab(ya, s, COMB_TM) + w2 * _load_pitched_slab(yb, s, COMB_TM)
        o_ref[:, cols] = x_ref[:, cols] + g2_ref[:, cols] * y


def _combine(dest1, dest2, ybuf, x, w1, w2, mod):
    n = x.shape[0]
    grid_spec = pltpu.PrefetchScalarGridSpec(
        num_scalar_prefetch=2,
        grid=(n // COMB_TM,),
        in_specs=[
            pl.BlockSpec(memory_space=pl.ANY),
            pl.BlockSpec((COMB_TM, D_MODEL), lambda i, d1, d2: (i, 0)),
            pl.BlockSpec((COMB_TM, 1), lambda i, d1, d2: (i, 0)),
            pl.BlockSpec((COMB_TM, 1), lambda i, d1, d2: (i, 0)),
            pl.BlockSpec((1, D_MODEL), lambda i, d1, d2: (0, 5)),
        ],
        out_specs=pl.BlockSpec((COMB_TM, D_MODEL), lambda i, d1, d2: (i, 0)),
        scratch_shapes=[
            pltpu.VMEM((2, COMB_TM * ROW_PITCH, LANES), F32),
            pltpu.VMEM((2, COMB_TM * ROW_PITCH, LANES), F32),
            pltpu.SemaphoreType.DMA((2, 2)),
        ],
    )
    return pl.pallas_call(
        _combine_kernel,
        grid_spec=grid_spec,
        out_shape=jax.ShapeDtypeStruct((n, D_MODEL), F32),
        compiler_params=_cparams(1),
        name="combine",
    )(dest1, dest2, ybuf, x, w1, w2, mod)


def _rope_angles(pos, d):
    f32 = np.float32
    inv = f32(ROPE_THETA) ** (-np.arange(0, d, 2, dtype=f32) / f32(d))
    return pos.astype(f32)[:, None] * inv[None, :]


def _rope_tables(S):
    t = np.arange(S)
    z = np.zeros((S, MLA_ROPE), np.float32)
    a = _rope_angles(t, MLA_ROPE)
    cm = np.concatenate([np.cos(a), np.cos(a), z], axis=1)
    sm = np.concatenate([-np.sin(a), np.sin(a), z], axis=1)
    ar = _rope_angles(t // GRID_W, HEAD_DIM // 2)
    ac = _rope_angles(t % GRID_W, HEAD_DIM // 2)
    ca = np.concatenate([np.cos(ar), np.cos(ar), np.cos(ac), np.cos(ac)], axis=1)
    sa = np.concatenate([-np.sin(ar), np.sin(ar), -np.sin(ac), np.sin(ac)], axis=1)
    return tuple(jnp.asarray(v, F32) for v in (cm, sm, ca, sa))


def _pad_lanes(a, width):
    return jnp.concatenate([a, jnp.zeros(a.shape[:-1] + (width - a.shape[-1],), a.dtype)], axis=-1)


def _layer_weights(w_in, w_uq, w_ukv, qk_q_g, qk_k_g):
    split = C_APE + MLA_ROPE
    w_in_p = jnp.concatenate(
        [w_in[:, :split], jnp.zeros((D_MODEL, PE_PAD - MLA_ROPE), F32), w_in[:, split:]], axis=1
    ).astype(BF16)
    wq = w_uq.reshape(MLA_Q_RANK, MLA_HEADS, MLA_QK)
    wq_n = wq[:, :, :MLA_NOPE].reshape(MLA_Q_RANK, MLA_HEADS * LANES)
    wq_r = _pad_lanes(wq[:, :, MLA_NOPE:], LANES).reshape(MLA_Q_RANK, MLA_HEADS * LANES)
    wq_p = jnp.concatenate([wq_n, wq_r], axis=1).astype(BF16)
    wkv = w_ukv.reshape(MLA_KV_RANK, MLA_HEADS, MLA_NOPE + MLA_V)
    wkv_p = jnp.concatenate(
        [wkv[:, :, :MLA_NOPE].reshape(MLA_KV_RANK, -1), wkv[:, :, MLA_NOPE:].reshape(MLA_KV_RANK, -1)],
        axis=1).astype(BF16)
    gmq = _pad_lanes(qk_q_g[None, :], MLA_QK_PAD)
    gmk = _pad_lanes(qk_k_g[None, :], MLA_QK_PAD)
    return w_in_p, wq_p, wkv_p, gmq, gmk


def _routing_plan(ri, cnt, n_blocks):
    n = ri.shape[1]
    e1, e2, r1, r2 = ri[0], ri[1], ri[2], ri[3]
    counts = cnt[:, 0].astype(I32)
    padded = ((counts + MOE_ROWS - 1) // MOE_ROWS) * MOE_ROWS
    pad_end = jnp.cumsum(padded)
    pad_start = pad_end - padded
    dest1 = pad_start[e1] + r1
    dest2 = pad_start[e2] + r2
    tok_off = jnp.arange(n, dtype=I32) * ROW_PITCH
    pad_src = (jnp.arange(n_blocks * MOE_ROWS, dtype=I32) % n) * ROW_PITCH
    src_off = pad_src.at[jnp.concatenate([dest1, dest2])].set(
        jnp.concatenate([tok_off, tok_off]), unique_indices=True)
    bstart = jnp.concatenate([jnp.zeros((1,), I32), pad_end // MOE_ROWS]).astype(I32)
    return dest1 * ROW_PITCH, dest2 * ROW_PITCH, src_off, bstart


def kernel(x, c, ada_w, ada_b, norm1_g, norm2_g, w_in, mla_q_norm_g, mla_w_uq, mla_kv_norm_g, mla_w_ukv, mla_qk_q_g, mla_qk_k_g, gqa_q_g, gqa_k_g, na_q_g, na_k_g, na_rpb, mix_out_norm_g, w_out, router_group_w, router_group_b, router_expert_w, router_expert_b, expert_w_gate, expert_w_up, expert_w_down):
    B, S, D = x.shape
    assert B == 1 and D == D_MODEL and S % (GRID_W * NA_QROWS) == 0 and S // GRID_W >= NA_KROWS
    depth = ada_w.shape[0]
    n = B * S
    xt = x.reshape(n, D)

    mod_all = _adaln(c, ada_w, ada_b)
    cm, sm, ca, sa = _rope_tables(S)
    tm_r = min(ROUTER_TM, n)
    upper = jnp.asarray(np.arange(tm_r)[:, None] < np.arange(tm_r)[None, :], BF16)
    n_blocks = -(-(n * 2 + N_EXPERTS * (MOE_ROWS - 1)) // MOE_ROWS)
    wg_all = expert_w_gate.reshape(depth * N_EXPERTS, D_MODEL, D_EXPERT)
    wu_all = expert_w_up.reshape(depth * N_EXPERTS, D_MODEL, D_EXPERT)
    wd_all = expert_w_down.reshape(depth * N_EXPERTS, D_EXPERT, D_MODEL)

    def row(v):
        return v[None, :]

    for l in range(depth):
        mod = mod_all[l]
        w_in_p, wq_p, wkv_p, gmq, gmk = _layer_weights(
            w_in[l], mla_w_uq[l], mla_w_ukv[l], mla_qk_q_g[l], mla_qk_k_g[l])

        qm, km, vmt, qg, kg, vgt, qn, kn, vn = _mixin(
            xt, row(norm1_g[l]), mod, w_in_p, wq_p, wkv_p, row(mla_q_norm_g[l]), row(mla_kv_norm_g[l]), gmq, gmk,
            row(gqa_q_g[l]), row(gqa_k_g[l]), row(na_q_g[l]), row(na_k_g[l]), cm, sm, ca, sa)
        o_a = _flash(qm, km, vmt)
        o_b = _flash(qg, kg, vgt)
        o_c = _na(qn, kn, vn, _na_bias(na_rpb[l]))
        xt = _outproj(o_a, o_b, o_c, row(mix_out_norm_g[l]), w_out, l, xt, mod)

        w_r = jnp.concatenate(
            [router_group_w[l].T, jnp.zeros((ROUTER_E0 - N_GROUPS, D), F32), router_expert_w[l].T], axis=0)
        b_r = jnp.concatenate(
            [router_group_b[l], jnp.zeros((ROUTER_E0 - N_GROUPS,), F32), router_expert_b[l]])[:, None]
        h2, ri, rw, cnt = _router(xt, row(norm2_g[l]), mod, w_r, b_r, upper)
        off1, off2, src_off, bstart = _routing_plan(ri, cnt, n_blocks)
        ybuf = _experts(src_off, bstart, h2, wg_all, wu_all, wd_all, l, n_blocks)
        xt = _combine(off1, off2, ybuf, xt, rw[0][:, None], rw[1][:, None], mod)

    return xt.reshape(B, S, D)
```

```python
import functools

import jax
import jax.numpy as jnp
import numpy as np
from jax import lax
from jax.experimental import pallas as pl
from jax.experimental.pallas import tpu as pltpu

F32 = jnp.float32
BF16 = jnp.bfloat16
I32 = jnp.int32

D_MODEL = 2048
GRID_W = 64
HEAD_DIM = 128
ROPE_THETA = 10000.0
EPS = 1e-6
MLA_HEADS = 6
MLA_Q_RANK = 384
MLA_KV_RANK = 256
MLA_NOPE = 128
MLA_ROPE = 64
MLA_V = 128
MLA_QK = MLA_NOPE + MLA_ROPE
GQA_HEADS = 4
GQA_KV_HEADS = 2
NA_HEADS = 6
NA_KH = 8
NA_KW = 16
N_GROUPS = 4
EXPERTS_PER_GROUP = 8
N_EXPERTS = N_GROUPS * EXPERTS_PER_GROUP
D_EXPERT = 512

LANES = 128
SUBLANES = 8
VMEM_LIMIT_BYTES = 56 * 1024 * 1024

PE_PAD = LANES
C_AQ = 0
C_AKV = C_AQ + MLA_Q_RANK
C_APE = C_AKV + MLA_KV_RANK
C_GQ = C_APE + PE_PAD
C_GK = C_GQ + GQA_HEADS * HEAD_DIM
C_GV = C_GK + GQA_KV_HEADS * HEAD_DIM
C_NQ = C_GV + GQA_KV_HEADS * HEAD_DIM
C_NK = C_NQ + NA_HEADS * HEAD_DIM
C_NV = C_NK + NA_HEADS * HEAD_DIM
D_IN_PAD = C_NV + NA_HEADS * HEAD_DIM
MLA_QK_PAD = 2 * LANES

NA_QROWS = 4
NA_KROWS = 12
NA_TQ = NA_QROWS * GRID_W
NA_TK = NA_KROWS * GRID_W

LOG2E = float(np.log2(np.e))
DV_EXT = HEAD_DIM + 16

MOE_ROWS = 256
ROW_SLABS = D_MODEL // LANES
ROW_PITCH = 20
ROUTER_ROWS = 40
ROUTER_E0 = 8


def _cparams(n_axes):
    return pltpu.CompilerParams(
        dimension_semantics=("arbitrary",) * n_axes,
        vmem_limit_bytes=VMEM_LIMIT_BYTES,
    )


ADA_TN = 1024
ADA_RC = 256


def _adaln_kernel(c_ref, w_ref, b_ref, o_ref):
    def body(i, acc):
        r0 = pl.multiple_of(i * ADA_RC, ADA_RC)
        c = c_ref[pl.ds(r0, ADA_RC), :]
        cond = c * (1.0 / (1.0 + jnp.exp(-c)))
        w = w_ref[0, pl.ds(r0, ADA_RC), :]
        return acc + jnp.sum(w * cond, axis=0, keepdims=True)

    acc = lax.fori_loop(0, D_MODEL // ADA_RC, body, jnp.zeros((1, ADA_TN), F32))
    o_ref[0] = acc + b_ref[0]


def _adaln(c, ada_w, ada_b):
    L = ada_w.shape[0]
    n_out = ada_w.shape[2]
    return pl.pallas_call(
        _adaln_kernel,
        grid=(L, n_out // ADA_TN),
        in_specs=[
            pl.BlockSpec((D_MODEL, 1), lambda l, j: (0, 0)),
            pl.BlockSpec((1, D_MODEL, ADA_TN), lambda l, j: (l, 0, j)),
            pl.BlockSpec((1, 1, ADA_TN), lambda l, j: (l, 0, j)),
        ],
        out_specs=pl.BlockSpec((1, 1, ADA_TN), lambda l, j: (l, 0, j)),
        out_shape=jax.ShapeDtypeStruct((L, 1, n_out), F32),
        compiler_params=_cparams(2),
        name="adaln",
    )(c.reshape(D_MODEL, 1), ada_w, ada_b.reshape(L, 1, n_out))


def _modulated_norm(x, g, sc, sh):
    ms = jnp.mean(x * x, axis=-1, keepdims=True)
    y = x * lax.rsqrt(ms + EPS) * g
    return y * (1.0 + sc) + sh


MIX_TM = 256
INPROJ_TN = 1024


def _rot_swap(x):
    lane = lax.broadcasted_iota(I32, x.shape, 1)
    fwd = pltpu.roll(x, LANES - 32, 1)
    bwd = pltpu.roll(x, 32, 1)
    return jnp.where((lane & 63) < 32, fwd, bwd)


def _rope_tile(x, cos, sin_signed):
    return x * cos + _rot_swap(x) * sin_signed


def _sumsq(x):
    return jnp.sum(x * x, axis=-1, keepdims=True)


def _prep_body(p_ref, wq_ref, wkv_ref, gql_ref, gkvl_ref, gmq_ref, gmk_ref, ggq_ref, ggk_ref,
                 gnq_ref, gnk_ref, cm_ref, sm_ref, ca_ref, sa_ref,
                 qm_ref, km_ref, vmt_ref, qg_ref, kg_ref, vgt_ref, qn_ref, kn_ref, vn_ref):
    mla_scale = MLA_QK ** -0.5 * LOG2E
    gqa_scale = HEAD_DIM ** -0.5 * LOG2E
    hd_scale = HEAD_DIM ** -0.5
    tm = p_ref.shape[0]
    ones_rows = (lax.broadcasted_iota(I32, (DV_EXT - HEAD_DIM, tm), 0) == 0).astype(BF16)
    cm, sm = cm_ref[...], sm_ref[...]
    ca, sa = ca_ref[...], sa_ref[...]

    a_q = p_ref[:, C_AQ:C_AQ + MLA_Q_RANK]
    a_q = a_q * lax.rsqrt(_sumsq(a_q) / MLA_Q_RANK + EPS) * gql_ref[...]
    q = jnp.dot(a_q.astype(BF16), wq_ref[...], preferred_element_type=F32)
    gq_n, gq_r = gmq_ref[:, :LANES] * mla_scale, gmq_ref[:, LANES:] * mla_scale
    for h in range(MLA_HEADS):
        nope = q[:, h * LANES:(h + 1) * LANES]
        ropp = q[:, (MLA_HEADS + h) * LANES:(MLA_HEADS + h + 1) * LANES]
        rinv = lax.rsqrt((_sumsq(nope) + _sumsq(ropp)) / MLA_QK + EPS)
        qm_ref[h, :, :LANES] = (nope * rinv * gq_n).astype(BF16)
        qm_ref[h, :, LANES:] = _rope_tile(ropp * rinv * gq_r, cm, sm).astype(BF16)

    a_kv = p_ref[:, C_AKV:C_AKV + MLA_KV_RANK]
    a_kv = a_kv * lax.rsqrt(_sumsq(a_kv) / MLA_KV_RANK + EPS) * gkvl_ref[...]
    kv = jnp.dot(a_kv.astype(BF16), wkv_ref[...], preferred_element_type=F32)
    pe = p_ref[:, C_APE:C_APE + PE_PAD]
    ss_pe = _sumsq(pe)
    gk_n, gk_r = gmk_ref[:, :LANES], gmk_ref[:, LANES:]
    pe_rot = _rope_tile(pe * gk_r, cm, sm)
    for h in range(MLA_HEADS):
        nope = kv[:, h * LANES:(h + 1) * LANES]
        v = kv[:, (MLA_HEADS + h) * LANES:(MLA_HEADS + h + 1) * LANES]
        rinv = lax.rsqrt((_sumsq(nope) + ss_pe) / MLA_QK + EPS)
        km_ref[h, :, :LANES] = (nope * rinv * gk_n).astype(BF16)
        km_ref[h, :, LANES:] = (pe_rot * rinv).astype(BF16)
        vmt_ref[h, 0, :HEAD_DIM] = v.T.astype(BF16)
        vmt_ref[h, 0, HEAD_DIM:] = ones_rows

    def head_norm(col, g):
        t = p_ref[:, col:col + HEAD_DIM]
        return t * lax.rsqrt(_sumsq(t) / HEAD_DIM + EPS) * g

    ggq_s = ggq_ref[...] * gqa_scale
    for h in range(GQA_HEADS):
        qg_ref[h] = _rope_tile(head_norm(C_GQ + h * HEAD_DIM, ggq_s), ca, sa).astype(BF16)
    for h in range(GQA_KV_HEADS):
        t = head_norm(C_GK + h * HEAD_DIM, ggk_ref[...])
        kg_ref[h] = _rope_tile(t, ca, sa).astype(BF16)
        vgt_ref[h, 0, :HEAD_DIM] = p_ref[:, C_GV + h * HEAD_DIM:C_GV + (h + 1) * HEAD_DIM].T.astype(BF16)
        vgt_ref[h, 0, HEAD_DIM:] = ones_rows

    gnq_s = gnq_ref[...] * hd_scale
    for h in range(NA_HEADS):
        qn_ref[h] = head_norm(C_NQ + h * HEAD_DIM, gnq_s).astype(BF16)
        kn_ref[h] = head_norm(C_NK + h * HEAD_DIM, gnk_ref[...]).astype(BF16)
        vn_ref[h] = p_ref[:, C_NV + h * HEAD_DIM:C_NV + (h + 1) * HEAD_DIM].astype(BF16)


def _mixin_kernel(x_ref, g_ref, sh_ref, sc_ref, w_ref, *rest):
    prep_args, proj = rest[:-2], rest[-2:]
    i = pl.program_id(0)

    @pl.when(i == 0)
    def _():
        proj[1][...] = jnp.zeros(proj[1].shape, F32)

    def step(dst, src):
        h = _modulated_norm(x_ref[...], g_ref[...], sc_ref[...], sh_ref[...]).astype(BF16)
        for j in range(D_IN_PAD // INPROJ_TN):
            cols = slice(j * INPROJ_TN, (j + 1) * INPROJ_TN)
            dst[:, cols] = jnp.dot(h, w_ref[:, cols], preferred_element_type=F32)
        _prep_body(src, *prep_args)

    @pl.when(i % 2 == 0)
    def _():
        step(proj[0], proj[1])

    @pl.when(i % 2 == 1)
    def _():
        step(proj[1], proj[0])


def _mixin(x, norm_g, mod, w_in_p, wq, wkv, gql, gkvl, gmq, gmk, ggq, ggk, gnq, gnk, cm, sm, ca, sa):
    n = x.shape[0]
    tm = min(MIX_TM, n)
    nb = n // tm

    def cur(i):
        return jnp.minimum(i, nb - 1)

    def prev(i):
        return jnp.maximum(i - 1, 0)

    def full(a):
        return pl.BlockSpec(a.shape, lambda i: (0,) * a.ndim)

    def rows(width):
        return pl.BlockSpec((tm, width), lambda i: (prev(i), 0))

    def heads(h, width):
        return pl.BlockSpec((h, tm, width), lambda i: (0, prev(i), 0))

    def heads_t(h):
        return pl.BlockSpec((h, 1, DV_EXT, tm), lambda i: (0, prev(i), 0, 0))

    out_shape = (
        jax.ShapeDtypeStruct((MLA_HEADS, n, MLA_QK_PAD), BF16),
        jax.ShapeDtypeStruct((MLA_HEADS, n, MLA_QK_PAD), BF16),
        jax.ShapeDtypeStruct((MLA_HEADS, nb, DV_EXT, tm), BF16),
        jax.ShapeDtypeStruct((GQA_HEADS, n, HEAD_DIM), BF16),
        jax.ShapeDtypeStruct((GQA_KV_HEADS, n, HEAD_DIM), BF16),
        jax.ShapeDtypeStruct((GQA_KV_HEADS, nb, DV_EXT, tm), BF16),
        jax.ShapeDtypeStruct((NA_HEADS, n, HEAD_DIM), BF16),
        jax.ShapeDtypeStruct((NA_HEADS, n, HEAD_DIM), BF16),
        jax.ShapeDtypeStruct((NA_HEADS, n, HEAD_DIM), BF16),
    )
    out_specs = (
        heads(MLA_HEADS, MLA_QK_PAD), heads(MLA_HEADS, MLA_QK_PAD), heads_t(MLA_HEADS),
        heads(GQA_HEADS, HEAD_DIM), heads(GQA_KV_HEADS, HEAD_DIM), heads_t(GQA_KV_HEADS),
        heads(NA_HEADS, HEAD_DIM), heads(NA_HEADS, HEAD_DIM), heads(NA_HEADS, HEAD_DIM),
    )
    small = (wq, wkv, gql, gkvl, gmq, gmk, ggq, ggk, gnq, gnk)
    return pl.pallas_call(
        _mixin_kernel,
        grid=(nb + 1,),
        in_specs=[
            pl.BlockSpec((tm, D_MODEL), lambda i: (cur(i), 0)),
            pl.BlockSpec((1, D_MODEL), lambda i: (0, 0)),
            pl.BlockSpec((1, D_MODEL), lambda i: (0, 0)),
            pl.BlockSpec((1, D_MODEL), lambda i: (0, 1)),
            pl.BlockSpec((D_MODEL, D_IN_PAD), lambda i: (0, 0), pipeline_mode=pl.Buffered(1)),
        ] + [full(a) for a in small] + [rows(LANES)] * 4,
        out_specs=out_specs,
        out_shape=out_shape,
        scratch_shapes=[pltpu.VMEM((tm, D_IN_PAD), F32), pltpu.VMEM((tm, D_IN_PAD), F32)],
        compiler_params=_cparams(1),
        name="mixin",
    )(x, norm_g, mod, mod, w_in_p, *small, cm, sm, ca, sa)


FLASH_TQ = 1024
FLASH_TK = 512
FLASH_UNROLL = 8


def _flash_kernel(q_ref, k_ref, vt_ref, o_ref, m_ref, acc_ref, s_ref, cm_ref, *, n_chunks, tk, n_sub, unroll):
    dv = acc_ref.shape[0] - (DV_EXT - HEAD_DIM)
    tsub = tk // n_sub
    m_ref[...] = jnp.full(m_ref.shape, -jnp.inf, F32)
    acc_ref[...] = jnp.zeros(acc_ref.shape, F32)

    def scores(c, slot):
        kc = k_ref[0, pl.ds(pl.multiple_of(c * tk, tk), tk), :]
        s = lax.dot_general(kc, q_ref[0], (((1,), (1,)), ((), ())), preferred_element_type=F32)
        s_ref[slot] = s
        cm_ref[slot] = jnp.max(s, axis=0, keepdims=True)

    def update(c, slot):
        m_prev = m_ref[...]
        m_new = jnp.maximum(m_prev, cm_ref[slot])
        p = jnp.exp2(s_ref[slot] - m_new).astype(BF16)
        alpha = jnp.exp2(m_prev - m_new)
        pv = jnp.dot(vt_ref[0, n_sub * c], p[:tsub], preferred_element_type=F32)
        for u in range(1, n_sub):
            pv = pv + jnp.dot(vt_ref[0, n_sub * c + u], p[u * tsub:(u + 1) * tsub], preferred_element_type=F32)
        acc_ref[...] = alpha * acc_ref[...] + pv
        m_ref[...] = m_new

    scores(0, 0)

    def body(j, carry):
        c0 = unroll * j
        for u in range(unroll):
            nxt = c0 + u + 1
            if u == unroll - 1:
                nxt = jnp.where(nxt == n_chunks, 0, nxt)
            scores(nxt, (u + 1) % 2)
            update(c0 + u, u % 2)
        return carry

    lax.fori_loop(0, n_chunks // unroll, body, 0)
    acc = acc_ref[...]
    o_ref[...] = (acc[:dv] / acc[dv:dv + 1]).T


def _flash(q, k, vt):
    H, S, dq = q.shape
    hk, n_vt, dv_ext, tsub = vt.shape
    tk = max(min(FLASH_TK, S), tsub)
    n_sub = tk // tsub
    n_chunks = S // tk
    dv = dv_ext - (DV_EXT - HEAD_DIM)
    group = H // hk
    tq = min(FLASH_TQ, S)
    unroll = FLASH_UNROLL if n_chunks % FLASH_UNROLL == 0 else 2
    assert n_chunks % unroll == 0
    return pl.pallas_call(
        functools.partial(_flash_kernel, n_chunks=n_chunks, tk=tk, n_sub=n_sub, unroll=unroll),
        grid=(H, S // tq),
        in_specs=[
            pl.BlockSpec((1, tq, dq), lambda h, i: (h, i, 0)),
            pl.BlockSpec((1, S, dq), lambda h, i: (h // group, 0, 0)),
            pl.BlockSpec((1, n_vt, dv_ext, tsub), lambda h, i: (h // group, 0, 0, 0)),
        ],
        out_specs=pl.BlockSpec((tq, dv), lambda h, i: (i, h)),
        out_shape=jax.ShapeDtypeStruct((S, H * dv), F32),
        scratch_shapes=[pltpu.VMEM((1, tq), F32), pltpu.VMEM((dv_ext, tq), F32),
                        pltpu.VMEM((2, tk, tq), F32), pltpu.VMEM((2, 1, tq), F32)],
        compiler_params=_cparams(2),
        name="flash",
    )(q, k, vt)


NEG_MIN = float(np.finfo(np.float32).min)
RPB_ROWS = 2 * NA_KH - 1
RPB_COLS = 2 * NA_KW - 1


def _na_rel(kind, i, j):
    if kind == 0:
        return j - i + NA_KH - 1, j < NA_KH
    if kind == 1:
        return j - i + NA_KH // 2 - 1, i <= j <= i + NA_KH - 1
    return j - i - 1, j >= NA_KROWS - NA_KH


def _na_bias_kernel(rpb_ref, o_ref):
    h = pl.program_id(0)
    shape = (GRID_W, LANES)
    c = lax.broadcasted_iota(I32, shape, 0)
    kc = lax.broadcasted_iota(I32, shape, 1) & (GRID_W - 1)
    cs = jnp.clip(c - NA_KW // 2, 0, GRID_W - NA_KW)
    in_win = (kc >= cs) & (kc < cs + NA_KW)
    dc = kc - c + NA_KW - 1
    lane_lo = lax.broadcasted_iota(I32, shape, 1) < GRID_W
    masked = jnp.full(shape, NEG_MIN, F32)
    cache = {}

    def toeplitz(dr):
        if dr not in cache:
            t = jnp.zeros(shape, F32)
            base = (h * RPB_ROWS + dr) * RPB_COLS
            for d in range(RPB_COLS):
                t = jnp.where(dc == d, rpb_ref[base + d], t)
            cache[dr] = jnp.where(in_win, t, NEG_MIN)
        return cache[dr]

    def half(kind, i, j):
        dr, valid = _na_rel(kind, i, j)
        return toeplitz(dr) if valid else masked

    for kind in range(3):
        for i in range(NA_QROWS):
            for jp in range(NA_KROWS // 2):
                piece = jnp.where(lane_lo, half(kind, i, 2 * jp), half(kind, i, 2 * jp + 1))
                o_ref[kind, 0, i * GRID_W:(i + 1) * GRID_W, jp * LANES:(jp + 1) * LANES] = piece


def _na_bias(rpb_all):
    n_tables = rpb_all.shape[0] * NA_HEADS
    return pl.pallas_call(
        _na_bias_kernel,
        grid=(n_tables,),
        in_specs=[pl.BlockSpec(memory_space=pltpu.SMEM)],
        out_specs=pl.BlockSpec((3, 1, NA_TQ, NA_TK), lambda h: (0, h, 0, 0)),
        out_shape=jax.ShapeDtypeStruct((3, n_tables, NA_TQ, NA_TK), F32),
        compiler_params=_cparams(1),
        name="na_bias",
    )(rpb_all.reshape(-1))


NA_SUB = 8


def _na_kernel(q_ref, k_ref, v_ref, b_ref, o_ref, *, grid_rows, sub):
    i = pl.program_id(1)
    nblk = grid_rows // NA_QROWS

    def scores(j):
        blk = i * sub + j
        kstart = jnp.clip(blk * NA_QROWS - NA_KH // 2, 0, grid_rows - NA_KROWS)
        off = pl.multiple_of(kstart * GRID_W, GRID_W)
        kind = jnp.where(blk == 0, 0, jnp.where(blk == nblk - 1, 2, 1))
        kw = k_ref[0, pl.ds(off, NA_TK), :]
        q = q_ref[0, j * NA_TQ:(j + 1) * NA_TQ, :]
        s = lax.dot_general(q, kw, (((1,), (1,)), ((), ())), preferred_element_type=F32)
        return s + b_ref[kind, 0], off

    def finish(j, s, off):
        vw = v_ref[0, pl.ds(off, NA_TK), :]
        p = jnp.exp(s - jnp.max(s, axis=-1, keepdims=True))
        l = jnp.sum(p, axis=-1, keepdims=True)
        o_ref[j * NA_TQ:(j + 1) * NA_TQ, :] = jnp.dot(p.astype(BF16), vw, preferred_element_type=F32) / l

    cur = scores(0)
    for j in range(sub):
        nxt = scores(j + 1) if j + 1 < sub else None
        finish(j, *cur)
        cur = nxt


def _na(q, k, v, bias, layer):
    H, S, d = q.shape
    grid_rows = S // GRID_W
    nblk = grid_rows // NA_QROWS
    sub = int(np.gcd(nblk, NA_SUB))
    return pl.pallas_call(
        functools.partial(_na_kernel, grid_rows=grid_rows, sub=sub),
        grid=(H, nblk // sub),
        in_specs=[
            pl.BlockSpec((1, NA_TQ * sub, d), lambda h, i: (h, i, 0)),
            pl.BlockSpec((1, S, d), lambda h, i: (h, 0, 0)),
            pl.BlockSpec((1, S, d), lambda h, i: (h, 0, 0)),
            pl.BlockSpec((3, 1, NA_TQ, NA_TK), lambda h, i: (0, layer * H + h, 0, 0)),
        ],
        out_specs=pl.BlockSpec((NA_TQ * sub, d), lambda h, i: (i, h)),
        out_shape=jax.ShapeDtypeStruct((S, H * d), F32),
        compiler_params=_cparams(2),
        name="na",
    )(q, k, v, bias)


W_MLA = MLA_HEADS * MLA_V
W_GQA = GQA_HEADS * HEAD_DIM
W_NA = NA_HEADS * HEAD_DIM


def _outproj_kernel(oa_ref, ob_ref, oc_ref, g_ref, w_ref, x_ref, g1_ref, o_ref, wb_ref):
    @pl.when(pl.program_id(0) == 0)
    def _():
        wb_ref[...] = w_ref[0].astype(BF16)

    def gnorm(o, g):
        return (o * lax.rsqrt(jnp.mean(o * o, axis=-1, keepdims=True) + EPS) * g).astype(BF16)

    a = gnorm(oa_ref[...], g_ref[:, :W_MLA])
    b = gnorm(ob_ref[...], g_ref[:, W_MLA:W_MLA + W_GQA])
    c = gnorm(oc_ref[...], g_ref[:, W_MLA + W_GQA:])
    y = jnp.dot(a, wb_ref[:W_MLA, :], preferred_element_type=F32)
    y = y + jnp.dot(b, wb_ref[W_MLA:W_MLA + W_GQA, :], preferred_element_type=F32)
    y = y + jnp.dot(c, wb_ref[W_MLA + W_GQA:, :], preferred_element_type=F32)
    o_ref[...] = x_ref[...] + g1_ref[...] * y


def _outproj(oa, ob, oc, gains, w_out_all, layer, x, mod):
    n = x.shape[0]
    tm = min(256, n)
    return pl.pallas_call(
        _outproj_kernel,
        grid=(n // tm,),
        in_specs=[
            pl.BlockSpec((tm, W_MLA), lambda i: (i, 0)),
            pl.BlockSpec((tm, W_GQA), lambda i: (i, 0)),
            pl.BlockSpec((tm, W_NA), lambda i: (i, 0)),
            pl.BlockSpec((1, D_MODEL), lambda i: (0, 0)),
            pl.BlockSpec((1, D_MODEL, D_MODEL), lambda i: (layer, 0, 0), pipeline_mode=pl.Buffered(1)),
            pl.BlockSpec((tm, D_MODEL), lambda i: (i, 0)),
            pl.BlockSpec((1, D_MODEL), lambda i: (0, 2)),
        ],
        out_specs=pl.BlockSpec((tm, D_MODEL), lambda i: (i, 0)),
        out_shape=jax.ShapeDtypeStruct((n, D_MODEL), F32),
        scratch_shapes=[pltpu.VMEM((D_MODEL, D_MODEL), BF16)],
        compiler_params=_cparams(1),
        name="outproj",
    )(oa, ob, oc, gains, w_out_all, x, mod)


ROUTER_TM = 512


def _store_pitched(ref, val):
    rows = val.shape[0]
    for s in range(ROW_SLABS):
        ref[pl.ds(s, rows, stride=ROW_PITCH), :] = val[:, s * LANES:(s + 1) * LANES]
    for s in range(ROW_SLABS, ROW_PITCH):
        ref[pl.ds(s, rows, stride=ROW_PITCH), :] = jnp.zeros((rows, LANES), val.dtype)


def _load_pitched_slab(ref, s, rows):
    return ref[pl.ds(s, rows, stride=ROW_PITCH), :]


def _first_argmax(v, vmax, n):
    idx = lax.broadcasted_iota(I32, v.shape, 0)
    return jnp.min(jnp.where(v == vmax, idx, n), axis=0, keepdims=True)


def _router_kernel(x_ref, g_ref, sh_ref, sc_ref, w_ref, b_ref, u_ref,
                   h_ref, ri_ref, rw_ref, cnt_ref, carry_ref):
    @pl.when(pl.program_id(0) == 0)
    def _():
        carry_ref[...] = jnp.zeros(carry_ref.shape, F32)

    h = _modulated_norm(x_ref[...], g_ref[...], sc_ref[...], sh_ref[...])
    _store_pitched(h_ref, h)
    def split(v):
        hi = v.astype(BF16)
        return hi, (v - hi.astype(F32)).astype(BF16)

    def nt_dot(a, b):
        return lax.dot_general(a, b, (((1,), (1,)), ((), ())), preferred_element_type=F32)

    w_hi, w_lo = split(w_ref[...])
    h_hi, h_lo = split(h)
    logits = nt_dot(w_hi, h_hi) + (nt_dot(w_hi, h_lo) + nt_dot(w_lo, h_hi)) + b_ref[...]
    tm = logits.shape[1]

    gl = logits[0:N_GROUPS]
    gmax = jnp.max(gl, axis=0, keepdims=True)
    gexp = jnp.exp(gl - gmax)
    gprob = gexp / jnp.sum(gexp, axis=0, keepdims=True)
    p_g = jnp.max(gprob, axis=0, keepdims=True)
    g_top = _first_argmax(gprob, p_g, N_GROUPS)

    e_in = jnp.zeros((EXPERTS_PER_GROUP, tm), F32)
    for g in range(N_GROUPS):
        lo = ROUTER_E0 + g * EXPERTS_PER_GROUP
        e_in = jnp.where(g_top == g, logits[lo:lo + EXPERTS_PER_GROUP], e_in)
    eexp = jnp.exp(e_in - jnp.max(e_in, axis=0, keepdims=True))
    eprob = eexp / jnp.sum(eexp, axis=0, keepdims=True)
    v1 = jnp.max(eprob, axis=0, keepdims=True)
    i1 = _first_argmax(eprob, v1, EXPERTS_PER_GROUP)
    row = lax.broadcasted_iota(I32, eprob.shape, 0)
    rest = jnp.where(row == i1, -1.0, eprob)
    v2 = jnp.max(rest, axis=0, keepdims=True)
    i2 = _first_argmax(rest, v2, EXPERTS_PER_GROUP)
    wsum = v1 + v2
    w1 = v1 / wsum * p_g
    w2 = v2 / wsum * p_g
    e1 = g_top * EXPERTS_PER_GROUP + i1
    e2 = g_top * EXPERTS_PER_GROUP + i2

    erow = lax.broadcasted_iota(I32, (N_EXPERTS, tm), 0)
    oh1 = (erow == e1).astype(F32)
    oh2 = (erow == e2).astype(F32)
    oh = oh1 + oh2
    before = jnp.dot(oh.astype(BF16), u_ref[...], preferred_element_type=F32) + carry_ref[...]
    r1 = jnp.sum(oh1 * before, axis=0, keepdims=True)
    r2 = jnp.sum(oh2 * before, axis=0, keepdims=True)
    carry_ref[...] = carry_ref[...] + jnp.sum(oh, axis=1, keepdims=True)
    cnt_ref[...] = jnp.broadcast_to(carry_ref[...], cnt_ref.shape)

    ri_ref[...] = jnp.zeros(ri_ref.shape, I32)
    ri_ref[0:1, :] = e1
    ri_ref[1:2, :] = e2
    ri_ref[2:3, :] = r1.astype(I32)
    ri_ref[3:4, :] = r2.astype(I32)
    rw_ref[...] = jnp.zeros(rw_ref.shape, F32)
    rw_ref[0:1, :] = w1
    rw_ref[1:2, :] = w2


def _router(x, norm_g, mod, w_r, b_r, upper):
    n = x.shape[0]
    tm = upper.shape[0]
    return pl.pallas_call(
        _router_kernel,
        grid=(n // tm,),
        in_specs=[
            pl.BlockSpec((tm, D_MODEL), lambda i: (i, 0)),
            pl.BlockSpec((1, D_MODEL), lambda i: (0, 0)),
            pl.BlockSpec((1, D_MODEL), lambda i: (0, 3)),
            pl.BlockSpec((1, D_MODEL), lambda i: (0, 4)),
            pl.BlockSpec((ROUTER_ROWS, D_MODEL), lambda i: (0, 0)),
            pl.BlockSpec((ROUTER_ROWS, 1), lambda i: (0, 0)),
            pl.BlockSpec((tm, tm), lambda i: (0, 0)),
        ],
        out_specs=(
            pl.BlockSpec((tm * ROW_PITCH, LANES), lambda i: (i, 0)),
            pl.BlockSpec((SUBLANES, tm), lambda i: (0, i)),
            pl.BlockSpec((SUBLANES, tm), lambda i: (0, i)),
            pl.BlockSpec((N_EXPERTS, LANES), lambda i: (0, 0)),
        ),
        out_shape=(
            jax.ShapeDtypeStruct((n * ROW_PITCH, LANES), F32),
            jax.ShapeDtypeStruct((SUBLANES, n), I32),
            jax.ShapeDtypeStruct((SUBLANES, n), F32),
            jax.ShapeDtypeStruct((N_EXPERTS, LANES), F32),
        ),
        scratch_shapes=[pltpu.VMEM((N_EXPERTS, 1), F32)],
        compiler_params=_cparams(1),
        name="router",
    )(x, norm_g, mod, mod, w_r, b_r, upper)


GATHER_UNROLL = 8


def _row_gather_start(src_hbm, idx_ref, base, dst, sem, n_rows):
    def body(r, carry):
        off = idx_ref[base + r]
        pltpu.make_async_copy(src_hbm.at[pl.ds(off, ROW_SLABS)],
                              dst.at[pl.ds(r * ROW_PITCH, ROW_SLABS)], sem).start()
        return carry

    lax.fori_loop(0, n_rows, body, 0, unroll=GATHER_UNROLL)


def _row_gather_wait(src_hbm, dst, sem, n_rows):
    n = n_rows * ROW_SLABS
    pltpu.make_async_copy(src_hbm.at[pl.ds(0, n)], dst.at[pl.ds(0, n)], sem).wait()


BLOCK_SLAB_ROWS = MOE_ROWS * ROW_PITCH
GATHER_SLOTS = 4
GATHER_DEPTH = GATHER_SLOTS


def _expert_kernel(src_ref, bstart_ref, h_hbm, wg_ref, wu_ref, wd_ref, y_hbm,
                   xg_ref, yo_ref, wgb_ref, wub_ref, wdb_ref, gsem, osem):
    e = pl.program_id(0)
    lo, hi = bstart_ref[e], bstart_ref[e + 1]
    total = bstart_ref[N_EXPERTS]

    def out_copy(g, slot):
        row0 = pl.multiple_of(g * BLOCK_SLAB_ROWS, BLOCK_SLAB_ROWS)
        return pltpu.make_async_copy(yo_ref.at[slot], y_hbm.at[pl.ds(row0, BLOCK_SLAB_ROWS)], osem.at[slot])

    @pl.when(e == 0)
    def _():
        for d in range(GATHER_DEPTH):
            @pl.when(d < total)
            def _():
                _row_gather_start(h_hbm, src_ref, d * MOE_ROWS, xg_ref.at[d], gsem.at[d], MOE_ROWS)

    @pl.when(hi > lo)
    def _():
        wgb_ref[...] = wg_ref[0].astype(BF16)
        wub_ref[...] = wu_ref[0].astype(BF16)
        wdb_ref[...] = wd_ref[0].astype(BF16)

    def block(g, carry):
        slot = g % 2
        gslot = g % GATHER_SLOTS

        _row_gather_wait(h_hbm, xg_ref.at[gslot], gsem.at[gslot], MOE_ROWS)
        xs = xg_ref.at[gslot]
        x = jnp.concatenate(
            [_load_pitched_slab(xs, s, MOE_ROWS).astype(BF16) for s in range(ROW_SLABS)], axis=1)
        gate = jnp.dot(x, wgb_ref[...], preferred_element_type=F32)
        up = jnp.dot(x, wub_ref[...], preferred_element_type=F32)
        a = (gate * (1.0 / (1.0 + jnp.exp(-gate))) * up).astype(BF16)
        y = jnp.dot(a, wdb_ref[...], preferred_element_type=F32)

        @pl.when(g + GATHER_DEPTH < total)
        def _():
            nslot = (g + GATHER_DEPTH) % GATHER_SLOTS
            _row_gather_start(h_hbm, src_ref, (g + GATHER_DEPTH) * MOE_ROWS, xg_ref.at[nslot], gsem.at[nslot],
                              MOE_ROWS)

        @pl.when(g >= 2)
        def _():
            out_copy(g - 2, slot).wait()

        _store_pitched(yo_ref.at[slot], y)
        out_copy(g, slot).start()
        return carry

    lax.fori_loop(lo, hi, block, 0)

    @pl.when(e == N_EXPERTS - 1)
    def _():
        @pl.when(total >= 2)
        def _():
            out_copy(total - 2, total % 2).wait()

        @pl.when(total >= 1)
        def _():
            out_copy(total - 1, (total - 1) % 2).wait()

        n_blocks = y_hbm.shape[0] // BLOCK_SLAB_ROWS
        yo_ref[0] = jnp.zeros(yo_ref.shape[1:], F32)

        def zero_start(g, carry):
            out_copy(g, 0).start()
            return carry

        def zero_wait(g, carry):
            out_copy(g, 0).wait()
            return carry

        lax.fori_loop(total, n_blocks, zero_start, 0)
        lax.fori_loop(total, n_blocks, zero_wait, 0)


def _experts(src_off, bstart, h, w_gate, w_up, w_down, layer, n_blocks):
    base = layer * N_EXPERTS
    grid_spec = pltpu.PrefetchScalarGridSpec(
        num_scalar_prefetch=2,
        grid=(N_EXPERTS,),
        in_specs=[
            pl.BlockSpec(memory_space=pl.ANY),
            pl.BlockSpec((1, D_MODEL, D_EXPERT), lambda e, src, bs: (base + e, 0, 0)),
            pl.BlockSpec((1, D_MODEL, D_EXPERT), lambda e, src, bs: (base + e, 0, 0)),
            pl.BlockSpec((1, D_EXPERT, D_MODEL), lambda e, src, bs: (base + e, 0, 0)),
        ],
        out_specs=pl.BlockSpec(memory_space=pl.ANY),
        scratch_shapes=[
            pltpu.VMEM((GATHER_SLOTS, BLOCK_SLAB_ROWS, LANES), F32),
            pltpu.VMEM((2, BLOCK_SLAB_ROWS, LANES), F32),
            pltpu.VMEM((D_MODEL, D_EXPERT), BF16),
            pltpu.VMEM((D_MODEL, D_EXPERT), BF16),
            pltpu.VMEM((D_EXPERT, D_MODEL), BF16),
            pltpu.SemaphoreType.DMA((GATHER_SLOTS,)),
            pltpu.SemaphoreType.DMA((2,)),
        ],
    )
    return pl.pallas_call(
        _expert_kernel,
        grid_spec=grid_spec,
        out_shape=jax.ShapeDtypeStruct((n_blocks * BLOCK_SLAB_ROWS, LANES), F32),
        compiler_params=_cparams(1),
        name="experts",
    )(src_off, bstart, h, w_gate, w_up, w_down)


COMB_TM = 256


def _combine_kernel(d1_ref, d2_ref, y_hbm, x_ref, w1_ref, w2_ref, g2_ref, o_ref, ya_ref, yb_ref, sem):
    i = pl.program_id(0)
    n = pl.num_programs(0)
    slot = i % 2

    def start(blk, s):
        _row_gather_start(y_hbm, d1_ref, blk * COMB_TM, ya_ref.at[s], sem.at[0, s], COMB_TM)
        _row_gather_start(y_hbm, d2_ref, blk * COMB_TM, yb_ref.at[s], sem.at[1, s], COMB_TM)

    @pl.when(i == 0)
    def _():
        start(0, 0)

    @pl.when(i + 1 < n)
    def _():
        start(i + 1, 1 - slot)

    _row_gather_wait(y_hbm, ya_ref.at[slot], sem.at[0, slot], COMB_TM)
    _row_gather_wait(y_hbm, yb_ref.at[slot], sem.at[1, slot], COMB_TM)
    w1, w2 = w1_ref[...], w2_ref[...]
    ya, yb = ya_ref.at[slot], yb_ref.at[slot]
    for s in range(ROW_SLABS):
        cols = slice(s * LANES, (s + 1) * LANES)
        y = w1 * _load_pitched_slab(ya, s, COMB_TM) + w2 * _load_pitched_slab(yb, s, COMB_TM)
        o_ref[:, cols] = x_ref[:, cols] + g2_ref[:, cols] * y


def _combine(dest1, dest2, ybuf, x, w1, w2, mod):
    n = x.shape[0]
    grid_spec = pltpu.PrefetchScalarGridSpec(
        num_scalar_prefetch=2,
        grid=(n // COMB_TM,),
        in_specs=[
            pl.BlockSpec(memory_space=pl.ANY),
            pl.BlockSpec((COMB_TM, D_MODEL), lambda i, d1, d2: (i, 0)),
            pl.BlockSpec((COMB_TM, 1), lambda i, d1, d2: (i, 0)),
            pl.BlockSpec((COMB_TM, 1), lambda i, d1, d2: (i, 0)),
            pl.BlockSpec((1, D_MODEL), lambda i, d1, d2: (0, 5)),
        ],
        out_specs=pl.BlockSpec((COMB_TM, D_MODEL), lambda i, d1, d2: (i, 0)),
        scratch_shapes=[
            pltpu.VMEM((2, COMB_TM * ROW_PITCH, LANES), F32),
            pltpu.VMEM((2, COMB_TM * ROW_PITCH, LANES), F32),
            pltpu.SemaphoreType.DMA((2, 2)),
        ],
    )
    return pl.pallas_call(
        _combine_kernel,
        grid_spec=grid_spec,
        out_shape=jax.ShapeDtypeStruct((n, D_MODEL), F32),
        compiler_params=_cparams(1),
        name="combine",
    )(dest1, dest2, ybuf, x, w1, w2, mod)


def _rope_angles(pos, d):
    f32 = np.float32
    inv = f32(ROPE_THETA) ** (-np.arange(0, d, 2, dtype=f32) / f32(d))
    return pos.astype(f32)[:, None] * inv[None, :]


def _rope_tables(S):
    t = np.arange(S)
    z = np.zeros((S, MLA_ROPE), np.float32)
    a = _rope_angles(t, MLA_ROPE)
    cm = np.concatenate([np.cos(a), np.cos(a), z], axis=1)
    sm = np.concatenate([-np.sin(a), np.sin(a), z], axis=1)
    ar = _rope_angles(t // GRID_W, HEAD_DIM // 2)
    ac = _rope_angles(t % GRID_W, HEAD_DIM // 2)
    ca = np.concatenate([np.cos(ar), np.cos(ar), np.cos(ac), np.cos(ac)], axis=1)
    sa = np.concatenate([-np.sin(ar), np.sin(ar), -np.sin(ac), np.sin(ac)], axis=1)
    return tuple(jnp.asarray(v, F32) for v in (cm, sm, ca, sa))


def _pad_lanes(a, width):
    return jnp.concatenate([a, jnp.zeros(a.shape[:-1] + (width - a.shape[-1],), a.dtype)], axis=-1)


def _layer_weights(w_in, w_uq, w_ukv, qk_q_g, qk_k_g):
    split = C_APE + MLA_ROPE
    w_in_p = jnp.concatenate(
        [w_in[:, :split], jnp.zeros((D_MODEL, PE_PAD - MLA_ROPE), F32), w_in[:, split:]], axis=1
    ).astype(BF16)
    wq = w_uq.reshape(MLA_Q_RANK, MLA_HEADS, MLA_QK)
    wq_n = wq[:, :, :MLA_NOPE].reshape(MLA_Q_RANK, MLA_HEADS * LANES)
    wq_r = _pad_lanes(wq[:, :, MLA_NOPE:], LANES).reshape(MLA_Q_RANK, MLA_HEADS * LANES)
    wq_p = jnp.concatenate([wq_n, wq_r], axis=1).astype(BF16)
    wkv = w_ukv.reshape(MLA_KV_RANK, MLA_HEADS, MLA_NOPE + MLA_V)
    wkv_p = jnp.concatenate(
        [wkv[:, :, :MLA_NOPE].reshape(MLA_KV_RANK, -1), wkv[:, :, MLA_NOPE:].reshape(MLA_KV_RANK, -1)],
        axis=1).astype(BF16)
    gmq = _pad_lanes(qk_q_g[None, :], MLA_QK_PAD)
    gmk = _pad_lanes(qk_k_g[None, :], MLA_QK_PAD)
    return w_in_p, wq_p, wkv_p, gmq, gmk


def _slot_kernel(start_ref, ri_ref, o_ref):
    e = ri_ref[0:2, :]
    first = jnp.zeros(e.shape, I32)
    for k in range(N_EXPERTS):
        first = jnp.where(e == k, start_ref[k], first)
    slot = first + ri_ref[2:4, :]
    o_ref[...] = jnp.zeros(o_ref.shape, I32)
    o_ref[0:2, :] = slot
    o_ref[2:4, :] = slot * ROW_PITCH


def _slots(pad_start, ri):
    return pl.pallas_call(
        _slot_kernel,
        grid=(1,),
        in_specs=[pl.BlockSpec(memory_space=pltpu.SMEM), pl.BlockSpec(ri.shape, lambda i: (0, 0))],
        out_specs=pl.BlockSpec(ri.shape, lambda i: (0, 0)),
        out_shape=jax.ShapeDtypeStruct(ri.shape, I32),
        compiler_params=_cparams(1),
        name="slots",
    )(pad_start, ri)


def _routing_plan(ri, cnt, n_blocks):
    n = ri.shape[1]
    counts = cnt[:, 0].astype(I32)
    padded = ((counts + MOE_ROWS - 1) // MOE_ROWS) * MOE_ROWS
    pad_end = jnp.cumsum(padded)
    pad_start = pad_end - padded
    slots = _slots(pad_start, ri)
    dest1, dest2 = slots[0], slots[1]
    tok_off = jnp.arange(n, dtype=I32) * ROW_PITCH
    pad_src = (jnp.arange(n_blocks * MOE_ROWS, dtype=I32) % n) * ROW_PITCH
    src_off = pad_src.at[jnp.concatenate([dest1, dest2])].set(
        jnp.concatenate([tok_off, tok_off]), unique_indices=True)
    bstart = jnp.concatenate([jnp.zeros((1,), I32), pad_end // MOE_ROWS]).astype(I32)
    return slots[2], slots[3], src_off, bstart


def kernel(x, c, ada_w, ada_b, norm1_g, norm2_g, w_in, mla_q_norm_g, mla_w_uq, mla_kv_norm_g, mla_w_ukv, mla_qk_q_g, mla_qk_k_g, gqa_q_g, gqa_k_g, na_q_g, na_k_g, na_rpb, mix_out_norm_g, w_out, router_group_w, router_group_b, router_expert_w, router_expert_b, expert_w_gate, expert_w_up, expert_w_down):
    B, S, D = x.shape
    assert B == 1 and D == D_MODEL and S % (GRID_W * NA_QROWS) == 0 and S // GRID_W >= NA_KROWS
    depth = ada_w.shape[0]
    n = B * S
    xt = x.reshape(n, D)

    mod_all = _adaln(c, ada_w, ada_b)
    na_bias = _na_bias(na_rpb)
    cm, sm, ca, sa = _rope_tables(S)
    tm_r = min(ROUTER_TM, n)
    upper = jnp.asarray(np.arange(tm_r)[:, None] < np.arange(tm_r)[None, :], BF16)
    n_blocks = -(-(n * 2 + N_EXPERTS * (MOE_ROWS - 1)) // MOE_ROWS)
    wg_all = expert_w_gate.reshape(depth * N_EXPERTS, D_MODEL, D_EXPERT)
    wu_all = expert_w_up.reshape(depth * N_EXPERTS, D_MODEL, D_EXPERT)
    wd_all = expert_w_down.reshape(depth * N_EXPERTS, D_EXPERT, D_MODEL)

    def row(v):
        return v[None, :]

    for l in range(depth):
        mod = mod_all[l]
        w_in_p, wq_p, wkv_p, gmq, gmk = _layer_weights(
            w_in[l], mla_w_uq[l], mla_w_ukv[l], mla_qk_q_g[l], mla_qk_k_g[l])

        qm, km, vmt, qg, kg, vgt, qn, kn, vn = _mixin(
            xt, row(norm1_g[l]), mod, w_in_p, wq_p, wkv_p, row(mla_q_norm_g[l]), row(mla_kv_norm_g[l]), gmq, gmk,
            row(gqa_q_g[l]), row(gqa_k_g[l]), row(na_q_g[l]), row(na_k_g[l]), cm, sm, ca, sa)
        o_a = _flash(qm, km, vmt)
        o_b = _flash(qg, kg, vgt)
        o_c = _na(qn, kn, vn, na_bias, l)
        xt = _outproj(o_a, o_b, o_c, row(mix_out_norm_g[l]), w_out, l, xt, mod)

        w_r = jnp.concatenate(
            [router_group_w[l].T, jnp.zeros((ROUTER_E0 - N_GROUPS, D), F32), router_expert_w[l].T], axis=0)
        b_r = jnp.concatenate(
            [router_group_b[l], jnp.zeros((ROUTER_E0 - N_GROUPS,), F32), router_expert_b[l]])[:, None]
        h2, ri, rw, cnt = _router(xt, row(norm2_g[l]), mod, w_r, b_r, upper)
        off1, off2, src_off, bstart = _routing_plan(ri, cnt, n_blocks)
        ybuf = _experts(src_off, bstart, h2, wg_all, wu_all, wd_all, l, n_blocks)
        xt = _combine(off1, off2, ybuf, xt, rw[0][:, None], rw[1][:, None], mod)

    return xt.reshape(B, S, D)
```

```python
import functools

import jax
import jax.numpy as jnp
import numpy as np
from jax import lax
from jax.experimental import pallas as pl
from jax.experimental.pallas import tpu as pltpu

F32 = jnp.float32
BF16 = jnp.bfloat16
I32 = jnp.int32

D_MODEL = 2048
GRID_W = 64
HEAD_DIM = 128
ROPE_THETA = 10000.0
EPS = 1e-6
MLA_HEADS = 6
MLA_Q_RANK = 384
MLA_KV_RANK = 256
MLA_NOPE = 128
MLA_ROPE = 64
MLA_V = 128
MLA_QK = MLA_NOPE + MLA_ROPE
GQA_HEADS = 4
GQA_KV_HEADS = 2
NA_HEADS = 6
NA_KH = 8
NA_KW = 16
N_GROUPS = 4
EXPERTS_PER_GROUP = 8
N_EXPERTS = N_GROUPS * EXPERTS_PER_GROUP
D_EXPERT = 512

LANES = 128
SUBLANES = 8
VMEM_LIMIT_BYTES = 56 * 1024 * 1024

PE_PAD = LANES
C_AQ = 0
C_AKV = C_AQ + MLA_Q_RANK
C_APE = C_AKV + MLA_KV_RANK
C_GQ = C_APE + PE_PAD
C_GK = C_GQ + GQA_HEADS * HEAD_DIM
C_GV = C_GK + GQA_KV_HEADS * HEAD_DIM
C_NQ = C_GV + GQA_KV_HEADS * HEAD_DIM
C_NK = C_NQ + NA_HEADS * HEAD_DIM
C_NV = C_NK + NA_HEADS * HEAD_DIM
D_IN_PAD = C_NV + NA_HEADS * HEAD_DIM
MLA_QK_PAD = 2 * LANES

NA_QROWS = 4
NA_KROWS = 12
NA_TQ = NA_QROWS * GRID_W
NA_TK = NA_KROWS * GRID_W

LOG2E = float(np.log2(np.e))
DV_EXT = HEAD_DIM + 16

MOE_ROWS = 256
ROW_SLABS = D_MODEL // LANES
ROW_PITCH = 20
ROUTER_ROWS = 40
ROUTER_E0 = 8


def _cparams(n_axes):
    return pltpu.CompilerParams(
        dimension_semantics=("arbitrary",) * n_axes,
        vmem_limit_bytes=VMEM_LIMIT_BYTES,
    )


ADA_TN = 1024
ADA_RC = 256


def _adaln_kernel(c_ref, w_ref, b_ref, o_ref):
    def body(i, acc):
        r0 = pl.multiple_of(i * ADA_RC, ADA_RC)
        c = c_ref[pl.ds(r0, ADA_RC), :]
        cond = c * (1.0 / (1.0 + jnp.exp(-c)))
        w = w_ref[0, pl.ds(r0, ADA_RC), :]
        return acc + jnp.sum(w * cond, axis=0, keepdims=True)

    acc = lax.fori_loop(0, D_MODEL // ADA_RC, body, jnp.zeros((1, ADA_TN), F32))
    o_ref[0] = acc + b_ref[0]


def _adaln(c, ada_w, ada_b):
    L = ada_w.shape[0]
    n_out = ada_w.shape[2]
    return pl.pallas_call(
        _adaln_kernel,
        grid=(L, n_out // ADA_TN),
        in_specs=[
            pl.BlockSpec((D_MODEL, 1), lambda l, j: (0, 0)),
            pl.BlockSpec((1, D_MODEL, ADA_TN), lambda l, j: (l, 0, j)),
            pl.BlockSpec((1, 1, ADA_TN), lambda l, j: (l, 0, j)),
        ],
        out_specs=pl.BlockSpec((1, 1, ADA_TN), lambda l, j: (l, 0, j)),
        out_shape=jax.ShapeDtypeStruct((L, 1, n_out), F32),
        compiler_params=_cparams(2),
        name="adaln",
    )(c.reshape(D_MODEL, 1), ada_w, ada_b.reshape(L, 1, n_out))


def _modulated_norm(x, g, sc, sh):
    ms = jnp.mean(x * x, axis=-1, keepdims=True)
    y = x * lax.rsqrt(ms + EPS) * g
    return y * (1.0 + sc) + sh


MIX_TM = 256
INPROJ_TN = 1024


def _rot_swap(x):
    lane = lax.broadcasted_iota(I32, x.shape, 1)
    fwd = pltpu.roll(x, LANES - 32, 1)
    bwd = pltpu.roll(x, 32, 1)
    return jnp.where((lane & 63) < 32, fwd, bwd)


def _rope_tile(x, cos, sin_signed):
    return x * cos + _rot_swap(x) * sin_signed


def _sumsq(x):
    return jnp.sum(x * x, axis=-1, keepdims=True)


def _prep_body(p_ref, wq_ref, wkv_ref, gql_ref, gkvl_ref, gmq_ref, gmk_ref, ggq_ref, ggk_ref,
                 gnq_ref, gnk_ref, cm_ref, sm_ref, ca_ref, sa_ref,
                 qm_ref, km_ref, vmt_ref, qg_ref, kg_ref, vgt_ref, qn_ref, kn_ref, vn_ref):
    mla_scale = MLA_QK ** -0.5 * LOG2E
    gqa_scale = HEAD_DIM ** -0.5 * LOG2E
    hd_scale = HEAD_DIM ** -0.5
    tm = p_ref.shape[0]
    ones_rows = (lax.broadcasted_iota(I32, (DV_EXT - HEAD_DIM, tm), 0) == 0).astype(BF16)
    cm, sm = cm_ref[...], sm_ref[...]
    ca, sa = ca_ref[...], sa_ref[...]

    a_q = p_ref[:, C_AQ:C_AQ + MLA_Q_RANK]
    a_q = a_q * lax.rsqrt(_sumsq(a_q) / MLA_Q_RANK + EPS) * gql_ref[...]
    q = jnp.dot(a_q.astype(BF16), wq_ref[...], preferred_element_type=F32)
    gq_n, gq_r = gmq_ref[:, :LANES] * mla_scale, gmq_ref[:, LANES:] * mla_scale
    for h in range(MLA_HEADS):
        nope = q[:, h * LANES:(h + 1) * LANES]
        ropp = q[:, (MLA_HEADS + h) * LANES:(MLA_HEADS + h + 1) * LANES]
        rinv = lax.rsqrt((_sumsq(nope) + _sumsq(ropp)) / MLA_QK + EPS)
        qm_ref[h, :, :LANES] = (nope * rinv * gq_n).astype(BF16)
        qm_ref[h, :, LANES:] = _rope_tile(ropp * rinv * gq_r, cm, sm).astype(BF16)

    a_kv = p_ref[:, C_AKV:C_AKV + MLA_KV_RANK]
    a_kv = a_kv * lax.rsqrt(_sumsq(a_kv) / MLA_KV_RANK + EPS) * gkvl_ref[...]
    kv = jnp.dot(a_kv.astype(BF16), wkv_ref[...], preferred_element_type=F32)
    pe = p_ref[:, C_APE:C_APE + PE_PAD]
    ss_pe = _sumsq(pe)
    gk_n, gk_r = gmk_ref[:, :LANES], gmk_ref[:, LANES:]
    pe_rot = _rope_tile(pe * gk_r, cm, sm)
    for h in range(MLA_HEADS):
        nope = kv[:, h * LANES:(h + 1) * LANES]
        v = kv[:, (MLA_HEADS + h) * LANES:(MLA_HEADS + h + 1) * LANES]
        rinv = lax.rsqrt((_sumsq(nope) + ss_pe) / MLA_QK + EPS)
        km_ref[h, :, :LANES] = (nope * rinv * gk_n).astype(BF16)
        km_ref[h, :, LANES:] = (pe_rot * rinv).astype(BF16)
        vmt_ref[h, 0, :HEAD_DIM] = v.T.astype(BF16)
        vmt_ref[h, 0, HEAD_DIM:] = ones_rows

    def head_norm(col, g):
        t = p_ref[:, col:col + HEAD_DIM]
        return t * lax.rsqrt(_sumsq(t) / HEAD_DIM + EPS) * g

    ggq_s = ggq_ref[...] * gqa_scale
    for h in range(GQA_HEADS):
        qg_ref[h] = _rope_tile(head_norm(C_GQ + h * HEAD_DIM, ggq_s), ca, sa).astype(BF16)
    for h in range(GQA_KV_HEADS):
        t = head_norm(C_GK + h * HEAD_DIM, ggk_ref[...])
        kg_ref[h] = _rope_tile(t, ca, sa).astype(BF16)
        vgt_ref[h, 0, :HEAD_DIM] = p_ref[:, C_GV + h * HEAD_DIM:C_GV + (h + 1) * HEAD_DIM].T.astype(BF16)
        vgt_ref[h, 0, HEAD_DIM:] = ones_rows

    gnq_s = gnq_ref[...] * hd_scale
    for h in range(NA_HEADS):
        qn_ref[h] = head_norm(C_NQ + h * HEAD_DIM, gnq_s).astype(BF16)
        kn_ref[h] = head_norm(C_NK + h * HEAD_DIM, gnk_ref[...]).astype(BF16)
        vn_ref[h] = p_ref[:, C_NV + h * HEAD_DIM:C_NV + (h + 1) * HEAD_DIM].astype(BF16)


def _mixin_kernel(x_ref, g_ref, sh_ref, sc_ref, w_ref, *rest):
    prep_args, proj = rest[:-2], rest[-2:]
    i = pl.program_id(0)

    @pl.when(i == 0)
    def _():
        proj[1][...] = jnp.zeros(proj[1].shape, F32)

    def step(dst, src):
        h = _modulated_norm(x_ref[...], g_ref[...], sc_ref[...], sh_ref[...]).astype(BF16)
        for j in range(D_IN_PAD // INPROJ_TN):
            cols = slice(j * INPROJ_TN, (j + 1) * INPROJ_TN)
            dst[:, cols] = jnp.dot(h, w_ref[:, cols], preferred_element_type=F32)
        _prep_body(src, *prep_args)

    @pl.when(i % 2 == 0)
    def _():
        step(proj[0], proj[1])

    @pl.when(i % 2 == 1)
    def _():
        step(proj[1], proj[0])


def _mixin(x, norm_g, mod, w_in_p, wq, wkv, gql, gkvl, gmq, gmk, ggq, ggk, gnq, gnk, cm, sm, ca, sa):
    n = x.shape[0]
    tm = min(MIX_TM, n)
    nb = n // tm

    def cur(i):
        return jnp.minimum(i, nb - 1)

    def prev(i):
        return jnp.maximum(i - 1, 0)

    def full(a):
        return pl.BlockSpec(a.shape, lambda i: (0,) * a.ndim)

    def rows(width):
        return pl.BlockSpec((tm, width), lambda i: (prev(i), 0))

    def heads(h, width):
        return pl.BlockSpec((h, tm, width), lambda i: (0, prev(i), 0))

    def heads_t(h):
        return pl.BlockSpec((h, 1, DV_EXT, tm), lambda i: (0, prev(i), 0, 0))

    out_shape = (
        jax.ShapeDtypeStruct((MLA_HEADS, n, MLA_QK_PAD), BF16),
        jax.ShapeDtypeStruct((MLA_HEADS, n, MLA_QK_PAD), BF16),
        jax.ShapeDtypeStruct((MLA_HEADS, nb, DV_EXT, tm), BF16),
        jax.ShapeDtypeStruct((GQA_HEADS, n, HEAD_DIM), BF16),
        jax.ShapeDtypeStruct((GQA_KV_HEADS, n, HEAD_DIM), BF16),
        jax.ShapeDtypeStruct((GQA_KV_HEADS, nb, DV_EXT, tm), BF16),
        jax.ShapeDtypeStruct((NA_HEADS, n, HEAD_DIM), BF16),
        jax.ShapeDtypeStruct((NA_HEADS, n, HEAD_DIM), BF16),
        jax.ShapeDtypeStruct((NA_HEADS, n, HEAD_DIM), BF16),
    )
    out_specs = (
        heads(MLA_HEADS, MLA_QK_PAD), heads(MLA_HEADS, MLA_QK_PAD), heads_t(MLA_HEADS),
        heads(GQA_HEADS, HEAD_DIM), heads(GQA_KV_HEADS, HEAD_DIM), heads_t(GQA_KV_HEADS),
        heads(NA_HEADS, HEAD_DIM), heads(NA_HEADS, HEAD_DIM), heads(NA_HEADS, HEAD_DIM),
    )
    small = (wq, wkv, gql, gkvl, gmq, gmk, ggq, ggk, gnq, gnk)
    return pl.pallas_call(
        _mixin_kernel,
        grid=(nb + 1,),
        in_specs=[
            pl.BlockSpec((tm, D_MODEL), lambda i: (cur(i), 0)),
            pl.BlockSpec((1, D_MODEL), lambda i: (0, 0)),
            pl.BlockSpec((1, D_MODEL), lambda i: (0, 0)),
            pl.BlockSpec((1, D_MODEL), lambda i: (0, 1)),
            pl.BlockSpec((D_MODEL, D_IN_PAD), lambda i: (0, 0), pipeline_mode=pl.Buffered(1)),
        ] + [full(a) for a in small] + [rows(LANES)] * 4,
        out_specs=out_specs,
        out_shape=out_shape,
        scratch_shapes=[pltpu.VMEM((tm, D_IN_PAD), F32), pltpu.VMEM((tm, D_IN_PAD), F32)],
        compiler_params=_cparams(1),
        name="mixin",
    )(x, norm_g, mod, mod, w_in_p, *small, cm, sm, ca, sa)


FLASH_TQ = 1024
FLASH_TK = 512
FLASH_UNROLL = 8


def _flash_kernel(q_ref, k_ref, vt_ref, o_ref, m_ref, acc_ref, s_ref, cm_ref, *, n_chunks, tk, n_sub, unroll):
    dv = acc_ref.shape[0] - (DV_EXT - HEAD_DIM)
    tsub = tk // n_sub
    m_ref[...] = jnp.full(m_ref.shape, -jnp.inf, F32)
    acc_ref[...] = jnp.zeros(acc_ref.shape, F32)

    def scores(c, slot):
        kc = k_ref[0, pl.ds(pl.multiple_of(c * tk, tk), tk), :]
        s = lax.dot_general(kc, q_ref[0], (((1,), (1,)), ((), ())), preferred_element_type=F32)
        s_ref[slot] = s
        cm_ref[slot] = jnp.max(s, axis=0, keepdims=True)

    def update(c, slot):
        m_prev = m_ref[...]
        m_new = jnp.maximum(m_prev, cm_ref[slot])
        p = jnp.exp2(s_ref[slot] - m_new).astype(BF16)
        alpha = jnp.exp2(m_prev - m_new)
        pv = jnp.dot(vt_ref[0, n_sub * c], p[:tsub], preferred_element_type=F32)
        for u in range(1, n_sub):
            pv = pv + jnp.dot(vt_ref[0, n_sub * c + u], p[u * tsub:(u + 1) * tsub], preferred_element_type=F32)
        acc_ref[...] = alpha * acc_ref[...] + pv
        m_ref[...] = m_new

    scores(0, 0)

    def body(j, carry):
        c0 = unroll * j
        for u in range(unroll):
            nxt = c0 + u + 1
            if u == unroll - 1:
                nxt = jnp.where(nxt == n_chunks, 0, nxt)
            scores(nxt, (u + 1) % 2)
            update(c0 + u, u % 2)
        return carry

    lax.fori_loop(0, n_chunks // unroll, body, 0)
    acc = acc_ref[...]
    o_ref[...] = (acc[:dv] / acc[dv:dv + 1]).T


def _flash(q, k, vt):
    H, S, dq = q.shape
    hk, n_vt, dv_ext, tsub = vt.shape
    tk = max(min(FLASH_TK, S), tsub)
    n_sub = tk // tsub
    n_chunks = S // tk
    dv = dv_ext - (DV_EXT - HEAD_DIM)
    group = H // hk
    tq = min(FLASH_TQ, S)
    unroll = FLASH_UNROLL if n_chunks % FLASH_UNROLL == 0 else 2
    assert n_chunks % unroll == 0
    return pl.pallas_call(
        functools.partial(_flash_kernel, n_chunks=n_chunks, tk=tk, n_sub=n_sub, unroll=unroll),
        grid=(H, S // tq),
        in_specs=[
            pl.BlockSpec((1, tq, dq), lambda h, i: (h, i, 0)),
            pl.BlockSpec((1, S, dq), lambda h, i: (h // group, 0, 0)),
            pl.BlockSpec((1, n_vt, dv_ext, tsub), lambda h, i: (h // group, 0, 0, 0)),
        ],
        out_specs=pl.BlockSpec((tq, dv), lambda h, i: (i, h)),
        out_shape=jax.ShapeDtypeStruct((S, H * dv), F32),
        scratch_shapes=[pltpu.VMEM((1, tq), F32), pltpu.VMEM((dv_ext, tq), F32),
                        pltpu.VMEM((2, tk, tq), F32), pltpu.VMEM((2, 1, tq), F32)],
        compiler_params=_cparams(2),
        name="flash",
    )(q, k, vt)


NEG_MIN = float(np.finfo(np.float32).min)
RPB_ROWS = 2 * NA_KH - 1
RPB_COLS = 2 * NA_KW - 1


def _na_rel(kind, i, j):
    if kind == 0:
        return j - i + NA_KH - 1, j < NA_KH
    if kind == 1:
        return j - i + NA_KH // 2 - 1, i <= j <= i + NA_KH - 1
    return j - i - 1, j >= NA_KROWS - NA_KH


def _na_bias_kernel(rpb_ref, o_ref):
    h = pl.program_id(0)
    shape = (GRID_W, LANES)
    c = lax.broadcasted_iota(I32, shape, 0)
    kc = lax.broadcasted_iota(I32, shape, 1) & (GRID_W - 1)
    cs = jnp.clip(c - NA_KW // 2, 0, GRID_W - NA_KW)
    in_win = (kc >= cs) & (kc < cs + NA_KW)
    dc = kc - c + NA_KW - 1
    lane_lo = lax.broadcasted_iota(I32, shape, 1) < GRID_W
    masked = jnp.full(shape, NEG_MIN, F32)
    cache = {}

    def toeplitz(dr):
        if dr not in cache:
            t = jnp.zeros(shape, F32)
            base = (h * RPB_ROWS + dr) * RPB_COLS
            for d in range(RPB_COLS):
                t = jnp.where(dc == d, rpb_ref[base + d], t)
            cache[dr] = jnp.where(in_win, t, NEG_MIN)
        return cache[dr]

    def half(kind, i, j):
        dr, valid = _na_rel(kind, i, j)
        return toeplitz(dr) if valid else masked

    for kind in range(3):
        for i in range(NA_QROWS):
            for jp in range(NA_KROWS // 2):
                piece = jnp.where(lane_lo, half(kind, i, 2 * jp), half(kind, i, 2 * jp + 1))
                o_ref[kind, 0, i * GRID_W:(i + 1) * GRID_W, jp * LANES:(jp + 1) * LANES] = piece


def _na_bias(rpb_all):
    n_tables = rpb_all.shape[0] * NA_HEADS
    return pl.pallas_call(
        _na_bias_kernel,
        grid=(n_tables,),
        in_specs=[pl.BlockSpec(memory_space=pltpu.SMEM)],
        out_specs=pl.BlockSpec((3, 1, NA_TQ, NA_TK), lambda h: (0, h, 0, 0)),
        out_shape=jax.ShapeDtypeStruct((3, n_tables, NA_TQ, NA_TK), F32),
        compiler_params=_cparams(1),
        name="na_bias",
    )(rpb_all.reshape(-1))


NA_SUB = 8


def _na_kernel(q_ref, k_ref, v_ref, b_ref, o_ref, *, grid_rows, sub):
    i = pl.program_id(1)
    nblk = grid_rows // NA_QROWS

    def scores(j):
        blk = i * sub + j
        kstart = jnp.clip(blk * NA_QROWS - NA_KH // 2, 0, grid_rows - NA_KROWS)
        off = pl.multiple_of(kstart * GRID_W, GRID_W)
        kind = jnp.where(blk == 0, 0, jnp.where(blk == nblk - 1, 2, 1))
        kw = k_ref[0, pl.ds(off, NA_TK), :]
        q = q_ref[0, j * NA_TQ:(j + 1) * NA_TQ, :]
        s = lax.dot_general(q, kw, (((1,), (1,)), ((), ())), preferred_element_type=F32)
        return s + b_ref[kind, 0], off

    def finish(j, s, off):
        vw = v_ref[0, pl.ds(off, NA_TK), :]
        p = jnp.exp(s - jnp.max(s, axis=-1, keepdims=True))
        l = jnp.sum(p, axis=-1, keepdims=True)
        o_ref[j * NA_TQ:(j + 1) * NA_TQ, :] = jnp.dot(p.astype(BF16), vw, preferred_element_type=F32) / l

    cur = scores(0)
    for j in range(sub):
        nxt = scores(j + 1) if j + 1 < sub else None
        finish(j, *cur)
        cur = nxt


def _na(q, k, v, bias, layer):
    H, S, d = q.shape
    grid_rows = S // GRID_W
    nblk = grid_rows // NA_QROWS
    sub = int(np.gcd(nblk, NA_SUB))
    return pl.pallas_call(
        functools.partial(_na_kernel, grid_rows=grid_rows, sub=sub),
        grid=(H, nblk // sub),
        in_specs=[
            pl.BlockSpec((1, NA_TQ * sub, d), lambda h, i: (h, i, 0)),
            pl.BlockSpec((1, S, d), lambda h, i: (h, 0, 0)),
            pl.BlockSpec((1, S, d), lambda h, i: (h, 0, 0)),
            pl.BlockSpec((3, 1, NA_TQ, NA_TK), lambda h, i: (0, layer * H + h, 0, 0)),
        ],
        out_specs=pl.BlockSpec((NA_TQ * sub, d), lambda h, i: (i, h)),
        out_shape=jax.ShapeDtypeStruct((S, H * d), F32),
        compiler_params=_cparams(2),
        name="na",
    )(q, k, v, bias)


W_MLA = MLA_HEADS * MLA_V
W_GQA = GQA_HEADS * HEAD_DIM
W_NA = NA_HEADS * HEAD_DIM


def _outproj_body(oa_ref, ob_ref, oc_ref, g_ref, wb_ref, x_ref, g1_ref):
    def gnorm(o, g):
        return (o * lax.rsqrt(jnp.mean(o * o, axis=-1, keepdims=True) + EPS) * g).astype(BF16)

    a = gnorm(oa_ref[...], g_ref[:, :W_MLA])
    b = gnorm(ob_ref[...], g_ref[:, W_MLA:W_MLA + W_GQA])
    c = gnorm(oc_ref[...], g_ref[:, W_MLA + W_GQA:])
    y = jnp.dot(a, wb_ref[:W_MLA, :], preferred_element_type=F32)
    y = y + jnp.dot(b, wb_ref[W_MLA:W_MLA + W_GQA, :], preferred_element_type=F32)
    y = y + jnp.dot(c, wb_ref[W_MLA + W_GQA:, :], preferred_element_type=F32)
    return x_ref[...] + g1_ref[...] * y


ROUTER_TM = 256


def _store_pitched(ref, val):
    rows = val.shape[0]
    for s in range(ROW_SLABS):
        ref[pl.ds(s, rows, stride=ROW_PITCH), :] = val[:, s * LANES:(s + 1) * LANES]
    for s in range(ROW_SLABS, ROW_PITCH):
        ref[pl.ds(s, rows, stride=ROW_PITCH), :] = jnp.zeros((rows, LANES), val.dtype)


def _load_pitched_slab(ref, s, rows):
    return ref[pl.ds(s, rows, stride=ROW_PITCH), :]


def _first_argmax(v, vmax, n):
    idx = lax.broadcasted_iota(I32, v.shape, 0)
    return jnp.min(jnp.where(v == vmax, idx, n), axis=0, keepdims=True)


def _router_body(x_ref, g_ref, sh_ref, sc_ref, w_ref, b_ref, u_ref,
                 h_ref, ri_ref, rw_ref, cnt_ref, carry_ref, live):
    h = _modulated_norm(x_ref[...], g_ref[...], sc_ref[...], sh_ref[...])
    _store_pitched(h_ref, h)
    def split(v):
        hi = v.astype(BF16)
        return hi, (v - hi.astype(F32)).astype(BF16)

    def nt_dot(a, b):
        return lax.dot_general(a, b, (((1,), (1,)), ((), ())), preferred_element_type=F32)

    w_hi, w_lo = split(w_ref[...])
    h_hi, h_lo = split(h)
    logits = nt_dot(w_hi, h_hi) + (nt_dot(w_hi, h_lo) + nt_dot(w_lo, h_hi)) + b_ref[...]
    tm = logits.shape[1]

    gl = logits[0:N_GROUPS]
    gmax = jnp.max(gl, axis=0, keepdims=True)
    gexp = jnp.exp(gl - gmax)
    gprob = gexp / jnp.sum(gexp, axis=0, keepdims=True)
    p_g = jnp.max(gprob, axis=0, keepdims=True)
    g_top = _first_argmax(gprob, p_g, N_GROUPS)

    e_in = jnp.zeros((EXPERTS_PER_GROUP, tm), F32)
    for g in range(N_GROUPS):
        lo = ROUTER_E0 + g * EXPERTS_PER_GROUP
        e_in = jnp.where(g_top == g, logits[lo:lo + EXPERTS_PER_GROUP], e_in)
    eexp = jnp.exp(e_in - jnp.max(e_in, axis=0, keepdims=True))
    eprob = eexp / jnp.sum(eexp, axis=0, keepdims=True)
    v1 = jnp.max(eprob, axis=0, keepdims=True)
    i1 = _first_argmax(eprob, v1, EXPERTS_PER_GROUP)
    row = lax.broadcasted_iota(I32, eprob.shape, 0)
    rest = jnp.where(row == i1, -1.0, eprob)
    v2 = jnp.max(rest, axis=0, keepdims=True)
    i2 = _first_argmax(rest, v2, EXPERTS_PER_GROUP)
    wsum = v1 + v2
    w1 = v1 / wsum * p_g
    w2 = v2 / wsum * p_g
    e1 = g_top * EXPERTS_PER_GROUP + i1
    e2 = g_top * EXPERTS_PER_GROUP + i2

    erow = lax.broadcasted_iota(I32, (N_EXPERTS, tm), 0)
    oh1 = (erow == e1).astype(F32)
    oh2 = (erow == e2).astype(F32)
    oh = oh1 + oh2
    before = jnp.dot(oh.astype(BF16), u_ref[...], preferred_element_type=F32) + carry_ref[...]
    r1 = jnp.sum(oh1 * before, axis=0, keepdims=True)
    r2 = jnp.sum(oh2 * before, axis=0, keepdims=True)
    carry_ref[...] = carry_ref[...] + live * jnp.sum(oh, axis=1, keepdims=True)
    cnt_ref[...] = jnp.broadcast_to(carry_ref[...], cnt_ref.shape)

    ri_ref[...] = jnp.zeros(ri_ref.shape, I32)
    ri_ref[0:1, :] = e1
    ri_ref[1:2, :] = e2
    ri_ref[2:3, :] = r1.astype(I32)
    ri_ref[3:4, :] = r2.astype(I32)
    rw_ref[...] = jnp.zeros(rw_ref.shape, F32)
    rw_ref[0:1, :] = w1
    rw_ref[1:2, :] = w2


def _outrouter_kernel(oa_ref, ob_ref, oc_ref, go_ref, w_ref, x_ref, g1_ref,
                      g2_ref, sh_ref, sc_ref, wr_ref, br_ref, u_ref,
                      xo_ref, h_ref, ri_ref, rw_ref, cnt_ref,
                      wb_ref, xa_ref, xb_ref, carry_ref):
    i = pl.program_id(0)

    @pl.when(i == 0)
    def _():
        wb_ref[...] = w_ref[0].astype(BF16)
        xb_ref[...] = jnp.zeros(xb_ref.shape, F32)
        carry_ref[...] = jnp.zeros(carry_ref.shape, F32)

    live = (i > 0).astype(F32)

    def step(dst, src):
        x_new = _outproj_body(oa_ref, ob_ref, oc_ref, go_ref, wb_ref, x_ref, g1_ref)
        xo_ref[...] = x_new
        dst[...] = x_new
        _router_body(src, g2_ref, sh_ref, sc_ref, wr_ref, br_ref, u_ref,
                     h_ref, ri_ref, rw_ref, cnt_ref, carry_ref, live)

    @pl.when(i % 2 == 0)
    def _():
        step(xa_ref, xb_ref)

    @pl.when(i % 2 == 1)
    def _():
        step(xb_ref, xa_ref)


def _outrouter(oa, ob, oc, gains, w_out_all, layer, x, mod, norm2_g, w_r, b_r, upper):
    n = x.shape[0]
    tm = upper.shape[0]
    nb = n // tm

    def cur(i):
        return jnp.minimum(i, nb - 1)

    def prev(i):
        return jnp.maximum(i - 1, 0)

    return pl.pallas_call(
        _outrouter_kernel,
        grid=(nb + 1,),
        in_specs=[
            pl.BlockSpec((tm, W_MLA), lambda i: (cur(i), 0)),
            pl.BlockSpec((tm, W_GQA), lambda i: (cur(i), 0)),
            pl.BlockSpec((tm, W_NA), lambda i: (cur(i), 0)),
            pl.BlockSpec((1, D_MODEL), lambda i: (0, 0)),
            pl.BlockSpec((1, D_MODEL, D_MODEL), lambda i: (layer, 0, 0), pipeline_mode=pl.Buffered(1)),
            pl.BlockSpec((tm, D_MODEL), lambda i: (cur(i), 0)),
            pl.BlockSpec((1, D_MODEL), lambda i: (0, 2)),
            pl.BlockSpec((1, D_MODEL), lambda i: (0, 0)),
            pl.BlockSpec((1, D_MODEL), lambda i: (0, 3)),
            pl.BlockSpec((1, D_MODEL), lambda i: (0, 4)),
            pl.BlockSpec((ROUTER_ROWS, D_MODEL), lambda i: (0, 0)),
            pl.BlockSpec((ROUTER_ROWS, 1), lambda i: (0, 0)),
            pl.BlockSpec((tm, tm), lambda i: (0, 0)),
        ],
        out_specs=(
            pl.BlockSpec((tm, D_MODEL), lambda i: (cur(i), 0)),
            pl.BlockSpec((tm * ROW_PITCH, LANES), lambda i: (prev(i), 0)),
            pl.BlockSpec((SUBLANES, tm), lambda i: (0, prev(i))),
            pl.BlockSpec((SUBLANES, tm), lambda i: (0, prev(i))),
            pl.BlockSpec((N_EXPERTS, LANES), lambda i: (0, 0)),
        ),
        out_shape=(
            jax.ShapeDtypeStruct((n, D_MODEL), F32),
            jax.ShapeDtypeStruct((n * ROW_PITCH, LANES), F32),
            jax.ShapeDtypeStruct((SUBLANES, n), I32),
            jax.ShapeDtypeStruct((SUBLANES, n), F32),
            jax.ShapeDtypeStruct((N_EXPERTS, LANES), F32),
        ),
        scratch_shapes=[
            pltpu.VMEM((D_MODEL, D_MODEL), BF16),
            pltpu.VMEM((tm, D_MODEL), F32),
            pltpu.VMEM((tm, D_MODEL), F32),
            pltpu.VMEM((N_EXPERTS, 1), F32),
        ],
        compiler_params=_cparams(1),
        name="outrouter",
    )(oa, ob, oc, gains, w_out_all, x, mod, norm2_g, mod, mod, w_r, b_r, upper)


GATHER_UNROLL = 8


def _row_gather_start(src_hbm, idx_ref, base, dst, sem, n_rows):
    def body(r, carry):
        off = idx_ref[base + r]
        pltpu.make_async_copy(src_hbm.at[pl.ds(off, ROW_SLABS)],
                              dst.at[pl.ds(r * ROW_PITCH, ROW_SLABS)], sem).start()
        return carry

    lax.fori_loop(0, n_rows, body, 0, unroll=GATHER_UNROLL)


def _row_gather_wait(src_hbm, dst, sem, n_rows):
    n = n_rows * ROW_SLABS
    pltpu.make_async_copy(src_hbm.at[pl.ds(0, n)], dst.at[pl.ds(0, n)], sem).wait()


BLOCK_SLAB_ROWS = MOE_ROWS * ROW_PITCH
GATHER_SLOTS = 4
GATHER_DEPTH = GATHER_SLOTS


def _expert_kernel(src_ref, bstart_ref, h_hbm, wg_ref, wu_ref, wd_ref, y_hbm,
                   xg_ref, yo_ref, wgb_ref, wub_ref, wdb_ref, gsem, osem):
    e = pl.program_id(0)
    lo, hi = bstart_ref[e], bstart_ref[e + 1]
    total = bstart_ref[N_EXPERTS]

    def out_copy(g, slot):
        row0 = pl.multiple_of(g * BLOCK_SLAB_ROWS, BLOCK_SLAB_ROWS)
        return pltpu.make_async_copy(yo_ref.at[slot], y_hbm.at[pl.ds(row0, BLOCK_SLAB_ROWS)], osem.at[slot])

    @pl.when(e == 0)
    def _():
        for d in range(GATHER_DEPTH):
            @pl.when(d < total)
            def _():
                _row_gather_start(h_hbm, src_ref, d * MOE_ROWS, xg_ref.at[d], gsem.at[d], MOE_ROWS)

    @pl.when(hi > lo)
    def _():
        wgb_ref[...] = wg_ref[0].astype(BF16)
        wub_ref[...] = wu_ref[0].astype(BF16)
        wdb_ref[...] = wd_ref[0].astype(BF16)

    def block(g, carry):
        slot = g % 2
        gslot = g % GATHER_SLOTS

        _row_gather_wait(h_hbm, xg_ref.at[gslot], gsem.at[gslot], MOE_ROWS)
        xs = xg_ref.at[gslot]
        x = jnp.concatenate(
            [_load_pitched_slab(xs, s, MOE_ROWS).astype(BF16) for s in range(ROW_SLABS)], axis=1)
        gate = jnp.dot(x, wgb_ref[...], preferred_element_type=F32)
        up = jnp.dot(x, wub_ref[...], preferred_element_type=F32)
        a = (gate * (1.0 / (1.0 + jnp.exp(-gate))) * up).astype(BF16)
        y = jnp.dot(a, wdb_ref[...], preferred_element_type=F32)

        @pl.when(g + GATHER_DEPTH < total)
        def _():
            nslot = (g + GATHER_DEPTH) % GATHER_SLOTS
            _row_gather_start(h_hbm, src_ref, (g + GATHER_DEPTH) * MOE_ROWS, xg_ref.at[nslot], gsem.at[nslot],
                              MOE_ROWS)

        @pl.when(g >= 2)
        def _():
            out_copy(g - 2, slot).wait()

        _store_pitched(yo_ref.at[slot], y)
        out_copy(g, slot).start()
        return carry

    lax.fori_loop(lo, hi, block, 0)

    @pl.when(e == N_EXPERTS - 1)
    def _():
        @pl.when(total >= 2)
        def _():
            out_copy(total - 2, total % 2).wait()

        @pl.when(total >= 1)
        def _():
            out_copy(total - 1, (total - 1) % 2).wait()

        n_blocks = y_hbm.shape[0] // BLOCK_SLAB_ROWS
        yo_ref[0] = jnp.zeros(yo_ref.shape[1:], F32)

        def zero_start(g, carry):
            out_copy(g, 0).start()
            return carry

        def zero_wait(g, carry):
            out_copy(g, 0).wait()
            return carry

        lax.fori_loop(total, n_blocks, zero_start, 0)
        lax.fori_loop(total, n_blocks, zero_wait, 0)


def _experts(src_off, bstart, h, w_gate, w_up, w_down, layer, n_blocks):
    base = layer * N_EXPERTS
    grid_spec = pltpu.PrefetchScalarGridSpec(
        num_scalar_prefetch=2,
        grid=(N_EXPERTS,),
        in_specs=[
            pl.BlockSpec(memory_space=pl.ANY),
            pl.BlockSpec((1, D_MODEL, D_EXPERT), lambda e, src, bs: (base + e, 0, 0)),
            pl.BlockSpec((1, D_MODEL, D_EXPERT), lambda e, src, bs: (base + e, 0, 0)),
            pl.BlockSpec((1, D_EXPERT, D_MODEL), lambda e, src, bs: (base + e, 0, 0)),
        ],
        out_specs=pl.BlockSpec(memory_space=pl.ANY),
        scratch_shapes=[
            pltpu.VMEM((GATHER_SLOTS, BLOCK_SLAB_ROWS, LANES), F32),
            pltpu.VMEM((2, BLOCK_SLAB_ROWS, LANES), F32),
            pltpu.VMEM((D_MODEL, D_EXPERT), BF16),
            pltpu.VMEM((D_MODEL, D_EXPERT), BF16),
            pltpu.VMEM((D_EXPERT, D_MODEL), BF16),
            pltpu.SemaphoreType.DMA((GATHER_SLOTS,)),
            pltpu.SemaphoreType.DMA((2,)),
        ],
    )
    return pl.pallas_call(
        _expert_kernel,
        grid_spec=grid_spec,
        out_shape=jax.ShapeDtypeStruct((n_blocks * BLOCK_SLAB_ROWS, LANES), F32),
        compiler_params=_cparams(1),
        name="experts",
    )(src_off, bstart, h, w_gate, w_up, w_down)


COMB_TM = 256


def _combine_kernel(d1_ref, d2_ref, y_hbm, x_ref, w1_ref, w2_ref, g2_ref, o_ref, ya_ref, yb_ref, sem):
    i = pl.program_id(0)
    n = pl.num_programs(0)
    slot = i % 2

    def start(blk, s):
        _row_gather_start(y_hbm, d1_ref, blk * COMB_TM, ya_ref.at[s], sem.at[0, s], COMB_TM)
        _row_gather_start(y_hbm, d2_ref, blk * COMB_TM, yb_ref.at[s], sem.at[1, s], COMB_TM)

    @pl.when(i == 0)
    def _():
        start(0, 0)

    @pl.when(i + 1 < n)
    def _():
        start(i + 1, 1 - slot)

    _row_gather_wait(y_hbm, ya_ref.at[slot], sem.at[0, slot], COMB_TM)
    _row_gather_wait(y_hbm, yb_ref.at[slot], sem.at[1, slot], COMB_TM)
    w1, w2 = w1_ref[...], w2_ref[...]
    ya, yb = ya_ref.at[slot], yb_ref.at[slot]
    for s in range(ROW_SLABS):
        cols = slice(s * LANES, (s + 1) * LANES)
        y = w1 * _load_pitched_slab(ya, s, COMB_TM) + w2 * _load_pitched_slab(yb, s, COMB_TM)
        o_ref[:, cols] = x_ref[:, cols] + g2_ref[:, cols] * y


def _combine(dest1, dest2, ybuf, x, w1, w2, mod):
    n = x.shape[0]
    grid_spec = pltpu.PrefetchScalarGridSpec(
        num_scalar_prefetch=2,
        grid=(n // COMB_TM,),
        in_specs=[
            pl.BlockSpec(memory_space=pl.ANY),
            pl.BlockSpec((COMB_TM, D_MODEL), lambda i, d1, d2: (i, 0)),
            pl.BlockSpec((COMB_TM, 1), lambda i, d1, d2: (i, 0)),
            pl.BlockSpec((COMB_TM, 1), lambda i, d1, d2: (i, 0)),
            pl.BlockSpec((1, D_MODEL), lambda i, d1, d2: (0, 5)),
        ],
        out_specs=pl.BlockSpec((COMB_TM, D_MODEL), lambda i, d1, d2: (i, 0)),
        scratch_shapes=[
            pltpu.VMEM((2, COMB_TM * ROW_PITCH, LANES), F32),
            pltpu.VMEM((2, COMB_TM * ROW_PITCH, LANES), F32),
            pltpu.SemaphoreType.DMA((2, 2)),
        ],
    )
    return pl.pallas_call(
        _combine_kernel,
        grid_spec=grid_spec,
        out_shape=jax.ShapeDtypeStruct((n, D_MODEL), F32),
        compiler_params=_cparams(1),
        name="combine",
    )(dest1, dest2, ybuf, x, w1, w2, mod)


def _rope_angles(pos, d):
    f32 = np.float32
    inv = f32(ROPE_THETA) ** (-np.arange(0, d, 2, dtype=f32) / f32(d))
    return pos.astype(f32)[:, None] * inv[None, :]


def _rope_tables(S):
    t = np.arange(S)
    z = np.zeros((S, MLA_ROPE), np.float32)
    a = _rope_angles(t, MLA_ROPE)
    cm = np.concatenate([np.cos(a), np.cos(a), z], axis=1)
    sm = np.concatenate([-np.sin(a), np.sin(a), z], axis=1)
    ar = _rope_angles(t // GRID_W, HEAD_DIM // 2)
    ac = _rope_angles(t % GRID_W, HEAD_DIM // 2)
    ca = np.concatenate([np.cos(ar), np.cos(ar), np.cos(ac), np.cos(ac)], axis=1)
    sa = np.concatenate([-np.sin(ar), np.sin(ar), -np.sin(ac), np.sin(ac)], axis=1)
    return tuple(jnp.asarray(v, F32) for v in (cm, sm, ca, sa))


def _pad_lanes(a, width):
    return jnp.concatenate([a, jnp.zeros(a.shape[:-1] + (width - a.shape[-1],), a.dtype)], axis=-1)


def _layer_weights(w_in, w_uq, w_ukv, qk_q_g, qk_k_g):
    split = C_APE + MLA_ROPE
    w_in_p = jnp.concatenate(
        [w_in[:, :split], jnp.zeros((D_MODEL, PE_PAD - MLA_ROPE), F32), w_in[:, split:]], axis=1
    ).astype(BF16)
    wq = w_uq.reshape(MLA_Q_RANK, MLA_HEADS, MLA_QK)
    wq_n = wq[:, :, :MLA_NOPE].reshape(MLA_Q_RANK, MLA_HEADS * LANES)
    wq_r = _pad_lanes(wq[:, :, MLA_NOPE:], LANES).reshape(MLA_Q_RANK, MLA_HEADS * LANES)
    wq_p = jnp.concatenate([wq_n, wq_r], axis=1).astype(BF16)
    wkv = w_ukv.reshape(MLA_KV_RANK, MLA_HEADS, MLA_NOPE + MLA_V)
    wkv_p = jnp.concatenate(
        [wkv[:, :, :MLA_NOPE].reshape(MLA_KV_RANK, -1), wkv[:, :, MLA_NOPE:].reshape(MLA_KV_RANK, -1)],
        axis=1).astype(BF16)
    gmq = _pad_lanes(qk_q_g[None, :], MLA_QK_PAD)
    gmk = _pad_lanes(qk_k_g[None, :], MLA_QK_PAD)
    return w_in_p, wq_p, wkv_p, gmq, gmk


def _slot_kernel(start_ref, ri_ref, o_ref):
    e = ri_ref[0:2, :]
    first = jnp.zeros(e.shape, I32)
    for k in range(N_EXPERTS):
        first = jnp.where(e == k, start_ref[k], first)
    slot = first + ri_ref[2:4, :]
    o_ref[...] = jnp.zeros(o_ref.shape, I32)
    o_ref[0:2, :] = slot
    o_ref[2:4, :] = slot * ROW_PITCH


def _slots(pad_start, ri):
    return pl.pallas_call(
        _slot_kernel,
        grid=(1,),
        in_specs=[pl.BlockSpec(memory_space=pltpu.SMEM), pl.BlockSpec(ri.shape, lambda i: (0, 0))],
        out_specs=pl.BlockSpec(ri.shape, lambda i: (0, 0)),
        out_shape=jax.ShapeDtypeStruct(ri.shape, I32),
        compiler_params=_cparams(1),
        name="slots",
    )(pad_start, ri)


def _routing_plan(ri, cnt, n_blocks):
    n = ri.shape[1]
    counts = cnt[:, 0].astype(I32)
    padded = ((counts + MOE_ROWS - 1) // MOE_ROWS) * MOE_ROWS
    pad_end = jnp.cumsum(padded)
    pad_start = pad_end - padded
    slots = _slots(pad_start, ri)
    dest1, dest2 = slots[0], slots[1]
    tok_off = jnp.arange(n, dtype=I32) * ROW_PITCH
    pad_src = (jnp.arange(n_blocks * MOE_ROWS, dtype=I32) % n) * ROW_PITCH
    src_off = pad_src.at[jnp.concatenate([dest1, dest2])].set(
        jnp.concatenate([tok_off, tok_off]), unique_indices=True)
    bstart = jnp.concatenate([jnp.zeros((1,), I32), pad_end // MOE_ROWS]).astype(I32)
    return slots[2], slots[3], src_off, bstart


def kernel(x, c, ada_w, ada_b, norm1_g, norm2_g, w_in, mla_q_norm_g, mla_w_uq, mla_kv_norm_g, mla_w_ukv, mla_qk_q_g, mla_qk_k_g, gqa_q_g, gqa_k_g, na_q_g, na_k_g, na_rpb, mix_out_norm_g, w_out, router_group_w, router_group_b, router_expert_w, router_expert_b, expert_w_gate, expert_w_up, expert_w_down):
    B, S, D = x.shape
    assert B == 1 and D == D_MODEL and S % (GRID_W * NA_QROWS) == 0 and S // GRID_W >= NA_KROWS
    depth = ada_w.shape[0]
    n = B * S
    xt = x.reshape(n, D)

    mod_all = _adaln(c, ada_w, ada_b)
    na_bias = _na_bias(na_rpb)
    cm, sm, ca, sa = _rope_tables(S)
    tm_r = min(ROUTER_TM, n)
    upper = jnp.asarray(np.arange(tm_r)[:, None] < np.arange(tm_r)[None, :], BF16)
    n_blocks = -(-(n * 2 + N_EXPERTS * (MOE_ROWS - 1)) // MOE_ROWS)
    wg_all = expert_w_gate.reshape(depth * N_EXPERTS, D_MODEL, D_EXPERT)
    wu_all = expert_w_up.reshape(depth * N_EXPERTS, D_MODEL, D_EXPERT)
    wd_all = expert_w_down.reshape(depth * N_EXPERTS, D_EXPERT, D_MODEL)

    def row(v):
        return v[None, :]

    for l in range(depth):
        mod = mod_all[l]
        w_in_p, wq_p, wkv_p, gmq, gmk = _layer_weights(
            w_in[l], mla_w_uq[l], mla_w_ukv[l], mla_qk_q_g[l], mla_qk_k_g[l])

        qm, km, vmt, qg, kg, vgt, qn, kn, vn = _mixin(
            xt, row(norm1_g[l]), mod, w_in_p, wq_p, wkv_p, row(mla_q_norm_g[l]), row(mla_kv_norm_g[l]), gmq, gmk,
            row(gqa_q_g[l]), row(gqa_k_g[l]), row(na_q_g[l]), row(na_k_g[l]), cm, sm, ca, sa)
        o_a = _flash(qm, km, vmt)
        o_b = _flash(qg, kg, vgt)
        o_c = _na(qn, kn, vn, na_bias, l)
        w_r = jnp.concatenate(
            [router_group_w[l].T, jnp.zeros((ROUTER_E0 - N_GROUPS, D), F32), router_expert_w[l].T], axis=0)
        b_r = jnp.concatenate(
            [router_group_b[l], jnp.zeros((ROUTER_E0 - N_GROUPS,), F32), router_expert_b[l]])[:, None]
        xt, h2, ri, rw, cnt = _outrouter(o_a, o_b, o_c, row(mix_out_norm_g[l]), w_out, l, xt, mod,
                                         row(norm2_g[l]), w_r, b_r, upper)
        off1, off2, src_off, bstart = _routing_plan(ri, cnt, n_blocks)
        ybuf = _experts(src_off, bstart, h2, wg_all, wu_all, wd_all, l, n_blocks)
        xt = _combine(off1, off2, ybuf, xt, rw[0][:, None], rw[1][:, None], mod)

    return xt.reshape(B, S, D)
```

```python
import functools

import jax
import jax.numpy as jnp
import numpy as np
from jax import lax
from jax.experimental import pallas as pl
from jax.experimental.pallas import tpu as pltpu

F32 = jnp.float32
BF16 = jnp.bfloat16
I32 = jnp.int32

D_MODEL = 2048
GRID_W = 64
HEAD_DIM = 128
ROPE_THETA = 10000.0
EPS = 1e-6
MLA_HEADS = 6
MLA_Q_RANK = 384
MLA_KV_RANK = 256
MLA_NOPE = 128
MLA_ROPE = 64
MLA_V = 128
MLA_QK = MLA_NOPE + MLA_ROPE
GQA_HEADS = 4
GQA_KV_HEADS = 2
NA_HEADS = 6
NA_KH = 8
NA_KW = 16
N_GROUPS = 4
EXPERTS_PER_GROUP = 8
N_EXPERTS = N_GROUPS * EXPERTS_PER_GROUP
D_EXPERT = 512

LANES = 128
SUBLANES = 8
VMEM_LIMIT_BYTES = 56 * 1024 * 1024

PE_PAD = LANES
C_AQ = 0
C_AKV = C_AQ + MLA_Q_RANK
C_APE = C_AKV + MLA_KV_RANK
C_GQ = C_APE + PE_PAD
C_GK = C_GQ + GQA_HEADS * HEAD_DIM
C_GV = C_GK + GQA_KV_HEADS * HEAD_DIM
C_NQ = C_GV + GQA_KV_HEADS * HEAD_DIM
C_NK = C_NQ + NA_HEADS * HEAD_DIM
C_NV = C_NK + NA_HEADS * HEAD_DIM
D_IN_PAD = C_NV + NA_HEADS * HEAD_DIM
MLA_QK_PAD = 2 * LANES

NA_QROWS = 4
NA_KROWS = 12
NA_TQ = NA_QROWS * GRID_W
NA_TK = NA_KROWS * GRID_W

LOG2E = float(np.log2(np.e))
DV_EXT = HEAD_DIM + 16

MOE_ROWS = 256
ROW_SLABS = D_MODEL // LANES
ROW_PITCH = 20
ROUTER_ROWS = 40
ROUTER_E0 = 8


def _cparams(n_axes):
    return pltpu.CompilerParams(
        dimension_semantics=("arbitrary",) * n_axes,
        vmem_limit_bytes=VMEM_LIMIT_BYTES,
    )


ADA_TN = 1024
ADA_RC = 256


def _adaln_kernel(c_ref, w_ref, b_ref, o_ref):
    def body(i, acc):
        r0 = pl.multiple_of(i * ADA_RC, ADA_RC)
        c = c_ref[pl.ds(r0, ADA_RC), :]
        cond = c * (1.0 / (1.0 + jnp.exp(-c)))
        w = w_ref[0, pl.ds(r0, ADA_RC), :]
        return acc + jnp.sum(w * cond, axis=0, keepdims=True)

    acc = lax.fori_loop(0, D_MODEL // ADA_RC, body, jnp.zeros((1, ADA_TN), F32))
    o_ref[0] = acc + b_ref[0]


def _adaln(c, ada_w, ada_b):
    L = ada_w.shape[0]
    n_out = ada_w.shape[2]
    return pl.pallas_call(
        _adaln_kernel,
        grid=(L, n_out // ADA_TN),
        in_specs=[
            pl.BlockSpec((D_MODEL, 1), lambda l, j: (0, 0)),
            pl.BlockSpec((1, D_MODEL, ADA_TN), lambda l, j: (l, 0, j)),
            pl.BlockSpec((1, 1, ADA_TN), lambda l, j: (l, 0, j)),
        ],
        out_specs=pl.BlockSpec((1, 1, ADA_TN), lambda l, j: (l, 0, j)),
        out_shape=jax.ShapeDtypeStruct((L, 1, n_out), F32),
        compiler_params=_cparams(2),
        name="adaln",
    )(c.reshape(D_MODEL, 1), ada_w, ada_b.reshape(L, 1, n_out))


def _modulated_norm(x, g, sc, sh):
    ms = jnp.mean(x * x, axis=-1, keepdims=True)
    y = x * lax.rsqrt(ms + EPS) * g
    return y * (1.0 + sc) + sh


MIX_TM = 256
INPROJ_TN = 1024


def _rot_swap(x):
    lane = lax.broadcasted_iota(I32, x.shape, 1)
    fwd = pltpu.roll(x, LANES - 32, 1)
    bwd = pltpu.roll(x, 32, 1)
    return jnp.where((lane & 63) < 32, fwd, bwd)


def _rope_tile(x, cos, sin_signed):
    return x * cos + _rot_swap(x) * sin_signed


def _sumsq(x):
    return jnp.sum(x * x, axis=-1, keepdims=True)


def _prep_body(p_ref, wq_ref, wkv_ref, gql_ref, gkvl_ref, gmq_ref, gmk_ref, ggq_ref, ggk_ref,
                 gnq_ref, gnk_ref, cm_ref, sm_ref, ca_ref, sa_ref,
                 qm_ref, km_ref, vmt_ref, qg_ref, kg_ref, vgt_ref, qn_ref, kn_ref, vn_ref):
    mla_scale = MLA_QK ** -0.5 * LOG2E
    gqa_scale = HEAD_DIM ** -0.5 * LOG2E
    hd_scale = HEAD_DIM ** -0.5
    tm = p_ref.shape[0]
    ones_rows = (lax.broadcasted_iota(I32, (DV_EXT - HEAD_DIM, tm), 0) == 0).astype(BF16)
    cm, sm = cm_ref[...], sm_ref[...]
    ca, sa = ca_ref[...], sa_ref[...]

    a_q = p_ref[:, C_AQ:C_AQ + MLA_Q_RANK]
    a_q = a_q * lax.rsqrt(_sumsq(a_q) / MLA_Q_RANK + EPS) * gql_ref[...]
    q = jnp.dot(a_q.astype(BF16), wq_ref[...], preferred_element_type=F32)
    gq_n, gq_r = gmq_ref[:, :LANES] * mla_scale, gmq_ref[:, LANES:] * mla_scale
    for h in range(MLA_HEADS):
        nope = q[:, h * LANES:(h + 1) * LANES]
        ropp = q[:, (MLA_HEADS + h) * LANES:(MLA_HEADS + h + 1) * LANES]
        rinv = lax.rsqrt((_sumsq(nope) + _sumsq(ropp)) / MLA_QK + EPS)
        qm_ref[h, :, :LANES] = (nope * rinv * gq_n).astype(BF16)
        qm_ref[h, :, LANES:] = _rope_tile(ropp * rinv * gq_r, cm, sm).astype(BF16)

    a_kv = p_ref[:, C_AKV:C_AKV + MLA_KV_RANK]
    a_kv = a_kv * lax.rsqrt(_sumsq(a_kv) / MLA_KV_RANK + EPS) * gkvl_ref[...]
    kv = jnp.dot(a_kv.astype(BF16), wkv_ref[...], preferred_element_type=F32)
    pe = p_ref[:, C_APE:C_APE + PE_PAD]
    ss_pe = _sumsq(pe)
    gk_n, gk_r = gmk_ref[:, :LANES], gmk_ref[:, LANES:]
    pe_rot = _rope_tile(pe * gk_r, cm, sm)
    for h in range(MLA_HEADS):
        nope = kv[:, h * LANES:(h + 1) * LANES]
        v = kv[:, (MLA_HEADS + h) * LANES:(MLA_HEADS + h + 1) * LANES]
        rinv = lax.rsqrt((_sumsq(nope) + ss_pe) / MLA_QK + EPS)
        km_ref[h, :, :LANES] = (nope * rinv * gk_n).astype(BF16)
        km_ref[h, :, LANES:] = (pe_rot * rinv).astype(BF16)
        vmt_ref[h, 0, :HEAD_DIM] = v.T.astype(BF16)
        vmt_ref[h, 0, HEAD_DIM:] = ones_rows

    def head_norm(col, g):
        t = p_ref[:, col:col + HEAD_DIM]
        return t * lax.rsqrt(_sumsq(t) / HEAD_DIM + EPS) * g

    ggq_s = ggq_ref[...] * gqa_scale
    for h in range(GQA_HEADS):
        qg_ref[h] = _rope_tile(head_norm(C_GQ + h * HEAD_DIM, ggq_s), ca, sa).astype(BF16)
    for h in range(GQA_KV_HEADS):
        t = head_norm(C_GK + h * HEAD_DIM, ggk_ref[...])
        kg_ref[h] = _rope_tile(t, ca, sa).astype(BF16)
        vgt_ref[h, 0, :HEAD_DIM] = p_ref[:, C_GV + h * HEAD_DIM:C_GV + (h + 1) * HEAD_DIM].T.astype(BF16)
        vgt_ref[h, 0, HEAD_DIM:] = ones_rows

    gnq_s = gnq_ref[...] * hd_scale
    for h in range(NA_HEADS):
        qn_ref[h] = head_norm(C_NQ + h * HEAD_DIM, gnq_s).astype(BF16)
        kn_ref[h] = head_norm(C_NK + h * HEAD_DIM, gnk_ref[...]).astype(BF16)
        vn_ref[h] = p_ref[:, C_NV + h * HEAD_DIM:C_NV + (h + 1) * HEAD_DIM].astype(BF16)


def _mixin_kernel(x_ref, g_ref, sh_ref, sc_ref, w_ref, *rest):
    prep_args, proj = rest[:-2], rest[-2:]
    i = pl.program_id(0)

    @pl.when(i == 0)
    def _():
        proj[1][...] = jnp.zeros(proj[1].shape, F32)

    def step(dst, src):
        h = _modulated_norm(x_ref[...], g_ref[...], sc_ref[...], sh_ref[...]).astype(BF16)
        for j in range(D_IN_PAD // INPROJ_TN):
            cols = slice(j * INPROJ_TN, (j + 1) * INPROJ_TN)
            dst[:, cols] = jnp.dot(h, w_ref[:, cols], preferred_element_type=F32)
        _prep_body(src, *prep_args)

    @pl.when(i % 2 == 0)
    def _():
        step(proj[0], proj[1])

    @pl.when(i % 2 == 1)
    def _():
        step(proj[1], proj[0])


def _mixin(x, norm_g, mod, w_in_p, wq, wkv, gql, gkvl, gmq, gmk, ggq, ggk, gnq, gnk, cm, sm, ca, sa):
    n = x.shape[0]
    tm = min(MIX_TM, n)
    nb = n // tm

    def cur(i):
        return jnp.minimum(i, nb - 1)

    def prev(i):
        return jnp.maximum(i - 1, 0)

    def full(a):
        return pl.BlockSpec(a.shape, lambda i: (0,) * a.ndim)

    def rows(width):
        return pl.BlockSpec((tm, width), lambda i: (prev(i), 0))

    def heads(h, width):
        return pl.BlockSpec((h, tm, width), lambda i: (0, prev(i), 0))

    def heads_t(h):
        return pl.BlockSpec((h, 1, DV_EXT, tm), lambda i: (0, prev(i), 0, 0))

    out_shape = (
        jax.ShapeDtypeStruct((MLA_HEADS, n, MLA_QK_PAD), BF16),
        jax.ShapeDtypeStruct((MLA_HEADS, n, MLA_QK_PAD), BF16),
        jax.ShapeDtypeStruct((MLA_HEADS, nb, DV_EXT, tm), BF16),
        jax.ShapeDtypeStruct((GQA_HEADS, n, HEAD_DIM), BF16),
        jax.ShapeDtypeStruct((GQA_KV_HEADS, n, HEAD_DIM), BF16),
        jax.ShapeDtypeStruct((GQA_KV_HEADS, nb, DV_EXT, tm), BF16),
        jax.ShapeDtypeStruct((NA_HEADS, n, HEAD_DIM), BF16),
        jax.ShapeDtypeStruct((NA_HEADS, n, HEAD_DIM), BF16),
        jax.ShapeDtypeStruct((NA_HEADS, n, HEAD_DIM), BF16),
    )
    out_specs = (
        heads(MLA_HEADS, MLA_QK_PAD), heads(MLA_HEADS, MLA_QK_PAD), heads_t(MLA_HEADS),
        heads(GQA_HEADS, HEAD_DIM), heads(GQA_KV_HEADS, HEAD_DIM), heads_t(GQA_KV_HEADS),
        heads(NA_HEADS, HEAD_DIM), heads(NA_HEADS, HEAD_DIM), heads(NA_HEADS, HEAD_DIM),
    )
    small = (wq, wkv, gql, gkvl, gmq, gmk, ggq, ggk, gnq, gnk)
    return pl.pallas_call(
        _mixin_kernel,
        grid=(nb + 1,),
        in_specs=[
            pl.BlockSpec((tm, D_MODEL), lambda i: (cur(i), 0)),
            pl.BlockSpec((1, D_MODEL), lambda i: (0, 0)),
            pl.BlockSpec((1, D_MODEL), lambda i: (0, 0)),
            pl.BlockSpec((1, D_MODEL), lambda i: (0, 1)),
            pl.BlockSpec((D_MODEL, D_IN_PAD), lambda i: (0, 0), pipeline_mode=pl.Buffered(1)),
        ] + [full(a) for a in small] + [rows(LANES)] * 4,
        out_specs=out_specs,
        out_shape=out_shape,
        scratch_shapes=[pltpu.VMEM((tm, D_IN_PAD), F32), pltpu.VMEM((tm, D_IN_PAD), F32)],
        compiler_params=_cparams(1),
        name="mixin",
    )(x, norm_g, mod, mod, w_in_p, *small, cm, sm, ca, sa)


FLASH_TQ = 1024
FLASH_TK = 512
FLASH_UNROLL = 8


def _flash_kernel(q_ref, k_ref, vt_ref, o_ref, m_ref, acc_ref, s_ref, cm_ref, *, n_chunks, tk, n_sub, unroll):
    dv = acc_ref.shape[0] - (DV_EXT - HEAD_DIM)
    tsub = tk // n_sub
    m_ref[...] = jnp.full(m_ref.shape, -jnp.inf, F32)
    acc_ref[...] = jnp.zeros(acc_ref.shape, F32)

    def scores(c, slot):
        kc = k_ref[0, pl.ds(pl.multiple_of(c * tk, tk), tk), :]
        s = lax.dot_general(kc, q_ref[0], (((1,), (1,)), ((), ())), preferred_element_type=F32)
        s_ref[slot] = s
        cm_ref[slot] = jnp.max(s, axis=0, keepdims=True)

    def update(c, slot):
        m_prev = m_ref[...]
        m_new = jnp.maximum(m_prev, cm_ref[slot])
        p = jnp.exp2(s_ref[slot] - m_new).astype(BF16)
        alpha = jnp.exp2(m_prev - m_new)
        pv = jnp.dot(vt_ref[0, n_sub * c], p[:tsub], preferred_element_type=F32)
        for u in range(1, n_sub):
            pv = pv + jnp.dot(vt_ref[0, n_sub * c + u], p[u * tsub:(u + 1) * tsub], preferred_element_type=F32)
        acc_ref[...] = alpha * acc_ref[...] + pv
        m_ref[...] = m_new

    scores(0, 0)

    def body(j, carry):
        c0 = unroll * j
        for u in range(unroll):
            nxt = c0 + u + 1
            if u == unroll - 1:
                nxt = jnp.where(nxt == n_chunks, 0, nxt)
            scores(nxt, (u + 1) % 2)
            update(c0 + u, u % 2)
        return carry

    lax.fori_loop(0, n_chunks // unroll, body, 0)
    acc = acc_ref[...]
    o_ref[...] = (acc[:dv] / acc[dv:dv + 1]).T


def _flash(q, k, vt):
    H, S, dq = q.shape
    hk, n_vt, dv_ext, tsub = vt.shape
    tk = max(min(FLASH_TK, S), tsub)
    n_sub = tk // tsub
    n_chunks = S // tk
    dv = dv_ext - (DV_EXT - HEAD_DIM)
    group = H // hk
    tq = min(FLASH_TQ, S)
    unroll = FLASH_UNROLL if n_chunks % FLASH_UNROLL == 0 else 2
    assert n_chunks % unroll == 0
    return pl.pallas_call(
        functools.partial(_flash_kernel, n_chunks=n_chunks, tk=tk, n_sub=n_sub, unroll=unroll),
        grid=(H, S // tq),
        in_specs=[
            pl.BlockSpec((1, tq, dq), lambda h, i: (h, i, 0)),
            pl.BlockSpec((1, S, dq), lambda h, i: (h // group, 0, 0)),
            pl.BlockSpec((1, n_vt, dv_ext, tsub), lambda h, i: (h // group, 0, 0, 0)),
        ],
        out_specs=pl.BlockSpec((tq, dv), lambda h, i: (i, h)),
        out_shape=jax.ShapeDtypeStruct((S, H * dv), F32),
        scratch_shapes=[pltpu.VMEM((1, tq), F32), pltpu.VMEM((dv_ext, tq), F32),
                        pltpu.VMEM((2, tk, tq), F32), pltpu.VMEM((2, 1, tq), F32)],
        compiler_params=_cparams(2),
        name="flash",
    )(q, k, vt)


NEG_MIN = float(np.finfo(np.float32).min)
RPB_ROWS = 2 * NA_KH - 1
RPB_COLS = 2 * NA_KW - 1


def _na_rel(kind, i, j):
    if kind == 0:
        return j - i + NA_KH - 1, j < NA_KH
    if kind == 1:
        return j - i + NA_KH // 2 - 1, i <= j <= i + NA_KH - 1
    return j - i - 1, j >= NA_KROWS - NA_KH


def _na_bias_kernel(rpb_ref, o_ref):
    h = pl.program_id(0)
    shape = (GRID_W, LANES)
    c = lax.broadcasted_iota(I32, shape, 0)
    kc = lax.broadcasted_iota(I32, shape, 1) & (GRID_W - 1)
    cs = jnp.clip(c - NA_KW // 2, 0, GRID_W - NA_KW)
    in_win = (kc >= cs) & (kc < cs + NA_KW)
    dc = kc - c + NA_KW - 1
    lane_lo = lax.broadcasted_iota(I32, shape, 1) < GRID_W
    masked = jnp.full(shape, NEG_MIN, F32)
    cache = {}

    def toeplitz(dr):
        if dr not in cache:
            t = jnp.zeros(shape, F32)
            base = (h * RPB_ROWS + dr) * RPB_COLS
            for d in range(RPB_COLS):
                t = jnp.where(dc == d, rpb_ref[base + d], t)
            cache[dr] = jnp.where(in_win, t, NEG_MIN)
        return cache[dr]

    def half(kind, i, j):
        dr, valid = _na_rel(kind, i, j)
        return toeplitz(dr) if valid else masked

    for kind in range(3):
        for i in range(NA_QROWS):
            for jp in range(NA_KROWS // 2):
                piece = jnp.where(lane_lo, half(kind, i, 2 * jp), half(kind, i, 2 * jp + 1))
                o_ref[kind, 0, i * GRID_W:(i + 1) * GRID_W, jp * LANES:(jp + 1) * LANES] = piece


def _na_bias(rpb_all):
    n_tables = rpb_all.shape[0] * NA_HEADS
    return pl.pallas_call(
        _na_bias_kernel,
        grid=(n_tables,),
        in_specs=[pl.BlockSpec(memory_space=pltpu.SMEM)],
        out_specs=pl.BlockSpec((3, 1, NA_TQ, NA_TK), lambda h: (0, h, 0, 0)),
        out_shape=jax.ShapeDtypeStruct((3, n_tables, NA_TQ, NA_TK), F32),
        compiler_params=_cparams(1),
        name="na_bias",
    )(rpb_all.reshape(-1))


NA_SUB = 16


def _na_kernel(q_ref, k_ref, v_ref, b_ref, o_ref, *, grid_rows, sub):
    i = pl.program_id(1)
    nblk = grid_rows // NA_QROWS

    def scores(j):
        blk = i * sub + j
        kstart = jnp.clip(blk * NA_QROWS - NA_KH // 2, 0, grid_rows - NA_KROWS)
        off = pl.multiple_of(kstart * GRID_W, GRID_W)
        kind = jnp.where(blk == 0, 0, jnp.where(blk == nblk - 1, 2, 1))
        kw = k_ref[0, pl.ds(off, NA_TK), :]
        q = q_ref[0, j * NA_TQ:(j + 1) * NA_TQ, :]
        s = lax.dot_general(q, kw, (((1,), (1,)), ((), ())), preferred_element_type=F32)
        return s + b_ref[kind, 0], off

    def finish(j, s, off):
        vw = v_ref[0, pl.ds(off, NA_TK), :]
        p = jnp.exp(s - jnp.max(s, axis=-1, keepdims=True))
        l = jnp.sum(p, axis=-1, keepdims=True)
        o_ref[j * NA_TQ:(j + 1) * NA_TQ, :] = jnp.dot(p.astype(BF16), vw, preferred_element_type=F32) / l

    cur = scores(0)
    for j in range(sub):
        nxt = scores(j + 1) if j + 1 < sub else None
        finish(j, *cur)
        cur = nxt


def _na(q, k, v, bias, layer):
    H, S, d = q.shape
    grid_rows = S // GRID_W
    nblk = grid_rows // NA_QROWS
    sub = int(np.gcd(nblk, NA_SUB))
    return pl.pallas_call(
        functools.partial(_na_kernel, grid_rows=grid_rows, sub=sub),
        grid=(H, nblk // sub),
        in_specs=[
            pl.BlockSpec((1, NA_TQ * sub, d), lambda h, i: (h, i, 0)),
            pl.BlockSpec((1, S, d), lambda h, i: (h, 0, 0)),
            pl.BlockSpec((1, S, d), lambda h, i: (h, 0, 0)),
            pl.BlockSpec((3, 1, NA_TQ, NA_TK), lambda h, i: (0, layer * H + h, 0, 0)),
        ],
        out_specs=pl.BlockSpec((NA_TQ * sub, d), lambda h, i: (i, h)),
        out_shape=jax.ShapeDtypeStruct((S, H * d), F32),
        compiler_params=_cparams(2),
        name="na",
    )(q, k, v, bias)


W_MLA = MLA_HEADS * MLA_V
W_GQA = GQA_HEADS * HEAD_DIM
W_NA = NA_HEADS * HEAD_DIM


def _outproj_body(oa_ref, ob_ref, oc_ref, g_ref, wb_ref, x_ref, g1_ref):
    def gnorm(o, g):
        return (o * lax.rsqrt(jnp.mean(o * o, axis=-1, keepdims=True) + EPS) * g).astype(BF16)

    a = gnorm(oa_ref[...], g_ref[:, :W_MLA])
    b = gnorm(ob_ref[...], g_ref[:, W_MLA:W_MLA + W_GQA])
    c = gnorm(oc_ref[...], g_ref[:, W_MLA + W_GQA:])
    y = jnp.dot(a, wb_ref[:W_MLA, :], preferred_element_type=F32)
    y = y + jnp.dot(b, wb_ref[W_MLA:W_MLA + W_GQA, :], preferred_element_type=F32)
    y = y + jnp.dot(c, wb_ref[W_MLA + W_GQA:, :], preferred_element_type=F32)
    return x_ref[...] + g1_ref[...] * y


ROUTER_TM = 256


def _store_pitched(ref, val):
    rows = val.shape[0]
    for s in range(ROW_SLABS):
        ref[pl.ds(s, rows, stride=ROW_PITCH), :] = val[:, s * LANES:(s + 1) * LANES]
    for s in range(ROW_SLABS, ROW_PITCH):
        ref[pl.ds(s, rows, stride=ROW_PITCH), :] = jnp.zeros((rows, LANES), val.dtype)


def _load_pitched_slab(ref, s, rows):
    return ref[pl.ds(s, rows, stride=ROW_PITCH), :]


def _first_argmax(v, vmax, n):
    idx = lax.broadcasted_iota(I32, v.shape, 0)
    return jnp.min(jnp.where(v == vmax, idx, n), axis=0, keepdims=True)


def _router_body(x_ref, g_ref, sh_ref, sc_ref, w_ref, b_ref, u_ref,
                 h_ref, ri_ref, rw_ref, cnt_ref, carry_ref, live):
    h = _modulated_norm(x_ref[...], g_ref[...], sc_ref[...], sh_ref[...])
    _store_pitched(h_ref, h)
    def split(v):
        hi = v.astype(BF16)
        return hi, (v - hi.astype(F32)).astype(BF16)

    def nt_dot(a, b):
        return lax.dot_general(a, b, (((1,), (1,)), ((), ())), preferred_element_type=F32)

    w_hi, w_lo = split(w_ref[...])
    h_hi, h_lo = split(h)
    logits = nt_dot(w_hi, h_hi) + (nt_dot(w_hi, h_lo) + nt_dot(w_lo, h_hi)) + b_ref[...]
    tm = logits.shape[1]

    gl = logits[0:N_GROUPS]
    gmax = jnp.max(gl, axis=0, keepdims=True)
    gexp = jnp.exp(gl - gmax)
    gprob = gexp / jnp.sum(gexp, axis=0, keepdims=True)
    p_g = jnp.max(gprob, axis=0, keepdims=True)
    g_top = _first_argmax(gprob, p_g, N_GROUPS)

    e_in = jnp.zeros((EXPERTS_PER_GROUP, tm), F32)
    for g in range(N_GROUPS):
        lo = ROUTER_E0 + g * EXPERTS_PER_GROUP
        e_in = jnp.where(g_top == g, logits[lo:lo + EXPERTS_PER_GROUP], e_in)
    eexp = jnp.exp(e_in - jnp.max(e_in, axis=0, keepdims=True))
    eprob = eexp / jnp.sum(eexp, axis=0, keepdims=True)
    v1 = jnp.max(eprob, axis=0, keepdims=True)
    i1 = _first_argmax(eprob, v1, EXPERTS_PER_GROUP)
    row = lax.broadcasted_iota(I32, eprob.shape, 0)
    rest = jnp.where(row == i1, -1.0, eprob)
    v2 = jnp.max(rest, axis=0, keepdims=True)
    i2 = _first_argmax(rest, v2, EXPERTS_PER_GROUP)
    wsum = v1 + v2
    w1 = v1 / wsum * p_g
    w2 = v2 / wsum * p_g
    e1 = g_top * EXPERTS_PER_GROUP + i1
    e2 = g_top * EXPERTS_PER_GROUP + i2

    erow = lax.broadcasted_iota(I32, (N_EXPERTS, tm), 0)
    oh1 = (erow == e1).astype(F32)
    oh2 = (erow == e2).astype(F32)
    oh = oh1 + oh2
    before = jnp.dot(oh.astype(BF16), u_ref[...], preferred_element_type=F32) + carry_ref[...]
    r1 = jnp.sum(oh1 * before, axis=0, keepdims=True)
    r2 = jnp.sum(oh2 * before, axis=0, keepdims=True)
    carry_ref[...] = carry_ref[...] + live * jnp.sum(oh, axis=1, keepdims=True)
    cnt_ref[...] = jnp.broadcast_to(carry_ref[...], cnt_ref.shape)

    ri_ref[...] = jnp.zeros(ri_ref.shape, I32)
    ri_ref[0:1, :] = e1
    ri_ref[1:2, :] = e2
    ri_ref[2:3, :] = r1.astype(I32)
    ri_ref[3:4, :] = r2.astype(I32)
    rw_ref[...] = jnp.zeros(rw_ref.shape, F32)
    rw_ref[0:1, :] = w1
    rw_ref[1:2, :] = w2


def _outrouter_kernel(oa_ref, ob_ref, oc_ref, go_ref, w_ref, x_ref, g1_ref,
                      g2_ref, sh_ref, sc_ref, wr_ref, br_ref, u_ref,
                      xo_ref, h_ref, ri_ref, rw_ref, cnt_ref,
                      wb_ref, xa_ref, xb_ref, carry_ref):
    i = pl.program_id(0)

    @pl.when(i == 0)
    def _():
        wb_ref[...] = w_ref[0].astype(BF16)
        xb_ref[...] = jnp.zeros(xb_ref.shape, F32)
        carry_ref[...] = jnp.zeros(carry_ref.shape, F32)

    live = (i > 0).astype(F32)

    def step(dst, src):
        x_new = _outproj_body(oa_ref, ob_ref, oc_ref, go_ref, wb_ref, x_ref, g1_ref)
        xo_ref[...] = x_new
        dst[...] = x_new
        _router_body(src, g2_ref, sh_ref, sc_ref, wr_ref, br_ref, u_ref,
                     h_ref, ri_ref, rw_ref, cnt_ref, carry_ref, live)

    @pl.when(i % 2 == 0)
    def _():
        step(xa_ref, xb_ref)

    @pl.when(i % 2 == 1)
    def _():
        step(xb_ref, xa_ref)


def _outrouter(oa, ob, oc, gains, w_out_all, layer, x, mod, norm2_g, w_r, b_r, upper):
    n = x.shape[0]
    tm = upper.shape[0]
    nb = n // tm

    def cur(i):
        return jnp.minimum(i, nb - 1)

    def prev(i):
        return jnp.maximum(i - 1, 0)

    return pl.pallas_call(
        _outrouter_kernel,
        grid=(nb + 1,),
        in_specs=[
            pl.BlockSpec((tm, W_MLA), lambda i: (cur(i), 0)),
            pl.BlockSpec((tm, W_GQA), lambda i: (cur(i), 0)),
            pl.BlockSpec((tm, W_NA), lambda i: (cur(i), 0)),
            pl.BlockSpec((1, D_MODEL), lambda i: (0, 0)),
            pl.BlockSpec((1, D_MODEL, D_MODEL), lambda i: (layer, 0, 0), pipeline_mode=pl.Buffered(1)),
            pl.BlockSpec((tm, D_MODEL), lambda i: (cur(i), 0)),
            pl.BlockSpec((1, D_MODEL), lambda i: (0, 2)),
            pl.BlockSpec((1, D_MODEL), lambda i: (0, 0)),
            pl.BlockSpec((1, D_MODEL), lambda i: (0, 3)),
            pl.BlockSpec((1, D_MODEL), lambda i: (0, 4)),
            pl.BlockSpec((ROUTER_ROWS, D_MODEL), lambda i: (0, 0)),
            pl.BlockSpec((ROUTER_ROWS, 1), lambda i: (0, 0)),
            pl.BlockSpec((tm, tm), lambda i: (0, 0)),
        ],
        out_specs=(
            pl.BlockSpec((tm, D_MODEL), lambda i: (cur(i), 0)),
            pl.BlockSpec((tm * ROW_PITCH, LANES), lambda i: (prev(i), 0)),
            pl.BlockSpec((SUBLANES, tm), lambda i: (0, prev(i))),
            pl.BlockSpec((SUBLANES, tm), lambda i: (0, prev(i))),
            pl.BlockSpec((N_EXPERTS, LANES), lambda i: (0, 0)),
        ),
        out_shape=(
            jax.ShapeDtypeStruct((n, D_MODEL), F32),
            jax.ShapeDtypeStruct((n * ROW_PITCH, LANES), F32),
            jax.ShapeDtypeStruct((SUBLANES, n), I32),
            jax.ShapeDtypeStruct((SUBLANES, n), F32),
            jax.ShapeDtypeStruct((N_EXPERTS, LANES), F32),
        ),
        scratch_shapes=[
            pltpu.VMEM((D_MODEL, D_MODEL), BF16),
            pltpu.VMEM((tm, D_MODEL), F32),
            pltpu.VMEM((tm, D_MODEL), F32),
            pltpu.VMEM((N_EXPERTS, 1), F32),
        ],
        compiler_params=_cparams(1),
        name="outrouter",
    )(oa, ob, oc, gains, w_out_all, x, mod, norm2_g, mod, mod, w_r, b_r, upper)


GATHER_UNROLL = 8


def _row_gather_start(src_hbm, idx_ref, base, dst, sem, n_rows):
    def body(r, carry):
        off = idx_ref[base + r]
        pltpu.make_async_copy(src_hbm.at[pl.ds(off, ROW_SLABS)],
                              dst.at[pl.ds(r * ROW_PITCH, ROW_SLABS)], sem).start()
        return carry

    lax.fori_loop(0, n_rows, body, 0, unroll=GATHER_UNROLL)


def _row_gather_wait(src_hbm, dst, sem, n_rows):
    n = n_rows * ROW_SLABS
    pltpu.make_async_copy(src_hbm.at[pl.ds(0, n)], dst.at[pl.ds(0, n)], sem).wait()


BLOCK_SLAB_ROWS = MOE_ROWS * ROW_PITCH
GATHER_SLOTS = 4
GATHER_DEPTH = GATHER_SLOTS


def _expert_kernel(src_ref, bstart_ref, h_hbm, wg_ref, wu_ref, wd_ref, y_hbm,
                   xg_ref, yo_ref, wgb_ref, wub_ref, wdb_ref, gsem, osem):
    e = pl.program_id(0)
    lo, hi = bstart_ref[e], bstart_ref[e + 1]
    total = bstart_ref[N_EXPERTS]

    def out_copy(g, slot):
        row0 = pl.multiple_of(g * BLOCK_SLAB_ROWS, BLOCK_SLAB_ROWS)
        return pltpu.make_async_copy(yo_ref.at[slot], y_hbm.at[pl.ds(row0, BLOCK_SLAB_ROWS)], osem.at[slot])

    @pl.when(e == 0)
    def _():
        for d in range(GATHER_DEPTH):
            @pl.when(d < total)
            def _():
                _row_gather_start(h_hbm, src_ref, d * MOE_ROWS, xg_ref.at[d], gsem.at[d], MOE_ROWS)

    @pl.when(hi > lo)
    def _():
        wgb_ref[...] = wg_ref[0].astype(BF16)
        wub_ref[...] = wu_ref[0].astype(BF16)
        wdb_ref[...] = wd_ref[0].astype(BF16)

    def block(g, carry):
        slot = g % 2
        gslot = g % GATHER_SLOTS

        _row_gather_wait(h_hbm, xg_ref.at[gslot], gsem.at[gslot], MOE_ROWS)
        xs = xg_ref.at[gslot]
        x = jnp.concatenate(
            [_load_pitched_slab(xs, s, MOE_ROWS).astype(BF16) for s in range(ROW_SLABS)], axis=1)
        gate = jnp.dot(x, wgb_ref[...], preferred_element_type=F32)
        up = jnp.dot(x, wub_ref[...], preferred_element_type=F32)
        a = (gate * (1.0 / (1.0 + jnp.exp(-gate))) * up).astype(BF16)
        y = jnp.dot(a, wdb_ref[...], preferred_element_type=F32)

        @pl.when(g + GATHER_DEPTH < total)
        def _():
            nslot = (g + GATHER_DEPTH) % GATHER_SLOTS
            _row_gather_start(h_hbm, src_ref, (g + GATHER_DEPTH) * MOE_ROWS, xg_ref.at[nslot], gsem.at[nslot],
                              MOE_ROWS)

        @pl.when(g >= 2)
        def _():
            out_copy(g - 2, slot).wait()

        _store_pitched(yo_ref.at[slot], y)
        out_copy(g, slot).start()
        return carry

    lax.fori_loop(lo, hi, block, 0)

    @pl.when(e == N_EXPERTS - 1)
    def _():
        @pl.when(total >= 2)
        def _():
            out_copy(total - 2, total % 2).wait()

        @pl.when(total >= 1)
        def _():
            out_copy(total - 1, (total - 1) % 2).wait()

        n_blocks = y_hbm.shape[0] // BLOCK_SLAB_ROWS
        yo_ref[0] = jnp.zeros(yo_ref.shape[1:], F32)

        def zero_start(g, carry):
            out_copy(g, 0).start()
            return carry

        def zero_wait(g, carry):
            out_copy(g, 0).wait()
            return carry

        lax.fori_loop(total, n_blocks, zero_start, 0)
        lax.fori_loop(total, n_blocks, zero_wait, 0)


def _experts(src_off, bstart, h, w_gate, w_up, w_down, layer, n_blocks):
    base = layer * N_EXPERTS
    grid_spec = pltpu.PrefetchScalarGridSpec(
        num_scalar_prefetch=2,
        grid=(N_EXPERTS,),
        in_specs=[
            pl.BlockSpec(memory_space=pl.ANY),
            pl.BlockSpec((1, D_MODEL, D_EXPERT), lambda e, src, bs: (base + e, 0, 0)),
            pl.BlockSpec((1, D_MODEL, D_EXPERT), lambda e, src, bs: (base + e, 0, 0)),
            pl.BlockSpec((1, D_EXPERT, D_MODEL), lambda e, src, bs: (base + e, 0, 0)),
        ],
        out_specs=pl.BlockSpec(memory_space=pl.ANY),
        scratch_shapes=[
            pltpu.VMEM((GATHER_SLOTS, BLOCK_SLAB_ROWS, LANES), F32),
            pltpu.VMEM((2, BLOCK_SLAB_ROWS, LANES), F32),
            pltpu.VMEM((D_MODEL, D_EXPERT), BF16),
            pltpu.VMEM((D_MODEL, D_EXPERT), BF16),
            pltpu.VMEM((D_EXPERT, D_MODEL), BF16),
            pltpu.SemaphoreType.DMA((GATHER_SLOTS,)),
            pltpu.SemaphoreType.DMA((2,)),
        ],
    )
    return pl.pallas_call(
        _expert_kernel,
        grid_spec=grid_spec,
        out_shape=jax.ShapeDtypeStruct((n_blocks * BLOCK_SLAB_ROWS, LANES), F32),
        compiler_params=_cparams(1),
        name="experts",
    )(src_off, bstart, h, w_gate, w_up, w_down)


COMB_TM = 256


def _combine_kernel(d1_ref, d2_ref, y_hbm, x_ref, w1_ref, w2_ref, g2_ref, o_ref, ya_ref, yb_ref, sem):
    i = pl.program_id(0)
    n = pl.num_programs(0)
    slot = i % 2

    def start(blk, s):
        _row_gather_start(y_hbm, d1_ref, blk * COMB_TM, ya_ref.at[s], sem.at[0, s], COMB_TM)
        _row_gather_start(y_hbm, d2_ref, blk * COMB_TM, yb_ref.at[s], sem.at[1, s], COMB_TM)

    @pl.when(i == 0)
    def _():
        start(0, 0)

    @pl.when(i + 1 < n)
    def _():
        start(i + 1, 1 - slot)

    _row_gather_wait(y_hbm, ya_ref.at[slot], sem.at[0, slot], COMB_TM)
    _row_gather_wait(y_hbm, yb_ref.at[slot], sem.at[1, slot], COMB_TM)
    w1, w2 = w1_ref[...], w2_ref[...]
    ya, yb = ya_ref.at[slot], yb_ref.at[slot]
    for s in range(ROW_SLABS):
        cols = slice(s * LANES, (s + 1) * LANES)
        y = w1 * _load_pitched_slab(ya, s, COMB_TM) + w2 * _load_pitched_slab(yb, s, COMB_TM)
        o_ref[:, cols] = x_ref[:, cols] + g2_ref[:, cols] * y


def _combine(dest1, dest2, ybuf, x, w1, w2, mod):
    n = x.shape[0]
    grid_spec = pltpu.PrefetchScalarGridSpec(
        num_scalar_prefetch=2,
        grid=(n // COMB_TM,),
        in_specs=[
            pl.BlockSpec(memory_space=pl.ANY),
            pl.BlockSpec((COMB_TM, D_MODEL), lambda i, d1, d2: (i, 0)),
            pl.BlockSpec((COMB_TM, 1), lambda i, d1, d2: (i, 0)),
            pl.BlockSpec((COMB_TM, 1), lambda i, d1, d2: (i, 0)),
            pl.BlockSpec((1, D_MODEL), lambda i, d1, d2: (0, 5)),
        ],
        out_specs=pl.BlockSpec((COMB_TM, D_MODEL), lambda i, d1, d2: (i, 0)),
        scratch_shapes=[
            pltpu.VMEM((2, COMB_TM * ROW_PITCH, LANES), F32),
            pltpu.VMEM((2, COMB_TM * ROW_PITCH, LANES), F32),
            pltpu.SemaphoreType.DMA((2, 2)),
        ],
    )
    return pl.pallas_call(
        _combine_kernel,
        grid_spec=grid_spec,
        out_shape=jax.ShapeDtypeStruct((n, D_MODEL), F32),
        compiler_params=_cparams(1),
        name="combine",
    )(dest1, dest2, ybuf, x, w1, w2, mod)


def _rope_angles(pos, d):
    f32 = np.float32
    inv = f32(ROPE_THETA) ** (-np.arange(0, d, 2, dtype=f32) / f32(d))
    return pos.astype(f32)[:, None] * inv[None, :]


def _rope_tables(S):
    t = np.arange(S)
    z = np.zeros((S, MLA_ROPE), np.float32)
    a = _rope_angles(t, MLA_ROPE)
    cm = np.concatenate([np.cos(a), np.cos(a), z], axis=1)
    sm = np.concatenate([-np.sin(a), np.sin(a), z], axis=1)
    ar = _rope_angles(t // GRID_W, HEAD_DIM // 2)
    ac = _rope_angles(t % GRID_W, HEAD_DIM // 2)
    ca = np.concatenate([np.cos(ar), np.cos(ar), np.cos(ac), np.cos(ac)], axis=1)
    sa = np.concatenate([-np.sin(ar), np.sin(ar), -np.sin(ac), np.sin(ac)], axis=1)
    return tuple(jnp.asarray(v, F32) for v in (cm, sm, ca, sa))


def _pad_lanes(a, width):
    return jnp.concatenate([a, jnp.zeros(a.shape[:-1] + (width - a.shape[-1],), a.dtype)], axis=-1)


def _layer_weights(w_in, w_uq, w_ukv, qk_q_g, qk_k_g):
    split = C_APE + MLA_ROPE
    w_in_p = jnp.concatenate(
        [w_in[:, :split], jnp.zeros((D_MODEL, PE_PAD - MLA_ROPE), F32), w_in[:, split:]], axis=1
    ).astype(BF16)
    wq = w_uq.reshape(MLA_Q_RANK, MLA_HEADS, MLA_QK)
    wq_n = wq[:, :, :MLA_NOPE].reshape(MLA_Q_RANK, MLA_HEADS * LANES)
    wq_r = _pad_lanes(wq[:, :, MLA_NOPE:], LANES).reshape(MLA_Q_RANK, MLA_HEADS * LANES)
    wq_p = jnp.concatenate([wq_n, wq_r], axis=1).astype(BF16)
    wkv = w_ukv.reshape(MLA_KV_RANK, MLA_HEADS, MLA_NOPE + MLA_V)
    wkv_p = jnp.concatenate(
        [wkv[:, :, :MLA_NOPE].reshape(MLA_KV_RANK, -1), wkv[:, :, MLA_NOPE:].reshape(MLA_KV_RANK, -1)],
        axis=1).astype(BF16)
    gmq = _pad_lanes(qk_q_g[None, :], MLA_QK_PAD)
    gmk = _pad_lanes(qk_k_g[None, :], MLA_QK_PAD)
    return w_in_p, wq_p, wkv_p, gmq, gmk


def _slot_kernel(start_ref, ri_ref, o_ref):
    e = ri_ref[0:2, :]
    first = jnp.zeros(e.shape, I32)
    for k in range(N_EXPERTS):
        first = jnp.where(e == k, start_ref[k], first)
    slot = first + ri_ref[2:4, :]
    o_ref[...] = jnp.zeros(o_ref.shape, I32)
    o_ref[0:2, :] = slot
    o_ref[2:4, :] = slot * ROW_PITCH


def _slots(pad_start, ri):
    return pl.pallas_call(
        _slot_kernel,
        grid=(1,),
        in_specs=[pl.BlockSpec(memory_space=pltpu.SMEM), pl.BlockSpec(ri.shape, lambda i: (0, 0))],
        out_specs=pl.BlockSpec(ri.shape, lambda i: (0, 0)),
        out_shape=jax.ShapeDtypeStruct(ri.shape, I32),
        compiler_params=_cparams(1),
        name="slots",
    )(pad_start, ri)


def _routing_plan(ri, cnt, n_blocks):
    n = ri.shape[1]
    counts = cnt[:, 0].astype(I32)
    padded = ((counts + MOE_ROWS - 1) // MOE_ROWS) * MOE_ROWS
    pad_end = jnp.cumsum(padded)
    pad_start = pad_end - padded
    slots = _slots(pad_start, ri)
    dest1, dest2 = slots[0], slots[1]
    tok_off = jnp.arange(n, dtype=I32) * ROW_PITCH
    pad_src = (jnp.arange(n_blocks * MOE_ROWS, dtype=I32) % n) * ROW_PITCH
    src_off = pad_src.at[jnp.concatenate([dest1, dest2])].set(
        jnp.concatenate([tok_off, tok_off]), unique_indices=True)
    bstart = jnp.concatenate([jnp.zeros((1,), I32), pad_end // MOE_ROWS]).astype(I32)
    return slots[2], slots[3], src_off, bstart


def kernel(x, c, ada_w, ada_b, norm1_g, norm2_g, w_in, mla_q_norm_g, mla_w_uq, mla_kv_norm_g, mla_w_ukv, mla_qk_q_g, mla_qk_k_g, gqa_q_g, gqa_k_g, na_q_g, na_k_g, na_rpb, mix_out_norm_g, w_out, router_group_w, router_group_b, router_expert_w, router_expert_b, expert_w_gate, expert_w_up, expert_w_down):
    B, S, D = x.shape
    assert B == 1 and D == D_MODEL and S % (GRID_W * NA_QROWS) == 0 and S // GRID_W >= NA_KROWS
    depth = ada_w.shape[0]
    n = B * S
    xt = x.reshape(n, D)

    mod_all = _adaln(c, ada_w, ada_b)
    na_bias = _na_bias(na_rpb)
    cm, sm, ca, sa = _rope_tables(S)
    tm_r = min(ROUTER_TM, n)
    upper = jnp.asarray(np.arange(tm_r)[:, None] < np.arange(tm_r)[None, :], BF16)
    n_blocks = -(-(n * 2 + N_EXPERTS * (MOE_ROWS - 1)) // MOE_ROWS)
    wg_all = expert_w_gate.reshape(depth * N_EXPERTS, D_MODEL, D_EXPERT)
    wu_all = expert_w_up.reshape(depth * N_EXPERTS, D_MODEL, D_EXPERT)
    wd_all = expert_w_down.reshape(depth * N_EXPERTS, D_EXPERT, D_MODEL)

    def row(v):
        return v[None, :]

    for l in range(depth):
        mod = mod_all[l]
        w_in_p, wq_p, wkv_p, gmq, gmk = _layer_weights(
            w_in[l], mla_w_uq[l], mla_w_ukv[l], mla_qk_q_g[l], mla_qk_k_g[l])

        qm, km, vmt, qg, kg, vgt, qn, kn, vn = _mixin(
            xt, row(norm1_g[l]), mod, w_in_p, wq_p, wkv_p, row(mla_q_norm_g[l]), row(mla_kv_norm_g[l]), gmq, gmk,
            row(gqa_q_g[l]), row(gqa_k_g[l]), row(na_q_g[l]), row(na_k_g[l]), cm, sm, ca, sa)
        o_a = _flash(qm, km, vmt)
        o_b = _flash(qg, kg, vgt)
        o_c = _na(qn, kn, vn, na_bias, l)
        w_r = jnp.concatenate(
            [router_group_w[l].T, jnp.zeros((ROUTER_E0 - N_GROUPS, D), F32), router_expert_w[l].T], axis=0)
        b_r = jnp.concatenate(
            [router_group_b[l], jnp.zeros((ROUTER_E0 - N_GROUPS,), F32), router_expert_b[l]])[:, None]
        xt, h2, ri, rw, cnt = _outrouter(o_a, o_b, o_c, row(mix_out_norm_g[l]), w_out, l, xt, mod,
                                         row(norm2_g[l]), w_r, b_r, upper)
        off1, off2, src_off, bstart = _routing_plan(ri, cnt, n_blocks)
        ybuf = _experts(src_off, bstart, h2, wg_all, wu_all, wd_all, l, n_blocks)
        xt = _combine(off1, off2, ybuf, xt, rw[0][:, None], rw[1][:, None], mod)

    return xt.reshape(B, S, D)
```

```python
import functools

import jax
import jax.numpy as jnp
import numpy as np
from jax import lax
from jax.experimental import pallas as pl
from jax.experimental.pallas import tpu as pltpu

F32 = jnp.float32
BF16 = jnp.bfloat16
I32 = jnp.int32

D_MODEL = 2048
GRID_W = 64
HEAD_DIM = 128
ROPE_THETA = 10000.0
EPS = 1e-6
MLA_HEADS = 6
MLA_Q_RANK = 384
MLA_KV_RANK = 256
MLA_NOPE = 128
MLA_ROPE = 64
MLA_V = 128
MLA_QK = MLA_NOPE + MLA_ROPE
GQA_HEADS = 4
GQA_KV_HEADS = 2
NA_HEADS = 6
NA_KH = 8
NA_KW = 16
N_GROUPS = 4
EXPERTS_PER_GROUP = 8
N_EXPERTS = N_GROUPS * EXPERTS_PER_GROUP
D_EXPERT = 512

LANES = 128
SUBLANES = 8
VMEM_LIMIT_BYTES = 56 * 1024 * 1024

PE_PAD = LANES
C_AQ = 0
C_AKV = C_AQ + MLA_Q_RANK
C_APE = C_AKV + MLA_KV_RANK
C_GQ = C_APE + PE_PAD
C_GK = C_GQ + GQA_HEADS * HEAD_DIM
C_GV = C_GK + GQA_KV_HEADS * HEAD_DIM
C_NQ = C_GV + GQA_KV_HEADS * HEAD_DIM
C_NK = C_NQ + NA_HEADS * HEAD_DIM
C_NV = C_NK + NA_HEADS * HEAD_DIM
D_IN_PAD = C_NV + NA_HEADS * HEAD_DIM
MLA_QK_PAD = 2 * LANES

NA_QROWS = 4
NA_KROWS = 12
NA_TQ = NA_QROWS * GRID_W
NA_TK = NA_KROWS * GRID_W

LOG2E = float(np.log2(np.e))
DV_EXT = HEAD_DIM + 16

MOE_ROWS = 256
ROW_SLABS = D_MODEL // LANES
ROW_PITCH = 20
ROUTER_ROWS = 40
ROUTER_E0 = 8


def _cparams(n_axes):
    return pltpu.CompilerParams(
        dimension_semantics=("arbitrary",) * n_axes,
        vmem_limit_bytes=VMEM_LIMIT_BYTES,
    )


ADA_TN = 1024
ADA_RC = 256


def _adaln_kernel(c_ref, w_ref, b_ref, o_ref):
    def body(i, acc):
        r0 = pl.multiple_of(i * ADA_RC, ADA_RC)
        c = c_ref[pl.ds(r0, ADA_RC), :]
        cond = c * (1.0 / (1.0 + jnp.exp(-c)))
        w = w_ref[0, pl.ds(r0, ADA_RC), :]
        return acc + jnp.sum(w * cond, axis=0, keepdims=True)

    acc = lax.fori_loop(0, D_MODEL // ADA_RC, body, jnp.zeros((1, ADA_TN), F32))
    o_ref[0] = acc + b_ref[0]


def _adaln(c, ada_w, ada_b):
    L = ada_w.shape[0]
    n_out = ada_w.shape[2]
    return pl.pallas_call(
        _adaln_kernel,
        grid=(L, n_out // ADA_TN),
        in_specs=[
            pl.BlockSpec((D_MODEL, 1), lambda l, j: (0, 0)),
            pl.BlockSpec((1, D_MODEL, ADA_TN), lambda l, j: (l, 0, j)),
            pl.BlockSpec((1, 1, ADA_TN), lambda l, j: (l, 0, j)),
        ],
        out_specs=pl.BlockSpec((1, 1, ADA_TN), lambda l, j: (l, 0, j)),
        out_shape=jax.ShapeDtypeStruct((L, 1, n_out), F32),
        compiler_params=_cparams(2),
        name="adaln",
    )(c.reshape(D_MODEL, 1), ada_w, ada_b.reshape(L, 1, n_out))


def _modulated_norm(x, g, sc, sh):
    ms = jnp.mean(x * x, axis=-1, keepdims=True)
    y = x * lax.rsqrt(ms + EPS) * g
    return y * (1.0 + sc) + sh


MIX_TM = 256
INPROJ_TN = 1024


def _rot_swap(x):
    lane = lax.broadcasted_iota(I32, x.shape, 1)
    fwd = pltpu.roll(x, LANES - 32, 1)
    bwd = pltpu.roll(x, 32, 1)
    return jnp.where((lane & 63) < 32, fwd, bwd)


def _rope_tile(x, cos, sin_signed):
    return x * cos + _rot_swap(x) * sin_signed


def _sumsq(x):
    return jnp.sum(x * x, axis=-1, keepdims=True)


def _prep_body(p_ref, wq_ref, wkv_ref, gql_ref, gkvl_ref, gmq_ref, gmk_ref, ggq_ref, ggk_ref,
                 gnq_ref, gnk_ref, cm_ref, sm_ref, ca_ref, sa_ref,
                 qm_ref, km_ref, vmt_ref, qg_ref, kg_ref, vgt_ref, qn_ref, kn_ref, vn_ref):
    mla_scale = MLA_QK ** -0.5 * LOG2E
    gqa_scale = HEAD_DIM ** -0.5 * LOG2E
    hd_scale = HEAD_DIM ** -0.5
    tm = p_ref.shape[0]
    ones_rows = (lax.broadcasted_iota(I32, (DV_EXT - HEAD_DIM, tm), 0) == 0).astype(BF16)
    cm, sm = cm_ref[...], sm_ref[...]
    ca, sa = ca_ref[...], sa_ref[...]

    a_q = p_ref[:, C_AQ:C_AQ + MLA_Q_RANK]
    a_q = a_q * lax.rsqrt(_sumsq(a_q) / MLA_Q_RANK + EPS) * gql_ref[...]
    q = jnp.dot(a_q.astype(BF16), wq_ref[...], preferred_element_type=F32)
    gq_n, gq_r = gmq_ref[:, :LANES] * mla_scale, gmq_ref[:, LANES:] * mla_scale
    for h in range(MLA_HEADS):
        nope = q[:, h * LANES:(h + 1) * LANES]
        ropp = q[:, (MLA_HEADS + h) * LANES:(MLA_HEADS + h + 1) * LANES]
        rinv = lax.rsqrt((_sumsq(nope) + _sumsq(ropp)) / MLA_QK + EPS)
        qm_ref[h, :, :LANES] = (nope * rinv * gq_n).astype(BF16)
        qm_ref[h, :, LANES:] = _rope_tile(ropp * rinv * gq_r, cm, sm).astype(BF16)

    a_kv = p_ref[:, C_AKV:C_AKV + MLA_KV_RANK]
    a_kv = a_kv * lax.rsqrt(_sumsq(a_kv) / MLA_KV_RANK + EPS) * gkvl_ref[...]
    kv = jnp.dot(a_kv.astype(BF16), wkv_ref[...], preferred_element_type=F32)
    pe = p_ref[:, C_APE:C_APE + PE_PAD]
    ss_pe = _sumsq(pe)
    gk_n, gk_r = gmk_ref[:, :LANES], gmk_ref[:, LANES:]
    pe_rot = _rope_tile(pe * gk_r, cm, sm)
    for h in range(MLA_HEADS):
        nope = kv[:, h * LANES:(h + 1) * LANES]
        v = kv[:, (MLA_HEADS + h) * LANES:(MLA_HEADS + h + 1) * LANES]
        rinv = lax.rsqrt((_sumsq(nope) + ss_pe) / MLA_QK + EPS)
        km_ref[h, :, :LANES] = (nope * rinv * gk_n).astype(BF16)
        km_ref[h, :, LANES:] = (pe_rot * rinv).astype(BF16)
        vmt_ref[h, 0, :HEAD_DIM] = v.T.astype(BF16)
        vmt_ref[h, 0, HEAD_DIM:] = ones_rows

    def head_norm(col, g):
        t = p_ref[:, col:col + HEAD_DIM]
        return t * lax.rsqrt(_sumsq(t) / HEAD_DIM + EPS) * g

    ggq_s = ggq_ref[...] * gqa_scale
    for h in range(GQA_HEADS):
        qg_ref[h] = _rope_tile(head_norm(C_GQ + h * HEAD_DIM, ggq_s), ca, sa).astype(BF16)
    for h in range(GQA_KV_HEADS):
        t = head_norm(C_GK + h * HEAD_DIM, ggk_ref[...])
        kg_ref[h] = _rope_tile(t, ca, sa).astype(BF16)
        vgt_ref[h, 0, :HEAD_DIM] = p_ref[:, C_GV + h * HEAD_DIM:C_GV + (h + 1) * HEAD_DIM].T.astype(BF16)
        vgt_ref[h, 0, HEAD_DIM:] = ones_rows

    gnq_s = gnq_ref[...] * hd_scale
    for h in range(NA_HEADS):
        qn_ref[h] = head_norm(C_NQ + h * HEAD_DIM, gnq_s).astype(BF16)
        kn_ref[h] = head_norm(C_NK + h * HEAD_DIM, gnk_ref[...]).astype(BF16)
        vn_ref[h] = p_ref[:, C_NV + h * HEAD_DIM:C_NV + (h + 1) * HEAD_DIM].astype(BF16)


def _mixin_kernel(x_ref, g_ref, sh_ref, sc_ref, w_ref, *rest):
    prep_args, proj = rest[:-2], rest[-2:]
    i = pl.program_id(0)

    @pl.when(i == 0)
    def _():
        proj[1][...] = jnp.zeros(proj[1].shape, F32)

    def step(dst, src):
        h = _modulated_norm(x_ref[...], g_ref[...], sc_ref[...], sh_ref[...]).astype(BF16)
        for j in range(D_IN_PAD // INPROJ_TN):
            cols = slice(j * INPROJ_TN, (j + 1) * INPROJ_TN)
            dst[:, cols] = jnp.dot(h, w_ref[:, cols], preferred_element_type=F32)
        _prep_body(src, *prep_args)

    @pl.when(i % 2 == 0)
    def _():
        step(proj[0], proj[1])

    @pl.when(i % 2 == 1)
    def _():
        step(proj[1], proj[0])


def _mixin(x, norm_g, mod, w_in_p, wq, wkv, gql, gkvl, gmq, gmk, ggq, ggk, gnq, gnk, cm, sm, ca, sa):
    n = x.shape[0]
    tm = min(MIX_TM, n)
    nb = n // tm

    def cur(i):
        return jnp.minimum(i, nb - 1)

    def prev(i):
        return jnp.maximum(i - 1, 0)

    def full(a):
        return pl.BlockSpec(a.shape, lambda i: (0,) * a.ndim)

    def rows(width):
        return pl.BlockSpec((tm, width), lambda i: (prev(i), 0))

    def heads(h, width):
        return pl.BlockSpec((h, tm, width), lambda i: (0, prev(i), 0))

    def heads_t(h):
        return pl.BlockSpec((h, 1, DV_EXT, tm), lambda i: (0, prev(i), 0, 0))

    out_shape = (
        jax.ShapeDtypeStruct((MLA_HEADS, n, MLA_QK_PAD), BF16),
        jax.ShapeDtypeStruct((MLA_HEADS, n, MLA_QK_PAD), BF16),
        jax.ShapeDtypeStruct((MLA_HEADS, nb, DV_EXT, tm), BF16),
        jax.ShapeDtypeStruct((GQA_HEADS, n, HEAD_DIM), BF16),
        jax.ShapeDtypeStruct((GQA_KV_HEADS, n, HEAD_DIM), BF16),
        jax.ShapeDtypeStruct((GQA_KV_HEADS, nb, DV_EXT, tm), BF16),
        jax.ShapeDtypeStruct((NA_HEADS, n, HEAD_DIM), BF16),
        jax.ShapeDtypeStruct((NA_HEADS, n, HEAD_DIM), BF16),
        jax.ShapeDtypeStruct((NA_HEADS, n, HEAD_DIM), BF16),
    )
    out_specs = (
        heads(MLA_HEADS, MLA_QK_PAD), heads(MLA_HEADS, MLA_QK_PAD), heads_t(MLA_HEADS),
        heads(GQA_HEADS, HEAD_DIM), heads(GQA_KV_HEADS, HEAD_DIM), heads_t(GQA_KV_HEADS),
        heads(NA_HEADS, HEAD_DIM), heads(NA_HEADS, HEAD_DIM), heads(NA_HEADS, HEAD_DIM),
    )
    small = (wq, wkv, gql, gkvl, gmq, gmk, ggq, ggk, gnq, gnk)
    return pl.pallas_call(
        _mixin_kernel,
        grid=(nb + 1,),
        in_specs=[
            pl.BlockSpec((tm, D_MODEL), lambda i: (cur(i), 0)),
            pl.BlockSpec((1, D_MODEL), lambda i: (0, 0)),
            pl.BlockSpec((1, D_MODEL), lambda i: (0, 0)),
            pl.BlockSpec((1, D_MODEL), lambda i: (0, 1)),
            pl.BlockSpec((D_MODEL, D_IN_PAD), lambda i: (0, 0), pipeline_mode=pl.Buffered(1)),
        ] + [full(a) for a in small] + [rows(LANES)] * 4,
        out_specs=out_specs,
        out_shape=out_shape,
        scratch_shapes=[pltpu.VMEM((tm, D_IN_PAD), F32), pltpu.VMEM((tm, D_IN_PAD), F32)],
        compiler_params=_cparams(1),
        name="mixin",
    )(x, norm_g, mod, mod, w_in_p, *small, cm, sm, ca, sa)


FLASH_TQ = 1024
FLASH_TK = 512
FLASH_UNROLL = 8


def _flash_kernel(q_ref, k_ref, vt_ref, o_ref, m_ref, acc_ref, s_ref, cm_ref, *, n_chunks, tk, n_sub, unroll):
    dv = acc_ref.shape[0] - (DV_EXT - HEAD_DIM)
    tsub = tk // n_sub
    m_ref[...] = jnp.full(m_ref.shape, -jnp.inf, F32)
    acc_ref[...] = jnp.zeros(acc_ref.shape, F32)

    def scores(c, slot):
        kc = k_ref[0, pl.ds(pl.multiple_of(c * tk, tk), tk), :]
        s = lax.dot_general(kc, q_ref[0], (((1,), (1,)), ((), ())), preferred_element_type=F32)
        s_ref[slot] = s
        cm_ref[slot] = jnp.max(s, axis=0, keepdims=True)

    def update(c, slot):
        m_prev = m_ref[...]
        m_new = jnp.maximum(m_prev, cm_ref[slot])
        p = jnp.exp2(s_ref[slot] - m_new).astype(BF16)
        alpha = jnp.exp2(m_prev - m_new)
        pv = jnp.dot(vt_ref[0, n_sub * c], p[:tsub], preferred_element_type=F32)
        for u in range(1, n_sub):
            pv = pv + jnp.dot(vt_ref[0, n_sub * c + u], p[u * tsub:(u + 1) * tsub], preferred_element_type=F32)
        acc_ref[...] = alpha * acc_ref[...] + pv
        m_ref[...] = m_new

    scores(0, 0)

    def body(j, carry):
        c0 = unroll * j
        for u in range(unroll):
            nxt = c0 + u + 1
            if u == unroll - 1:
                nxt = jnp.where(nxt == n_chunks, 0, nxt)
            scores(nxt, (u + 1) % 2)
            update(c0 + u, u % 2)
        return carry

    lax.fori_loop(0, n_chunks // unroll, body, 0)
    acc = acc_ref[...]
    o_ref[...] = (acc[:dv] / acc[dv:dv + 1]).T


def _flash(q, k, vt):
    H, S, dq = q.shape
    hk, n_vt, dv_ext, tsub = vt.shape
    tk = max(min(FLASH_TK, S), tsub)
    n_sub = tk // tsub
    n_chunks = S // tk
    dv = dv_ext - (DV_EXT - HEAD_DIM)
    group = H // hk
    tq = min(FLASH_TQ, S)
    unroll = FLASH_UNROLL if n_chunks % FLASH_UNROLL == 0 else 2
    assert n_chunks % unroll == 0
    return pl.pallas_call(
        functools.partial(_flash_kernel, n_chunks=n_chunks, tk=tk, n_sub=n_sub, unroll=unroll),
        grid=(H, S // tq),
        in_specs=[
            pl.BlockSpec((1, tq, dq), lambda h, i: (h, i, 0)),
            pl.BlockSpec((1, S, dq), lambda h, i: (h // group, 0, 0)),
            pl.BlockSpec((1, n_vt, dv_ext, tsub), lambda h, i: (h // group, 0, 0, 0)),
        ],
        out_specs=pl.BlockSpec((tq, dv), lambda h, i: (i, h)),
        out_shape=jax.ShapeDtypeStruct((S, H * dv), F32),
        scratch_shapes=[pltpu.VMEM((1, tq), F32), pltpu.VMEM((dv_ext, tq), F32),
                        pltpu.VMEM((2, tk, tq), F32), pltpu.VMEM((2, 1, tq), F32)],
        compiler_params=_cparams(2),
        name="flash",
    )(q, k, vt)


NEG_MIN = float(np.finfo(np.float32).min)
RPB_ROWS = 2 * NA_KH - 1
RPB_COLS = 2 * NA_KW - 1


def _na_rel(kind, i, j):
    if kind == 0:
        return j - i + NA_KH - 1, j < NA_KH
    if kind == 1:
        return j - i + NA_KH // 2 - 1, i <= j <= i + NA_KH - 1
    return j - i - 1, j >= NA_KROWS - NA_KH


def _na_bias_kernel(rpb_ref, o_ref):
    h = pl.program_id(0)
    shape = (GRID_W, LANES)
    c = lax.broadcasted_iota(I32, shape, 0)
    kc = lax.broadcasted_iota(I32, shape, 1) & (GRID_W - 1)
    cs = jnp.clip(c - NA_KW // 2, 0, GRID_W - NA_KW)
    in_win = (kc >= cs) & (kc < cs + NA_KW)
    dc = kc - c + NA_KW - 1
    lane_lo = lax.broadcasted_iota(I32, shape, 1) < GRID_W
    masked = jnp.full(shape, NEG_MIN, F32)
    cache = {}

    def toeplitz(dr):
        if dr not in cache:
            t = jnp.zeros(shape, F32)
            base = (h * RPB_ROWS + dr) * RPB_COLS
            for d in range(RPB_COLS):
                t = jnp.where(dc == d, rpb_ref[base + d], t)
            cache[dr] = jnp.where(in_win, t, NEG_MIN)
        return cache[dr]

    def half(kind, i, j):
        dr, valid = _na_rel(kind, i, j)
        return toeplitz(dr) if valid else masked

    for kind in range(3):
        for i in range(NA_QROWS):
            for jp in range(NA_KROWS // 2):
                piece = jnp.where(lane_lo, half(kind, i, 2 * jp), half(kind, i, 2 * jp + 1))
                o_ref[kind, 0, i * GRID_W:(i + 1) * GRID_W, jp * LANES:(jp + 1) * LANES] = piece


def _na_bias(rpb_all):
    n_tables = rpb_all.shape[0] * NA_HEADS
    return pl.pallas_call(
        _na_bias_kernel,
        grid=(n_tables,),
        in_specs=[pl.BlockSpec(memory_space=pltpu.SMEM)],
        out_specs=pl.BlockSpec((3, 1, NA_TQ, NA_TK), lambda h: (0, h, 0, 0)),
        out_shape=jax.ShapeDtypeStruct((3, n_tables, NA_TQ, NA_TK), F32),
        compiler_params=_cparams(1),
        name="na_bias",
    )(rpb_all.reshape(-1))


NA_SUB = 16


def _na_kernel(q_ref, k_ref, v_ref, b_ref, o_ref, *, grid_rows, sub):
    i = pl.program_id(1)
    nblk = grid_rows // NA_QROWS

    def scores(j):
        blk = i * sub + j
        kstart = jnp.clip(blk * NA_QROWS - NA_KH // 2, 0, grid_rows - NA_KROWS)
        off = pl.multiple_of(kstart * GRID_W, GRID_W)
        kind = jnp.where(blk == 0, 0, jnp.where(blk == nblk - 1, 2, 1))
        kw = k_ref[0, pl.ds(off, NA_TK), :]
        q = q_ref[0, j * NA_TQ:(j + 1) * NA_TQ, :]
        s = lax.dot_general(q, kw, (((1,), (1,)), ((), ())), preferred_element_type=F32)
        return s + b_ref[kind, 0], off

    def finish(j, s, off):
        vw = v_ref[0, pl.ds(off, NA_TK), :]
        p = jnp.exp(s - jnp.max(s, axis=-1, keepdims=True))
        l = jnp.sum(p, axis=-1, keepdims=True)
        o_ref[j * NA_TQ:(j + 1) * NA_TQ, :] = jnp.dot(p.astype(BF16), vw, preferred_element_type=F32) / l

    cur = scores(0)
    for j in range(sub):
        nxt = scores(j + 1) if j + 1 < sub else None
        finish(j, *cur)
        cur = nxt


def _na(q, k, v, bias, layer):
    H, S, d = q.shape
    grid_rows = S // GRID_W
    nblk = grid_rows // NA_QROWS
    sub = int(np.gcd(nblk, NA_SUB))
    return pl.pallas_call(
        functools.partial(_na_kernel, grid_rows=grid_rows, sub=sub),
        grid=(H, nblk // sub),
        in_specs=[
            pl.BlockSpec((1, NA_TQ * sub, d), lambda h, i: (h, i, 0)),
            pl.BlockSpec((1, S, d), lambda h, i: (h, 0, 0)),
            pl.BlockSpec((1, S, d), lambda h, i: (h, 0, 0)),
            pl.BlockSpec((3, 1, NA_TQ, NA_TK), lambda h, i: (0, layer * H + h, 0, 0)),
        ],
        out_specs=pl.BlockSpec((NA_TQ * sub, d), lambda h, i: (i, h)),
        out_shape=jax.ShapeDtypeStruct((S, H * d), F32),
        compiler_params=_cparams(2),
        name="na",
    )(q, k, v, bias)


W_MLA = MLA_HEADS * MLA_V
W_GQA = GQA_HEADS * HEAD_DIM
W_NA = NA_HEADS * HEAD_DIM


def _outproj_body(oa_ref, ob_ref, oc_ref, g_ref, wb_ref, x_ref, g1_ref):
    def gnorm(o, g):
        return (o * lax.rsqrt(jnp.mean(o * o, axis=-1, keepdims=True) + EPS) * g).astype(BF16)

    a = gnorm(oa_ref[...], g_ref[:, :W_MLA])
    b = gnorm(ob_ref[...], g_ref[:, W_MLA:W_MLA + W_GQA])
    c = gnorm(oc_ref[...], g_ref[:, W_MLA + W_GQA:])
    y = jnp.dot(a, wb_ref[:W_MLA, :], preferred_element_type=F32)
    y = y + jnp.dot(b, wb_ref[W_MLA:W_MLA + W_GQA, :], preferred_element_type=F32)
    y = y + jnp.dot(c, wb_ref[W_MLA + W_GQA:, :], preferred_element_type=F32)
    return x_ref[...] + g1_ref[...] * y


ROUTER_TM = 256


def _store_pitched(ref, val):
    rows = val.shape[0]
    for s in range(ROW_SLABS):
        ref[pl.ds(s, rows, stride=ROW_PITCH), :] = val[:, s * LANES:(s + 1) * LANES]
    for s in range(ROW_SLABS, ROW_PITCH):
        ref[pl.ds(s, rows, stride=ROW_PITCH), :] = jnp.zeros((rows, LANES), val.dtype)


def _load_pitched_slab(ref, s, rows):
    return ref[pl.ds(s, rows, stride=ROW_PITCH), :]


def _first_argmax(v, vmax, n):
    idx = lax.broadcasted_iota(I32, v.shape, 0)
    return jnp.min(jnp.where(v == vmax, idx, n), axis=0, keepdims=True)


def _router_body(x_ref, g_ref, sh_ref, sc_ref, w_ref, b_ref, u_ref,
                 h_ref, ri_ref, rw_ref, cnt_ref, carry_ref, live):
    h = _modulated_norm(x_ref[...], g_ref[...], sc_ref[...], sh_ref[...])
    _store_pitched(h_ref, h)
    def split(v):
        hi = v.astype(BF16)
        return hi, (v - hi.astype(F32)).astype(BF16)

    def nt_dot(a, b):
        return lax.dot_general(a, b, (((1,), (1,)), ((), ())), preferred_element_type=F32)

    w_hi, w_lo = split(w_ref[...])
    h_hi, h_lo = split(h)
    logits = nt_dot(w_hi, h_hi) + (nt_dot(w_hi, h_lo) + nt_dot(w_lo, h_hi)) + b_ref[...]
    tm = logits.shape[1]

    gl = logits[0:N_GROUPS]
    gmax = jnp.max(gl, axis=0, keepdims=True)
    gexp = jnp.exp(gl - gmax)
    gprob = gexp / jnp.sum(gexp, axis=0, keepdims=True)
    p_g = jnp.max(gprob, axis=0, keepdims=True)
    g_top = _first_argmax(gprob, p_g, N_GROUPS)

    e_in = jnp.zeros((EXPERTS_PER_GROUP, tm), F32)
    for g in range(N_GROUPS):
        lo = ROUTER_E0 + g * EXPERTS_PER_GROUP
        e_in = jnp.where(g_top == g, logits[lo:lo + EXPERTS_PER_GROUP], e_in)
    eexp = jnp.exp(e_in - jnp.max(e_in, axis=0, keepdims=True))
    eprob = eexp / jnp.sum(eexp, axis=0, keepdims=True)
    v1 = jnp.max(eprob, axis=0, keepdims=True)
    i1 = _first_argmax(eprob, v1, EXPERTS_PER_GROUP)
    row = lax.broadcasted_iota(I32, eprob.shape, 0)
    rest = jnp.where(row == i1, -1.0, eprob)
    v2 = jnp.max(rest, axis=0, keepdims=True)
    i2 = _first_argmax(rest, v2, EXPERTS_PER_GROUP)
    wsum = v1 + v2
    w1 = v1 / wsum * p_g
    w2 = v2 / wsum * p_g
    e1 = g_top * EXPERTS_PER_GROUP + i1
    e2 = g_top * EXPERTS_PER_GROUP + i2

    erow = lax.broadcasted_iota(I32, (N_EXPERTS, tm), 0)
    oh1 = (erow == e1).astype(F32)
    oh2 = (erow == e2).astype(F32)
    oh = oh1 + oh2
    before = jnp.dot(oh.astype(BF16), u_ref[...], preferred_element_type=F32) + carry_ref[...]
    r1 = jnp.sum(oh1 * before, axis=0, keepdims=True)
    r2 = jnp.sum(oh2 * before, axis=0, keepdims=True)
    carry_ref[...] = carry_ref[...] + live * jnp.sum(oh, axis=1, keepdims=True)
    cnt_ref[...] = jnp.broadcast_to(carry_ref[...], cnt_ref.shape)

    ri_ref[...] = jnp.zeros(ri_ref.shape, I32)
    ri_ref[0:1, :] = e1
    ri_ref[1:2, :] = e2
    ri_ref[2:3, :] = r1.astype(I32)
    ri_ref[3:4, :] = r2.astype(I32)
    rw_ref[...] = jnp.zeros(rw_ref.shape, F32)
    rw_ref[0:1, :] = w1
    rw_ref[1:2, :] = w2


def _outrouter_kernel(oa_ref, ob_ref, oc_ref, go_ref, w_ref, x_ref, g1_ref,
                      g2_ref, sh_ref, sc_ref, wr_ref, br_ref, u_ref,
                      xo_ref, h_ref, ri_ref, rw_ref, cnt_ref,
                      wb_ref, xa_ref, xb_ref, carry_ref):
    i = pl.program_id(0)

    @pl.when(i == 0)
    def _():
        wb_ref[...] = w_ref[0].astype(BF16)
        xb_ref[...] = jnp.zeros(xb_ref.shape, F32)
        carry_ref[...] = jnp.zeros(carry_ref.shape, F32)

    live = (i > 0).astype(F32)

    def step(dst, src):
        x_new = _outproj_body(oa_ref, ob_ref, oc_ref, go_ref, wb_ref, x_ref, g1_ref)
        xo_ref[...] = x_new
        dst[...] = x_new
        _router_body(src, g2_ref, sh_ref, sc_ref, wr_ref, br_ref, u_ref,
                     h_ref, ri_ref, rw_ref, cnt_ref, carry_ref, live)

    @pl.when(i % 2 == 0)
    def _():
        step(xa_ref, xb_ref)

    @pl.when(i % 2 == 1)
    def _():
        step(xb_ref, xa_ref)


def _outrouter(oa, ob, oc, gains, w_out_all, layer, x, mod, norm2_g, w_r, b_r, upper):
    n = x.shape[0]
    tm = upper.shape[0]
    nb = n // tm

    def cur(i):
        return jnp.minimum(i, nb - 1)

    def prev(i):
        return jnp.maximum(i - 1, 0)

    return pl.pallas_call(
        _outrouter_kernel,
        grid=(nb + 1,),
        in_specs=[
            pl.BlockSpec((tm, W_MLA), lambda i: (cur(i), 0)),
            pl.BlockSpec((tm, W_GQA), lambda i: (cur(i), 0)),
            pl.BlockSpec((tm, W_NA), lambda i: (cur(i), 0)),
            pl.BlockSpec((1, D_MODEL), lambda i: (0, 0)),
            pl.BlockSpec((1, D_MODEL, D_MODEL), lambda i: (layer, 0, 0), pipeline_mode=pl.Buffered(1)),
            pl.BlockSpec((tm, D_MODEL), lambda i: (cur(i), 0)),
            pl.BlockSpec((1, D_MODEL), lambda i: (0, 2)),
            pl.BlockSpec((1, D_MODEL), lambda i: (0, 0)),
            pl.BlockSpec((1, D_MODEL), lambda i: (0, 3)),
            pl.BlockSpec((1, D_MODEL), lambda i: (0, 4)),
            pl.BlockSpec((ROUTER_ROWS, D_MODEL), lambda i: (0, 0)),
            pl.BlockSpec((ROUTER_ROWS, 1), lambda i: (0, 0)),
            pl.BlockSpec((tm, tm), lambda i: (0, 0)),
        ],
        out_specs=(
            pl.BlockSpec((tm, D_MODEL), lambda i: (cur(i), 0)),
            pl.BlockSpec((tm * ROW_PITCH, LANES), lambda i: (prev(i), 0)),
            pl.BlockSpec((SUBLANES, tm), lambda i: (0, prev(i))),
            pl.BlockSpec((SUBLANES, tm), lambda i: (0, prev(i))),
            pl.BlockSpec((N_EXPERTS, LANES), lambda i: (0, 0)),
        ),
        out_shape=(
            jax.ShapeDtypeStruct((n, D_MODEL), F32),
            jax.ShapeDtypeStruct((n * ROW_PITCH, LANES), F32),
            jax.ShapeDtypeStruct((SUBLANES, n), I32),
            jax.ShapeDtypeStruct((SUBLANES, n), F32),
            jax.ShapeDtypeStruct((N_EXPERTS, LANES), F32),
        ),
        scratch_shapes=[
            pltpu.VMEM((D_MODEL, D_MODEL), BF16),
            pltpu.VMEM((tm, D_MODEL), F32),
            pltpu.VMEM((tm, D_MODEL), F32),
            pltpu.VMEM((N_EXPERTS, 1), F32),
        ],
        compiler_params=_cparams(1),
        name="outrouter",
    )(oa, ob, oc, gains, w_out_all, x, mod, norm2_g, mod, mod, w_r, b_r, upper)


GATHER_UNROLL = 8


def _row_gather_start(src_hbm, idx_ref, base, dst, sem, n_rows, priority=0):
    def body(r, carry):
        off = idx_ref[base + r]
        pltpu.make_async_copy(src_hbm.at[pl.ds(off, ROW_SLABS)],
                              dst.at[pl.ds(r * ROW_PITCH, ROW_SLABS)], sem).start(priority=priority)
        return carry

    lax.fori_loop(0, n_rows, body, 0, unroll=GATHER_UNROLL)


def _row_gather_wait(src_hbm, dst, sem, n_rows):
    n = n_rows * ROW_SLABS
    pltpu.make_async_copy(src_hbm.at[pl.ds(0, n)], dst.at[pl.ds(0, n)], sem).wait()


BLOCK_SLAB_ROWS = MOE_ROWS * ROW_PITCH
GATHER_SLOTS = 4
GATHER_DEPTH = GATHER_SLOTS


def _expert_kernel(src_ref, bstart_ref, h_hbm, wg_ref, wu_ref, wd_ref, y_hbm,
                   xg_ref, yo_ref, wgb_ref, wub_ref, wdb_ref, gsem, osem):
    e = pl.program_id(0)
    lo, hi = bstart_ref[e], bstart_ref[e + 1]
    total = bstart_ref[N_EXPERTS]

    def out_copy(g, slot):
        row0 = pl.multiple_of(g * BLOCK_SLAB_ROWS, BLOCK_SLAB_ROWS)
        return pltpu.make_async_copy(yo_ref.at[slot], y_hbm.at[pl.ds(row0, BLOCK_SLAB_ROWS)], osem.at[slot])

    @pl.when(e == 0)
    def _():
        for d in range(GATHER_DEPTH):
            @pl.when(d < total)
            def _():
                _row_gather_start(h_hbm, src_ref, d * MOE_ROWS, xg_ref.at[d], gsem.at[d], MOE_ROWS)

    @pl.when(hi > lo)
    def _():
        wgb_ref[...] = wg_ref[0].astype(BF16)
        wub_ref[...] = wu_ref[0].astype(BF16)
        wdb_ref[...] = wd_ref[0].astype(BF16)

    def block(g, carry):
        slot = g % 2
        gslot = g % GATHER_SLOTS

        _row_gather_wait(h_hbm, xg_ref.at[gslot], gsem.at[gslot], MOE_ROWS)
        xs = xg_ref.at[gslot]
        x = jnp.concatenate(
            [_load_pitched_slab(xs, s, MOE_ROWS).astype(BF16) for s in range(ROW_SLABS)], axis=1)
        gate = jnp.dot(x, wgb_ref[...], preferred_element_type=F32)
        up = jnp.dot(x, wub_ref[...], preferred_element_type=F32)
        a = (gate * (1.0 / (1.0 + jnp.exp(-gate))) * up).astype(BF16)
        y = jnp.dot(a, wdb_ref[...], preferred_element_type=F32)

        @pl.when(g + GATHER_DEPTH < total)
        def _():
            nslot = (g + GATHER_DEPTH) % GATHER_SLOTS
            _row_gather_start(h_hbm, src_ref, (g + GATHER_DEPTH) * MOE_ROWS, xg_ref.at[nslot], gsem.at[nslot],
                              MOE_ROWS)

        @pl.when(g >= 2)
        def _():
            out_copy(g - 2, slot).wait()

        _store_pitched(yo_ref.at[slot], y)
        out_copy(g, slot).start()
        return carry

    lax.fori_loop(lo, hi, block, 0)

    @pl.when(e == N_EXPERTS - 1)
    def _():
        @pl.when(total >= 2)
        def _():
            out_copy(total - 2, total % 2).wait()

        @pl.when(total >= 1)
        def _():
            out_copy(total - 1, (total - 1) % 2).wait()

        n_blocks = y_hbm.shape[0] // BLOCK_SLAB_ROWS
        yo_ref[0] = jnp.zeros(yo_ref.shape[1:], F32)

        def zero_start(g, carry):
            out_copy(g, 0).start()
            return carry

        def zero_wait(g, carry):
            out_copy(g, 0).wait()
            return carry

        lax.fori_loop(total, n_blocks, zero_start, 0)
        lax.fori_loop(total, n_blocks, zero_wait, 0)


def _experts(src_off, bstart, h, w_gate, w_up, w_down, layer, n_blocks):
    base = layer * N_EXPERTS
    grid_spec = pltpu.PrefetchScalarGridSpec(
        num_scalar_prefetch=2,
        grid=(N_EXPERTS,),
        in_specs=[
            pl.BlockSpec(memory_space=pl.ANY),
            pl.BlockSpec((1, D_MODEL, D_EXPERT), lambda e, src, bs: (base + e, 0, 0)),
            pl.BlockSpec((1, D_MODEL, D_EXPERT), lambda e, src, bs: (base + e, 0, 0)),
            pl.BlockSpec((1, D_EXPERT, D_MODEL), lambda e, src, bs: (base + e, 0, 0)),
        ],
        out_specs=pl.BlockSpec(memory_space=pl.ANY),
        scratch_shapes=[
            pltpu.VMEM((GATHER_SLOTS, BLOCK_SLAB_ROWS, LANES), F32),
            pltpu.VMEM((2, BLOCK_SLAB_ROWS, LANES), F32),
            pltpu.VMEM((D_MODEL, D_EXPERT), BF16),
            pltpu.VMEM((D_MODEL, D_EXPERT), BF16),
            pltpu.VMEM((D_EXPERT, D_MODEL), BF16),
            pltpu.SemaphoreType.DMA((GATHER_SLOTS,)),
            pltpu.SemaphoreType.DMA((2,)),
        ],
    )
    return pl.pallas_call(
        _expert_kernel,
        grid_spec=grid_spec,
        out_shape=jax.ShapeDtypeStruct((n_blocks * BLOCK_SLAB_ROWS, LANES), F32),
        compiler_params=_cparams(1),
        name="experts",
    )(src_off, bstart, h, w_gate, w_up, w_down)


COMB_TM = 256


def _combine_kernel(d1_ref, d2_ref, y_hbm, x_ref, w1_ref, w2_ref, g2_ref, o_ref, ya_ref, yb_ref, sem):
    i = pl.program_id(0)
    n = pl.num_programs(0)
    slot = i % 2

    def start(blk, s):
        _row_gather_start(y_hbm, d1_ref, blk * COMB_TM, ya_ref.at[s], sem.at[0, s], COMB_TM)
        _row_gather_start(y_hbm, d2_ref, blk * COMB_TM, yb_ref.at[s], sem.at[1, s], COMB_TM, priority=1)

    @pl.when(i == 0)
    def _():
        start(0, 0)

    @pl.when(i + 1 < n)
    def _():
        start(i + 1, 1 - slot)

    _row_gather_wait(y_hbm, ya_ref.at[slot], sem.at[0, slot], COMB_TM)
    _row_gather_wait(y_hbm, yb_ref.at[slot], sem.at[1, slot], COMB_TM)
    w1, w2 = w1_ref[...], w2_ref[...]
    ya, yb = ya_ref.at[slot], yb_ref.at[slot]
    for s in range(ROW_SLABS):
        cols = slice(s * LANES, (s + 1) * LANES)
        y = w1 * _load_pitched_slab(ya, s, COMB_TM) + w2 * _load_pitched_slab(yb, s, COMB_TM)
        o_ref[:, cols] = x_ref[:, cols] + g2_ref[:, cols] * y


def _combine(dest1, dest2, ybuf, x, w1, w2, mod):
    n = x.shape[0]
    grid_spec = pltpu.PrefetchScalarGridSpec(
        num_scalar_prefetch=2,
        grid=(n // COMB_TM,),
        in_specs=[
            pl.BlockSpec(memory_space=pl.ANY),
            pl.BlockSpec((COMB_TM, D_MODEL), lambda i, d1, d2: (i, 0)),
            pl.BlockSpec((COMB_TM, 1), lambda i, d1, d2: (i, 0)),
            pl.BlockSpec((COMB_TM, 1), lambda i, d1, d2: (i, 0)),
            pl.BlockSpec((1, D_MODEL), lambda i, d1, d2: (0, 5)),
        ],
        out_specs=pl.BlockSpec((COMB_TM, D_MODEL), lambda i, d1, d2: (i, 0)),
        scratch_shapes=[
            pltpu.VMEM((2, COMB_TM * ROW_PITCH, LANES), F32),
            pltpu.VMEM((2, COMB_TM * ROW_PITCH, LANES), F32),
            pltpu.SemaphoreType.DMA((2, 2)),
        ],
    )
    return pl.pallas_call(
        _combine_kernel,
        grid_spec=grid_spec,
        out_shape=jax.ShapeDtypeStruct((n, D_MODEL), F32),
        compiler_params=_cparams(1),
        name="combine",
    )(dest1, dest2, ybuf, x, w1, w2, mod)


def _rope_angles(pos, d):
    f32 = np.float32
    inv = f32(ROPE_THETA) ** (-np.arange(0, d, 2, dtype=f32) / f32(d))
    return pos.astype(f32)[:, None] * inv[None, :]


def _rope_tables(S):
    t = np.arange(S)
    z = np.zeros((S, MLA_ROPE), np.float32)
    a = _rope_angles(t, MLA_ROPE)
    cm = np.concatenate([np.cos(a), np.cos(a), z], axis=1)
    sm = np.concatenate([-np.sin(a), np.sin(a), z], axis=1)
    ar = _rope_angles(t // GRID_W, HEAD_DIM // 2)
    ac = _rope_angles(t % GRID_W, HEAD_DIM // 2)
    ca = np.concatenate([np.cos(ar), np.cos(ar), np.cos(ac), np.cos(ac)], axis=1)
    sa = np.concatenate([-np.sin(ar), np.sin(ar), -np.sin(ac), np.sin(ac)], axis=1)
    return tuple(jnp.asarray(v, F32) for v in (cm, sm, ca, sa))


def _pad_lanes(a, width):
    return jnp.concatenate([a, jnp.zeros(a.shape[:-1] + (width - a.shape[-1],), a.dtype)], axis=-1)


def _layer_weights(w_in, w_uq, w_ukv, qk_q_g, qk_k_g):
    split = C_APE + MLA_ROPE
    w_in_p = jnp.concatenate(
        [w_in[:, :split], jnp.zeros((D_MODEL, PE_PAD - MLA_ROPE), F32), w_in[:, split:]], axis=1
    ).astype(BF16)
    wq = w_uq.reshape(MLA_Q_RANK, MLA_HEADS, MLA_QK)
    wq_n = wq[:, :, :MLA_NOPE].reshape(MLA_Q_RANK, MLA_HEADS * LANES)
    wq_r = _pad_lanes(wq[:, :, MLA_NOPE:], LANES).reshape(MLA_Q_RANK, MLA_HEADS * LANES)
    wq_p = jnp.concatenate([wq_n, wq_r], axis=1).astype(BF16)
    wkv = w_ukv.reshape(MLA_KV_RANK, MLA_HEADS, MLA_NOPE + MLA_V)
    wkv_p = jnp.concatenate(
        [wkv[:, :, :MLA_NOPE].reshape(MLA_KV_RANK, -1), wkv[:, :, MLA_NOPE:].reshape(MLA_KV_RANK, -1)],
        axis=1).astype(BF16)
    gmq = _pad_lanes(qk_q_g[None, :], MLA_QK_PAD)
    gmk = _pad_lanes(qk_k_g[None, :], MLA_QK_PAD)
    return w_in_p, wq_p, wkv_p, gmq, gmk


def _slot_kernel(start_ref, ri_ref, o_ref):
    e = ri_ref[0:2, :]
    first = jnp.zeros(e.shape, I32)
    for k in range(N_EXPERTS):
        first = jnp.where(e == k, start_ref[k], first)
    slot = first + ri_ref[2:4, :]
    o_ref[...] = jnp.zeros(o_ref.shape, I32)
    o_ref[0:2, :] = slot
    o_ref[2:4, :] = slot * ROW_PITCH


def _slots(pad_start, ri):
    return pl.pallas_call(
        _slot_kernel,
        grid=(1,),
        in_specs=[pl.BlockSpec(memory_space=pltpu.SMEM), pl.BlockSpec(ri.shape, lambda i: (0, 0))],
        out_specs=pl.BlockSpec(ri.shape, lambda i: (0, 0)),
        out_shape=jax.ShapeDtypeStruct(ri.shape, I32),
        compiler_params=_cparams(1),
        name="slots",
    )(pad_start, ri)


def _routing_plan(ri, cnt, n_blocks):
    n = ri.shape[1]
    counts = cnt[:, 0].astype(I32)
    padded = ((counts + MOE_ROWS - 1) // MOE_ROWS) * MOE_ROWS
    pad_end = jnp.cumsum(padded)
    pad_start = pad_end - padded
    slots = _slots(pad_start, ri)
    dest1, dest2 = slots[0], slots[1]
    tok_off = jnp.arange(n, dtype=I32) * ROW_PITCH
    pad_src = (jnp.arange(n_blocks * MOE_ROWS, dtype=I32) % n) * ROW_PITCH
    src_off = pad_src.at[jnp.concatenate([dest1, dest2])].set(
        jnp.concatenate([tok_off, tok_off]), unique_indices=True)
    bstart = jnp.concatenate([jnp.zeros((1,), I32), pad_end // MOE_ROWS]).astype(I32)
    return slots[2], slots[3], src_off, bstart


def kernel(x, c, ada_w, ada_b, norm1_g, norm2_g, w_in, mla_q_norm_g, mla_w_uq, mla_kv_norm_g, mla_w_ukv, mla_qk_q_g, mla_qk_k_g, gqa_q_g, gqa_k_g, na_q_g, na_k_g, na_rpb, mix_out_norm_g, w_out, router_group_w, router_group_b, router_expert_w, router_expert_b, expert_w_gate, expert_w_up, expert_w_down):
    B, S, D = x.shape
    assert B == 1 and D == D_MODEL and S % (GRID_W * NA_QROWS) == 0 and S // GRID_W >= NA_KROWS
    depth = ada_w.shape[0]
    n = B * S
    xt = x.reshape(n, D)

    mod_all = _adaln(c, ada_w, ada_b)
    na_bias = _na_bias(na_rpb)
    cm, sm, ca, sa = _rope_tables(S)
    tm_r = min(ROUTER_TM, n)
    upper = jnp.asarray(np.arange(tm_r)[:, None] < np.arange(tm_r)[None, :], BF16)
    n_blocks = -(-(n * 2 + N_EXPERTS * (MOE_ROWS - 1)) // MOE_ROWS)
    wg_all = expert_w_gate.reshape(depth * N_EXPERTS, D_MODEL, D_EXPERT)
    wu_all = expert_w_up.reshape(depth * N_EXPERTS, D_MODEL, D_EXPERT)
    wd_all = expert_w_down.reshape(depth * N_EXPERTS, D_EXPERT, D_MODEL)

    def row(v):
        return v[None, :]

    for l in range(depth):
        mod = mod_all[l]
        w_in_p, wq_p, wkv_p, gmq, gmk = _layer_weights(
            w_in[l], mla_w_uq[l], mla_w_ukv[l], mla_qk_q_g[l], mla_qk_k_g[l])

        qm, km, vmt, qg, kg, vgt, qn, kn, vn = _mixin(
            xt, row(norm1_g[l]), mod, w_in_p, wq_p, wkv_p, row(mla_q_norm_g[l]), row(mla_kv_norm_g[l]), gmq, gmk,
            row(gqa_q_g[l]), row(gqa_k_g[l]), row(na_q_g[l]), row(na_k_g[l]), cm, sm, ca, sa)
        o_a = _flash(qm, km, vmt)
        o_b = _flash(qg, kg, vgt)
        o_c = _na(qn, kn, vn, na_bias, l)
        w_r = jnp.concatenate(
            [router_group_w[l].T, jnp.zeros((ROUTER_E0 - N_GROUPS, D), F32), router_expert_w[l].T], axis=0)
        b_r = jnp.concatenate(
            [router_group_b[l], jnp.zeros((ROUTER_E0 - N_GROUPS,), F32), router_expert_b[l]])[:, None]
        xt, h2, ri, rw, cnt = _outrouter(o_a, o_b, o_c, row(mix_out_norm_g[l]), w_out, l, xt, mod,
                                         row(norm2_g[l]), w_r, b_r, upper)
        off1, off2, src_off, bstart = _routing_plan(ri, cnt, n_blocks)
        ybuf = _experts(src_off, bstart, h2, wg_all, wu_all, wd_all, l, n_blocks)
        xt = _combine(off1, off2, ybuf, xt, rw[0][:, None], rw[1][:, None], mod)

    return xt.reshape(B, S, D)
```
